```python
import jax, jax.numpy as jnp
from jax import lax
import numpy as np

D_MODEL = 2048
BATCH = 8
SEQ = 8192
DEPTH = 1

CTX_LEN = 256
GRID_W = 64
HEAD_DIM = 128
N_Q_HEADS = 16
N_KV_HEADS = 4
Q_PER_KV = N_Q_HEADS // N_KV_HEADS
Q_BLOCK = 128
AXIS_DIM = HEAD_DIM // 2
ROPE_THETA = 10000.0
ATTN_SCALE = HEAD_DIM ** -0.5
GMLP_GROUPS = 16
GMLP_WIDTH = 2048
GMLP_GROUP_DIM = GMLP_WIDTH // GMLP_GROUPS
CHUNK = 128
D_FF = 5632
MACARON_WEIGHT = 0.5
N_MOD = 9
EPS = 1e-6

Q_W = N_Q_HEADS * HEAD_DIM
KV_W = N_KV_HEADS * HEAD_DIM
Q_END = Q_W
K_END = Q_END + KV_W
V_END = K_END + KV_W
GV_END = V_END + 2 * GMLP_WIDTH
IN_W = GV_END + 2 * D_MODEL

kernel_name = "hybrid_gqa_gmlp_macaron_dit_layer"


def rmsnorm(x, w):
    xf = x.astype(jnp.float32)
    y = xf * lax.rsqrt(jnp.mean(xf * xf, axis=-1, keepdims=True) + EPS)
    return (y * w.astype(jnp.float32)).astype(x.dtype)


def layernorm(x, w, b):
    xf = x.astype(jnp.float32)
    mu = jnp.mean(xf, axis=-1, keepdims=True)
    xc = xf - mu
    y = xc * lax.rsqrt(jnp.mean(xc * xc, axis=-1, keepdims=True) + EPS)
    return (y * w.astype(jnp.float32) + b.astype(jnp.float32)).astype(x.dtype)


def modulate(h, shift, scale):
    return h * (1 + scale) + shift


def axial_rope(n_rows):
    row = jnp.broadcast_to(jnp.arange(n_rows, dtype=jnp.float32)[:, None], (n_rows, GRID_W)).reshape(-1)
    col = jnp.broadcast_to(jnp.arange(GRID_W, dtype=jnp.float32)[None, :], (n_rows, GRID_W)).reshape(-1)
    inv_freq = ROPE_THETA ** (-jnp.arange(0, AXIS_DIM, 2, dtype=jnp.float32) / AXIS_DIM)
    ang = jnp.concatenate([row[:, None] * inv_freq, col[:, None] * inv_freq], axis=-1)
    return jnp.cos(ang), jnp.sin(ang)


def apply_rope(x, cos, sin):
    B, S, H, Dh = x.shape
    xr = x.astype(jnp.float32).reshape(B, S, H, Dh // 2, 2)
    x1, x2 = xr[..., 0], xr[..., 1]
    cs, sn = cos[None, :, None, :], sin[None, :, None, :]
    out = jnp.stack([x1 * cs - x2 * sn, x1 * sn + x2 * cs], axis=-1)
    return out.reshape(B, S, H, Dh).astype(x.dtype)


def heads_norm(z, n_heads, gain):
    return rmsnorm(z.reshape(z.shape[0], z.shape[1], n_heads, HEAD_DIM), gain)


def gqa_attend(qi, k_all, v_all):
    s = jnp.einsum('bqkgd,bskd->bkgqs', qi, k_all, preferred_element_type=jnp.float32) * ATTN_SCALE
    p = jax.nn.softmax(s, axis=-1).astype(v_all.dtype)
    return jnp.einsum('bkgqs,bskd->bqkgd', p, v_all)


def latent_attention(q, k, v, k_ctx, v_ctx):
    B, S = q.shape[:2]
    k_all = jnp.concatenate([k_ctx, k], axis=1)
    v_all = jnp.concatenate([v_ctx, v], axis=1)
    qb = q.reshape(B, S // Q_BLOCK, Q_BLOCK, N_KV_HEADS, Q_PER_KV, HEAD_DIM).swapaxes(0, 1)
    o = lax.map(lambda qi: gqa_attend(qi, k_all, v_all), qb)
    return o.swapaxes(0, 1).reshape(B, S, Q_W)


def context_attention(q, k, v):
    B, C = q.shape[:2]
    o = gqa_attend(q.reshape(B, C, N_KV_HEADS, Q_PER_KV, HEAD_DIM), k, v)
    return o.reshape(B, C, Q_W)


def gmlp_branch(z_uv, ln_w, ln_b, w_s, b_s):
    B, N, _ = z_uv.shape
    z = jax.nn.gelu(z_uv, approximate=False)
    u, v = z[..., :GMLP_WIDTH], z[..., GMLP_WIDTH:]
    vn = layernorm(v, ln_w, ln_b).reshape(B, N // CHUNK, CHUNK, GMLP_GROUPS, GMLP_GROUP_DIM)
    mixed = jnp.einsum('gpq,bcqgd->bcpgd', w_s, vn) + b_s.T[:, :, None]
    return u * mixed.reshape(B, N, GMLP_WIDTH)


def merge_branches(attn, gm, gate_logits, b_gate_l, w_ba, w_bg, w_o):
    g = jax.nn.sigmoid(gate_logits.reshape(*gate_logits.shape[:-1], 2, D_MODEL) + b_gate_l)
    return (g[..., 0, :] * (attn @ w_ba) + g[..., 1, :] * (gm @ w_bg)) @ w_o


def ffn_sublayer(x, shift, scale, gate, norm_w, w_in, w_out):
    h = modulate(rmsnorm(x, norm_w), shift, scale)
    a, b = jnp.split(h @ w_in, 2, axis=-1)
    return x + MACARON_WEIGHT * gate * ((jax.nn.silu(a) * b) @ w_out)


def _normal(key, shape, scale):
    return jax.random.normal(key, shape, jnp.float32) * scale


def _fwd_setup_inputs(seed: int = 0) -> dict:
    key = jax.random.key(seed)
    ks = jax.random.split(key, 24)
    L, D, F = DEPTH, D_MODEL, D_FF
    return {
        "x": _normal(ks[0], (BATCH, SEQ, D), 1.0),
        "c": _normal(ks[1], (BATCH, D), 1.0),
        "ctx": _normal(ks[2], (BATCH, CTX_LEN, D), 1.0),
        "c_ctx": _normal(ks[3], (D,), 1.0),
        "w_mod": _normal(ks[4], (L, D, N_MOD * D), 0.5 * D ** -0.5),
        "b_mod": _normal(ks[5], (L, N_MOD * D), 0.02),
        "norm_w": 1.0 + _normal(ks[6], (L, 3, D), 0.05),
        "w_ffn1_in": _normal(ks[7], (L, D, 2 * F), D ** -0.5),
        "w_ffn1_out": _normal(ks[8], (L, F, D), F ** -0.5),
        "w_ffn2_in": _normal(ks[9], (L, D, 2 * F), D ** -0.5),
        "w_ffn2_out": _normal(ks[10], (L, F, D), F ** -0.5),
        "w_in": _normal(ks[11], (L, D, IN_W), D ** -0.5),
        "b_gate": _normal(ks[12], (L, 2, D), 0.1),
        "q_norm_w": 1.0 + _normal(ks[13], (L, HEAD_DIM), 0.05),
        "k_norm_w": 1.0 + _normal(ks[14], (L, HEAD_DIM), 0.05),
        "gmlp_ln_w": 1.0 + _normal(ks[15], (L, GMLP_WIDTH), 0.05),
        "gmlp_ln_b": _normal(ks[16], (L, GMLP_WIDTH), 0.02),
        "w_spatial": _normal(ks[17], (L, GMLP_GROUPS, CHUNK, CHUNK), 0.5 * CHUNK ** -0.5),
        "b_spatial": 1.0 + _normal(ks[18], (L, GMLP_GROUPS, CHUNK), 0.1),
        "w_branch_attn": _normal(ks[19], (L, Q_W, D), Q_W ** -0.5),
        "w_branch_gmlp": _normal(ks[20], (L, GMLP_WIDTH, D), GMLP_WIDTH ** -0.5),
        "w_out": _normal(ks[21], (L, D, D), D ** -0.5),
        "final_norm_w": 1.0 + _normal(ks[22], (D,), 0.05),
    }


def _fwd_reference(x, c, ctx, c_ctx, w_mod, b_mod, norm_w, w_ffn1_in, w_ffn1_out, w_ffn2_in, w_ffn2_out,
              w_in, b_gate, q_norm_w, k_norm_w, gmlp_ln_w, gmlp_ln_b, w_spatial, b_spatial,
              w_branch_attn, w_branch_gmlp, w_out, final_norm_w):
    B, S, D = x.shape
    rows = S // GRID_W
    cos, sin = axial_rope(rows)
    sc = jax.nn.silu(c)
    scc = jax.nn.silu(c_ctx)
    for l in range(DEPTH):
        mx = (sc @ w_mod[l] + b_mod[l]).reshape(B, N_MOD, 1, D)
        mc = (scc @ w_mod[l] + b_mod[l]).reshape(1, N_MOD, 1, D)

        x = ffn_sublayer(x, mx[:, 0], mx[:, 1], mx[:, 2], norm_w[l, 0], w_ffn1_in[l], w_ffn1_out[l])
        ctx = ffn_sublayer(ctx, mc[:, 0], mc[:, 1], mc[:, 2], norm_w[l, 0], w_ffn1_in[l], w_ffn1_out[l])

        hx = modulate(rmsnorm(x, norm_w[l, 1]), mx[:, 3], mx[:, 4])
        hc = modulate(rmsnorm(ctx, norm_w[l, 1]), mc[:, 3], mc[:, 4])
        zx = hx @ w_in[l]
        qx = apply_rope(heads_norm(zx[..., :Q_END], N_Q_HEADS, q_norm_w[l]), cos, sin)
        kx = apply_rope(heads_norm(zx[..., Q_END:K_END], N_KV_HEADS, k_norm_w[l]), cos, sin)
        vx = zx[..., K_END:V_END].reshape(B, S, N_KV_HEADS, HEAD_DIM)

        zc_kv = hc @ w_in[l][:, Q_END:V_END]
        kc = heads_norm(zc_kv[..., :KV_W], N_KV_HEADS, k_norm_w[l])
        vc = zc_kv[..., KV_W:].reshape(B, hc.shape[1], N_KV_HEADS, HEAD_DIM)

        attn_x = latent_attention(qx, kx, vx, kc, vc)
        gm_x = gmlp_branch(zx[..., V_END:GV_END], gmlp_ln_w[l], gmlp_ln_b[l], w_spatial[l], b_spatial[l])
        y = merge_branches(attn_x, gm_x, zx[..., GV_END:], b_gate[l],
                           w_branch_attn[l], w_branch_gmlp[l], w_out[l])

        if l < DEPTH - 1:
            qc = heads_norm(hc @ w_in[l][:, :Q_END], N_Q_HEADS, q_norm_w[l])
            zc_rest = hc @ w_in[l][:, V_END:]
            attn_c = context_attention(qc, kc, vc)
            gm_c = gmlp_branch(zc_rest[..., :2 * GMLP_WIDTH], gmlp_ln_w[l], gmlp_ln_b[l],
                               w_spatial[l], b_spatial[l])
            yc = merge_branches(attn_c, gm_c, zc_rest[..., 2 * GMLP_WIDTH:], b_gate[l],
                                w_branch_attn[l], w_branch_gmlp[l], w_out[l])
            ctx = ctx + mc[:, 5] * yc
            ctx = ffn_sublayer(ctx, mc[:, 6], mc[:, 7], mc[:, 8], norm_w[l, 2], w_ffn2_in[l], w_ffn2_out[l])

        x = x + mx[:, 5] * y
        x = ffn_sublayer(x, mx[:, 6], mx[:, 7], mx[:, 8], norm_w[l, 2], w_ffn2_in[l], w_ffn2_out[l])
    return rmsnorm(x, final_norm_w)


import jax as _jax
import jax.numpy as _jnp

TWIN_FORMAT = 'train_step'
FWD_PARAMS = ['x', 'c', 'ctx', 'c_ctx', 'w_mod', 'b_mod', 'norm_w', 'w_ffn1_in', 'w_ffn1_out', 'w_ffn2_in', 'w_ffn2_out', 'w_in', 'b_gate', 'q_norm_w', 'k_norm_w', 'gmlp_ln_w', 'gmlp_ln_b', 'w_spatial', 'b_spatial', 'w_branch_attn', 'w_branch_gmlp', 'w_out', 'final_norm_w']
TWIN_WEIGHTS = ['c_ctx', 'w_mod', 'b_mod', 'norm_w', 'w_ffn1_in', 'w_ffn1_out', 'w_ffn2_in', 'w_ffn2_out', 'w_in', 'b_gate', 'q_norm_w', 'k_norm_w', 'gmlp_ln_w', 'gmlp_ln_b', 'w_spatial', 'b_spatial', 'w_branch_attn', 'w_branch_gmlp', 'w_out', 'final_norm_w']
TWIN_DIFF_INPUT = 'x'
TWIN_INPUTS = ['x', 'c', 'ctx', 'c_ctx', 'w_mod', 'b_mod', 'norm_w', 'w_ffn1_in', 'w_ffn1_out', 'w_ffn2_in', 'w_ffn2_out', 'w_in', 'b_gate', 'q_norm_w', 'k_norm_w', 'gmlp_ln_w', 'gmlp_ln_b', 'w_spatial', 'b_spatial', 'w_branch_attn', 'w_branch_gmlp', 'w_out', 'final_norm_w', 'loss_target', 'm_c_ctx', 'm_w_mod', 'm_b_mod', 'm_norm_w', 'm_w_ffn1_in', 'm_w_ffn1_out', 'm_w_ffn2_in', 'm_w_ffn2_out', 'm_w_in', 'm_b_gate', 'm_q_norm_w', 'm_k_norm_w', 'm_gmlp_ln_w', 'm_gmlp_ln_b', 'm_w_spatial', 'm_b_spatial', 'm_w_branch_attn', 'm_w_branch_gmlp', 'm_w_out', 'm_final_norm_w', 'v_c_ctx', 'v_w_mod', 'v_b_mod', 'v_norm_w', 'v_w_ffn1_in', 'v_w_ffn1_out', 'v_w_ffn2_in', 'v_w_ffn2_out', 'v_w_in', 'v_b_gate', 'v_q_norm_w', 'v_k_norm_w', 'v_gmlp_ln_w', 'v_gmlp_ln_b', 'v_w_spatial', 'v_b_spatial', 'v_w_branch_attn', 'v_w_branch_gmlp', 'v_w_out', 'v_final_norm_w']
TWIN_OUTPUTS = ['loss', 'grad_x', 'grad_c_ctx', 'grad_w_mod', 'grad_b_mod', 'grad_norm_w', 'grad_w_ffn1_in', 'grad_w_ffn1_out', 'grad_w_ffn2_in', 'grad_w_ffn2_out', 'grad_w_in', 'grad_b_gate', 'grad_q_norm_w', 'grad_k_norm_w', 'grad_gmlp_ln_w', 'grad_gmlp_ln_b', 'grad_w_spatial', 'grad_b_spatial', 'grad_w_branch_attn', 'grad_w_branch_gmlp', 'grad_w_out', 'grad_final_norm_w', 'delta_c_ctx', 'delta_w_mod', 'delta_b_mod', 'delta_norm_w', 'delta_w_ffn1_in', 'delta_w_ffn1_out', 'delta_w_ffn2_in', 'delta_w_ffn2_out', 'delta_w_in', 'delta_b_gate', 'delta_q_norm_w', 'delta_k_norm_w', 'delta_gmlp_ln_w', 'delta_gmlp_ln_b', 'delta_w_spatial', 'delta_b_spatial', 'delta_w_branch_attn', 'delta_w_branch_gmlp', 'delta_w_out', 'delta_final_norm_w', 'new_m_c_ctx', 'new_m_w_mod', 'new_m_b_mod', 'new_m_norm_w', 'new_m_w_ffn1_in', 'new_m_w_ffn1_out', 'new_m_w_ffn2_in', 'new_m_w_ffn2_out', 'new_m_w_in', 'new_m_b_gate', 'new_m_q_norm_w', 'new_m_k_norm_w', 'new_m_gmlp_ln_w', 'new_m_gmlp_ln_b', 'new_m_w_spatial', 'new_m_b_spatial', 'new_m_w_branch_attn', 'new_m_w_branch_gmlp', 'new_m_w_out', 'new_m_final_norm_w', 'new_v_c_ctx', 'new_v_w_mod', 'new_v_b_mod', 'new_v_norm_w', 'new_v_w_ffn1_in', 'new_v_w_ffn1_out', 'new_v_w_ffn2_in', 'new_v_w_ffn2_out', 'new_v_w_in', 'new_v_b_gate', 'new_v_q_norm_w', 'new_v_k_norm_w', 'new_v_gmlp_ln_w', 'new_v_gmlp_ln_b', 'new_v_w_spatial', 'new_v_b_spatial', 'new_v_w_branch_attn', 'new_v_w_branch_gmlp', 'new_v_w_out', 'new_v_final_norm_w']
TWIN_LEAF_KINDS = {'loss': 'loss', 'grad_x': 'grad_x', 'grad_c_ctx': 'grad_w', 'grad_w_mod': 'grad_w', 'grad_b_mod': 'grad_w', 'grad_norm_w': 'grad_w', 'grad_w_ffn1_in': 'grad_w', 'grad_w_ffn1_out': 'grad_w', 'grad_w_ffn2_in': 'grad_w', 'grad_w_ffn2_out': 'grad_w', 'grad_w_in': 'grad_w', 'grad_b_gate': 'grad_w', 'grad_q_norm_w': 'grad_w', 'grad_k_norm_w': 'grad_w', 'grad_gmlp_ln_w': 'grad_w', 'grad_gmlp_ln_b': 'grad_w', 'grad_w_spatial': 'grad_w', 'grad_b_spatial': 'grad_w', 'grad_w_branch_attn': 'grad_w', 'grad_w_branch_gmlp': 'grad_w', 'grad_w_out': 'grad_w', 'grad_final_norm_w': 'grad_w', 'delta_c_ctx': 'delta_w', 'delta_w_mod': 'delta_w', 'delta_b_mod': 'delta_w', 'delta_norm_w': 'delta_w', 'delta_w_ffn1_in': 'delta_w', 'delta_w_ffn1_out': 'delta_w', 'delta_w_ffn2_in': 'delta_w', 'delta_w_ffn2_out': 'delta_w', 'delta_w_in': 'delta_w', 'delta_b_gate': 'delta_w', 'delta_q_norm_w': 'delta_w', 'delta_k_norm_w': 'delta_w', 'delta_gmlp_ln_w': 'delta_w', 'delta_gmlp_ln_b': 'delta_w', 'delta_w_spatial': 'delta_w', 'delta_b_spatial': 'delta_w', 'delta_w_branch_attn': 'delta_w', 'delta_w_branch_gmlp': 'delta_w', 'delta_w_out': 'delta_w', 'delta_final_norm_w': 'delta_w', 'new_m_c_ctx': 'new_m', 'new_m_w_mod': 'new_m', 'new_m_b_mod': 'new_m', 'new_m_norm_w': 'new_m', 'new_m_w_ffn1_in': 'new_m', 'new_m_w_ffn1_out': 'new_m', 'new_m_w_ffn2_in': 'new_m', 'new_m_w_ffn2_out': 'new_m', 'new_m_w_in': 'new_m', 'new_m_b_gate': 'new_m', 'new_m_q_norm_w': 'new_m', 'new_m_k_norm_w': 'new_m', 'new_m_gmlp_ln_w': 'new_m', 'new_m_gmlp_ln_b': 'new_m', 'new_m_w_spatial': 'new_m', 'new_m_b_spatial': 'new_m', 'new_m_w_branch_attn': 'new_m', 'new_m_w_branch_gmlp': 'new_m', 'new_m_w_out': 'new_m', 'new_m_final_norm_w': 'new_m', 'new_v_c_ctx': 'new_v', 'new_v_w_mod': 'new_v', 'new_v_b_mod': 'new_v', 'new_v_norm_w': 'new_v', 'new_v_w_ffn1_in': 'new_v', 'new_v_w_ffn1_out': 'new_v', 'new_v_w_ffn2_in': 'new_v', 'new_v_w_ffn2_out': 'new_v', 'new_v_w_in': 'new_v', 'new_v_b_gate': 'new_v', 'new_v_q_norm_w': 'new_v', 'new_v_k_norm_w': 'new_v', 'new_v_gmlp_ln_w': 'new_v', 'new_v_gmlp_ln_b': 'new_v', 'new_v_w_spatial': 'new_v', 'new_v_b_spatial': 'new_v', 'new_v_w_branch_attn': 'new_v', 'new_v_w_branch_gmlp': 'new_v', 'new_v_w_out': 'new_v', 'new_v_final_norm_w': 'new_v'}


def _forward(args):
    return _fwd_reference(*[args[k] for k in FWD_PARAMS])


def _output_shape():
    def fwd():
        inp = _fwd_setup_inputs(0)
        return _fwd_reference(*[inp[k] for k in FWD_PARAMS])
    out = _jax.eval_shape(fwd)
    return out.shape, out.dtype

N_MICROBATCH = 1
ADAM_LR = 0.001
ADAM_B1 = 0.9
ADAM_B2 = 0.999
ADAM_EPS = 1e-08
ADAM_WD = 0.01
ADAM_STEP = 10
PER_EXAMPLE_BATCH_AXIS = {'x': 0, 'c': 0, 'ctx': 0, 'loss_target': 0}
SHARED_INPUTS = []
_WEIGHT_DTYPES = {'c_ctx': _jnp.float32, 'w_mod': _jnp.float32, 'b_mod': _jnp.float32, 'norm_w': _jnp.float32, 'w_ffn1_in': _jnp.float32, 'w_ffn1_out': _jnp.float32, 'w_ffn2_in': _jnp.float32, 'w_ffn2_out': _jnp.float32, 'w_in': _jnp.float32, 'b_gate': _jnp.float32, 'q_norm_w': _jnp.float32, 'k_norm_w': _jnp.float32, 'gmlp_ln_w': _jnp.float32, 'gmlp_ln_b': _jnp.float32, 'w_spatial': _jnp.float32, 'b_spatial': _jnp.float32, 'w_branch_attn': _jnp.float32, 'w_branch_gmlp': _jnp.float32, 'w_out': _jnp.float32, 'final_norm_w': _jnp.float32}
MOMENT_SCALE = {'c_ctx': 3.405436e-03, 'w_mod': 2.283987e-02, 'b_mod': 4.213071e-02, 'norm_w': 1.881585e-02, 'w_ffn1_in': 8.274426e-03, 'w_ffn1_out': 1.352071e-02, 'w_ffn2_in': 8.149201e-03, 'w_ffn2_out': 1.326901e-02, 'w_in': 9.120935e-03, 'b_gate': 4.929325e-03, 'q_norm_w': 7.065394e-03, 'k_norm_w': 6.938186e-03, 'gmlp_ln_w': 7.510625e-03, 'gmlp_ln_b': 7.353458e-03, 'w_spatial': 1.459415e-02, 'b_spatial': 1.503814e-02, 'w_branch_attn': 7.367948e-03, 'w_branch_gmlp': 1.745914e-02, 'w_out': 1.882372e-02, 'final_norm_w': 3.197030e+01}


def _to_microbatches(a, axis):
    t = _jnp.moveaxis(a, axis, 0)
    t = t.reshape((N_MICROBATCH, t.shape[0] // N_MICROBATCH) + t.shape[1:])
    return _jnp.moveaxis(t, 1, axis + 1)


def setup_inputs(seed: int = 0) -> dict:
    inp = _fwd_setup_inputs(seed)
    key = _jax.random.fold_in(_jax.random.key(seed), 7919)
    shape, _ = _output_shape()
    out = dict(inp)
    out["loss_target"] = _jax.random.normal(_jax.random.fold_in(key, 0), shape, _jnp.float32)
    for i, name in enumerate(TWIN_WEIGHTS):
        w = inp[name].astype(_jnp.float32)
        if MOMENT_SCALE is None:
            s = _jnp.sqrt(_jnp.mean(_jnp.square(w)) + 1e-30)
        else:
            s = MOMENT_SCALE[name]
        km, kv = _jax.random.split(_jax.random.fold_in(key, i + 1))
        out[name] = w
        out["m_" + name] = s * _jax.random.normal(km, w.shape, _jnp.float32)
        out["v_" + name] = (s * s) * _jax.random.uniform(kv, w.shape, _jnp.float32, 0.5, 1.5)
    if N_MICROBATCH > 1:
        for name, axis in PER_EXAMPLE_BATCH_AXIS.items():
            out[name] = _to_microbatches(out[name], axis)
    return {'x': out['x'], 'c': out['c'], 'ctx': out['ctx'], 'c_ctx': out['c_ctx'], 'w_mod': out['w_mod'], 'b_mod': out['b_mod'], 'norm_w': out['norm_w'], 'w_ffn1_in': out['w_ffn1_in'], 'w_ffn1_out': out['w_ffn1_out'], 'w_ffn2_in': out['w_ffn2_in'], 'w_ffn2_out': out['w_ffn2_out'], 'w_in': out['w_in'], 'b_gate': out['b_gate'], 'q_norm_w': out['q_norm_w'], 'k_norm_w': out['k_norm_w'], 'gmlp_ln_w': out['gmlp_ln_w'], 'gmlp_ln_b': out['gmlp_ln_b'], 'w_spatial': out['w_spatial'], 'b_spatial': out['b_spatial'], 'w_branch_attn': out['w_branch_attn'], 'w_branch_gmlp': out['w_branch_gmlp'], 'w_out': out['w_out'], 'final_norm_w': out['final_norm_w'], 'loss_target': out['loss_target'], 'm_c_ctx': out['m_c_ctx'], 'm_w_mod': out['m_w_mod'], 'm_b_mod': out['m_b_mod'], 'm_norm_w': out['m_norm_w'], 'm_w_ffn1_in': out['m_w_ffn1_in'], 'm_w_ffn1_out': out['m_w_ffn1_out'], 'm_w_ffn2_in': out['m_w_ffn2_in'], 'm_w_ffn2_out': out['m_w_ffn2_out'], 'm_w_in': out['m_w_in'], 'm_b_gate': out['m_b_gate'], 'm_q_norm_w': out['m_q_norm_w'], 'm_k_norm_w': out['m_k_norm_w'], 'm_gmlp_ln_w': out['m_gmlp_ln_w'], 'm_gmlp_ln_b': out['m_gmlp_ln_b'], 'm_w_spatial': out['m_w_spatial'], 'm_b_spatial': out['m_b_spatial'], 'm_w_branch_attn': out['m_w_branch_attn'], 'm_w_branch_gmlp': out['m_w_branch_gmlp'], 'm_w_out': out['m_w_out'], 'm_final_norm_w': out['m_final_norm_w'], 'v_c_ctx': out['v_c_ctx'], 'v_w_mod': out['v_w_mod'], 'v_b_mod': out['v_b_mod'], 'v_norm_w': out['v_norm_w'], 'v_w_ffn1_in': out['v_w_ffn1_in'], 'v_w_ffn1_out': out['v_w_ffn1_out'], 'v_w_ffn2_in': out['v_w_ffn2_in'], 'v_w_ffn2_out': out['v_w_ffn2_out'], 'v_w_in': out['v_w_in'], 'v_b_gate': out['v_b_gate'], 'v_q_norm_w': out['v_q_norm_w'], 'v_k_norm_w': out['v_k_norm_w'], 'v_gmlp_ln_w': out['v_gmlp_ln_w'], 'v_gmlp_ln_b': out['v_gmlp_ln_b'], 'v_w_spatial': out['v_w_spatial'], 'v_b_spatial': out['v_b_spatial'], 'v_w_branch_attn': out['v_w_branch_attn'], 'v_w_branch_gmlp': out['v_w_branch_gmlp'], 'v_w_out': out['v_w_out'], 'v_final_norm_w': out['v_final_norm_w']}


def _loss(weights, diff, rest, loss_target):
    with _jax.named_scope("forward"):
        args = {**rest, TWIN_DIFF_INPUT: diff, **{k: w.astype(_WEIGHT_DTYPES[k]) for k, w in weights.items()}}
        y = _forward(args)
    with _jax.named_scope("loss_head"):
        err = _jnp.square(y.astype(_jnp.float32) - loss_target)
        return 0.5 * _jnp.sum(_jnp.mean(err, axis=-1)) if err.ndim else 0.5 * err


def _adamw(w, g, m, v):
    m = ADAM_B1 * m + (1.0 - ADAM_B1) * g
    v = ADAM_B2 * v + (1.0 - ADAM_B2) * _jnp.square(g)
    m_hat = m / (1.0 - ADAM_B1 ** ADAM_STEP)
    v_hat = v / (1.0 - ADAM_B2 ** ADAM_STEP)
    delta = -ADAM_LR * (m_hat / (_jnp.sqrt(v_hat) + ADAM_EPS) + ADAM_WD * w)
    return delta, m, v


def reference(x, c, ctx, c_ctx, w_mod, b_mod, norm_w, w_ffn1_in, w_ffn1_out, w_ffn2_in, w_ffn2_out, w_in, b_gate, q_norm_w, k_norm_w, gmlp_ln_w, gmlp_ln_b, w_spatial, b_spatial, w_branch_attn, w_branch_gmlp, w_out, final_norm_w, loss_target, m_c_ctx, m_w_mod, m_b_mod, m_norm_w, m_w_ffn1_in, m_w_ffn1_out, m_w_ffn2_in, m_w_ffn2_out, m_w_in, m_b_gate, m_q_norm_w, m_k_norm_w, m_gmlp_ln_w, m_gmlp_ln_b, m_w_spatial, m_b_spatial, m_w_branch_attn, m_w_branch_gmlp, m_w_out, m_final_norm_w, v_c_ctx, v_w_mod, v_b_mod, v_norm_w, v_w_ffn1_in, v_w_ffn1_out, v_w_ffn2_in, v_w_ffn2_out, v_w_in, v_b_gate, v_q_norm_w, v_k_norm_w, v_gmlp_ln_w, v_gmlp_ln_b, v_w_spatial, v_b_spatial, v_w_branch_attn, v_w_branch_gmlp, v_w_out, v_final_norm_w):
    given = dict(x=x, c=c, ctx=ctx, c_ctx=c_ctx, w_mod=w_mod, b_mod=b_mod, norm_w=norm_w, w_ffn1_in=w_ffn1_in, w_ffn1_out=w_ffn1_out, w_ffn2_in=w_ffn2_in, w_ffn2_out=w_ffn2_out, w_in=w_in, b_gate=b_gate, q_norm_w=q_norm_w, k_norm_w=k_norm_w, gmlp_ln_w=gmlp_ln_w, gmlp_ln_b=gmlp_ln_b, w_spatial=w_spatial, b_spatial=b_spatial, w_branch_attn=w_branch_attn, w_branch_gmlp=w_branch_gmlp, w_out=w_out, final_norm_w=final_norm_w, loss_target=loss_target, m_c_ctx=m_c_ctx, m_w_mod=m_w_mod, m_b_mod=m_b_mod, m_norm_w=m_norm_w, m_w_ffn1_in=m_w_ffn1_in, m_w_ffn1_out=m_w_ffn1_out, m_w_ffn2_in=m_w_ffn2_in, m_w_ffn2_out=m_w_ffn2_out, m_w_in=m_w_in, m_b_gate=m_b_gate, m_q_norm_w=m_q_norm_w, m_k_norm_w=m_k_norm_w, m_gmlp_ln_w=m_gmlp_ln_w, m_gmlp_ln_b=m_gmlp_ln_b, m_w_spatial=m_w_spatial, m_b_spatial=m_b_spatial, m_w_branch_attn=m_w_branch_attn, m_w_branch_gmlp=m_w_branch_gmlp, m_w_out=m_w_out, m_final_norm_w=m_final_norm_w, v_c_ctx=v_c_ctx, v_w_mod=v_w_mod, v_b_mod=v_b_mod, v_norm_w=v_norm_w, v_w_ffn1_in=v_w_ffn1_in, v_w_ffn1_out=v_w_ffn1_out, v_w_ffn2_in=v_w_ffn2_in, v_w_ffn2_out=v_w_ffn2_out, v_w_in=v_w_in, v_b_gate=v_b_gate, v_q_norm_w=v_q_norm_w, v_k_norm_w=v_k_norm_w, v_gmlp_ln_w=v_gmlp_ln_w, v_gmlp_ln_b=v_gmlp_ln_b, v_w_spatial=v_w_spatial, v_b_spatial=v_b_spatial, v_w_branch_attn=v_w_branch_attn, v_w_branch_gmlp=v_w_branch_gmlp, v_w_out=v_w_out, v_final_norm_w=v_final_norm_w)
    weights = {n: given[n] for n in TWIN_WEIGHTS}
    shared = {n: given[n] for n in SHARED_INPUTS}
    per_example = {n: given[n] for n in ['x', 'c', 'ctx']}
    grad_fn = _jax.value_and_grad(_loss, argnums=(0, 1))

    def one_microbatch(ex, loss_target):
        ex = dict(ex)
        diff = ex.pop(TWIN_DIFF_INPUT)
        return grad_fn(weights, diff, {**shared, **ex}, loss_target)

    if N_MICROBATCH == 1:
        loss, (grad_w, grad_x) = one_microbatch(per_example, given["loss_target"])
    else:
        def body(carry, xs):
            loss_sum, grad_sum = carry
            l_k, (gw_k, gx_k) = one_microbatch(xs[0], xs[1])
            with _jax.named_scope("update"):
                return (loss_sum + l_k, _jax.tree.map(_jnp.add, grad_sum, gw_k)), gx_k

        init = (_jnp.zeros((), _jnp.float32), _jax.tree.map(_jnp.zeros_like, weights))
        (loss, grad_w), grad_x = _jax.lax.scan(body, init, (per_example, given["loss_target"]))
    with _jax.named_scope("update"):
        delta_w, new_m, new_v = {}, {}, {}
        for n in TWIN_WEIGHTS:
            delta_w[n], new_m[n], new_v[n] = _adamw(weights[n], grad_w[n], given["m_" + n], given["v_" + n])
    return (loss, grad_x, *[grad_w[n] for n in TWIN_WEIGHTS], *[delta_w[n] for n in TWIN_WEIGHTS],
            *[new_m[n] for n in TWIN_WEIGHTS], *[new_v[n] for n in TWIN_WEIGHTS])
```

```python
import functools
import math

import jax
import jax.numpy as jnp
from jax import lax
from jax.experimental import pallas as pl
from jax.experimental.pallas import tpu as pltpu

F32 = jnp.float32
BF16 = jnp.bfloat16
MESH = pl.DeviceIdType.MESH
HBM_SPEC = pl.BlockSpec(memory_space=pltpu.HBM)
VMEM_SPEC = pl.BlockSpec(memory_space=pltpu.VMEM)

HEAD_DIM = 128
N_Q_HEADS = 16
N_KV_HEADS = 4
GMLP_GROUPS = 16
CHUNK = 128
GRID_W = 64
ROPE_THETA = 10000.0
EPS = 1e-6
MACARON_WEIGHT = 0.5
N_MOD = 9

ADAM_LR = 0.001
ADAM_B1 = 0.9
ADAM_B2 = 0.999
ADAM_EPS = 1e-08
ADAM_WD = 0.01
ADAM_STEP = 10

N_SHARDS = 4
N_DEV = 8
ROW_TILE = 256
LANES = 128
VMEM_LIMIT_BYTES = 48 * 1024 * 1024
ADAM_TILE_BYTES = 1 << 20

NN = (((1,), (0,)), ((), ()))
NT = (((1,), (1,)), ((), ()))
TN = (((0,), (0,)), ((), ()))


def _pick(n, cands):
    for t in cands:
        if t <= n and n % t == 0:
            return t
    raise ValueError(f"no tile for {n} among {cands}")


def _params(*sem):
    return pltpu.CompilerParams(dimension_semantics=sem or None, vmem_limit_bytes=VMEM_LIMIT_BYTES)


def _sigmoid(x):
    return 1.0 / (1.0 + jnp.exp(-x))


def _matmul(a, b, *, form, name, out_dtype=F32, b_shards=1, out_shards=1, tm=None, tn=None, tk=None, order="ji"):
    if form == "nn":
        M, K = a.shape
        N = b.shape[-1] * b_shards
    elif form == "nt":
        M, K = a.shape
        N = b.shape[-2]
    else:
        K, M = a.shape
        N = b.shape[-1]
    n_sh = N // b_shards if form == "nn" else N // out_shards
    k_sh = K // b_shards if form == "nt" else K
    tm = _pick(M, ((tm,) if tm else ()) + (1024, 768, 512, 256, 128, 64, 32, 16))
    tn = _pick(n_sh, ((tn,) if tn else ()) + (1408, 1024, 512, 256, 128))
    tk = tk or _pick(k_sh, (1024, 768, 512, 256, 128) if form == "tn" else (2816, 2048, 1408, 1024, 768, 512, 256, 128))
    nk = K // tk
    nb = n_sh // tn
    kb = k_sh // tk
    dims = {"nn": NN, "nt": NT, "tn": TN}[form]

    def ij(g0, g1):
        return (g0, g1) if order == "ij" else (g1, g0)

    def a_map(g0, g1, k):
        i, _ = ij(g0, g1)
        return (k, i) if form == "tn" else (i, k)

    def b_map(g0, g1, k):
        _, j = ij(g0, g1)
        if form == "nn":
            return (j // nb, k, j % nb) if b_shards > 1 else (k, j)
        if form == "nt":
            return (k // kb, j, k % kb) if b_shards > 1 else (j, k)
        return (k, j)

    def o_map(g0, g1, k):
        i, j = ij(g0, g1)
        return (j // nb, i, j % nb) if out_shards > 1 else (i, j)

    a_block = (tk, tm) if form == "tn" else (tm, tk)
    b_block = (tk, tn) if form in ("nn", "tn") else (tn, tk)
    if b_shards > 1:
        b_block = (None,) + b_block
    o_block = (None, tm, tn) if out_shards > 1 else (tm, tn)
    o_shape = (out_shards, M, n_sh) if out_shards > 1 else (M, N)

    def body(a_ref, b_ref, o_ref, *scratch):
        part = lax.dot_general(a_ref[...].astype(BF16), b_ref[...].astype(BF16), dims, preferred_element_type=F32)
        if nk == 1:
            o_ref[...] = part.astype(o_ref.dtype)
        else:
            acc = scratch[0]
            k = pl.program_id(2)

            @pl.when(k == 0)
            def _():
                acc[...] = part

            @pl.when(k > 0)
            def _():
                acc[...] += part

            @pl.when(k == nk - 1)
            def _():
                o_ref[...] = acc[...].astype(o_ref.dtype)

    n_i, n_j = M // tm, N // tn
    grid = (n_i, n_j, nk) if order == "ij" else (n_j, n_i, nk)
    return pl.pallas_call(
        body,
        name=name,
        out_shape=jax.ShapeDtypeStruct(o_shape, out_dtype),
        grid=grid,
        in_specs=[pl.BlockSpec(a_block, a_map), pl.BlockSpec(b_block, b_map)],
        out_specs=pl.BlockSpec(o_block, o_map),
        scratch_shapes=[pltpu.VMEM((tm, tn), F32)] if nk > 1 else [],
        compiler_params=_params("parallel", "parallel", "arbitrary"),
    )(a, b)


def _type_of(i, n_xt, n_tiles):
    return jnp.where(i >= n_xt, 1, 0) if n_tiles > n_xt else 0


def _mod_spec(D, k, n_xt, n_tiles):
    return pl.BlockSpec((None, 1, D), lambda i: (_type_of(i, n_xt, n_tiles) * N_MOD + k, 0, 0))


def _acc_spec(D, n_xt, n_tiles):
    return pl.BlockSpec((None, 1, D), lambda i: (_type_of(i, n_xt, n_tiles), 0, 0))


def _accumulate(ref, value, first):
    @pl.when(first)
    def _():
        ref[...] = value

    @pl.when(jnp.logical_not(first))
    def _():
        ref[...] += value


def _normmod(xp, mods, nw, *, k_shift, k_scale, n_xt, name, resid=None, rows=None):
    M, D = rows or xp.shape[0], xp.shape[1]
    n_tiles = M // ROW_TILE
    row = pl.BlockSpec((ROW_TILE, D), lambda i: (i, 0))
    vec = pl.BlockSpec((1, D), lambda i: (0, 0))
    mod = functools.partial(_mod_spec, D, n_xt=n_xt, n_tiles=n_tiles)

    if resid is not None:
        y_in, k_gate, weight = resid

        def body(xp_ref, y_ref, g_ref, nw_ref, sh_ref, sc_ref, x_ref, h_ref):
            g = g_ref[...] if weight is None else weight * g_ref[...]
            x = xp_ref[...] + g * y_ref[...]
            x_ref[...] = x
            r = lax.rsqrt(jnp.mean(x * x, axis=-1, keepdims=True) + EPS)
            y = (x * r) * nw_ref[...]
            h_ref[...] = (y * (1.0 + sc_ref[...]) + sh_ref[...]).astype(BF16)

        return pl.pallas_call(
            body, name=name, grid=(n_tiles,),
            out_shape=(jax.ShapeDtypeStruct((M, D), F32), jax.ShapeDtypeStruct((M, D), BF16)),
            in_specs=[row, row, mod(k_gate), vec, mod(k_shift), mod(k_scale)],
            out_specs=(row, row), compiler_params=_params("parallel"),
        )(xp, y_in, mods, nw, mods, mods)

    def body(xp_ref, nw_ref, sh_ref, sc_ref, h_ref):
        x = xp_ref[...]
        r = lax.rsqrt(jnp.mean(x * x, axis=-1, keepdims=True) + EPS)
        y = (x * r) * nw_ref[...]
        h_ref[...] = (y * (1.0 + sc_ref[...]) + sh_ref[...]).astype(BF16)

    return pl.pallas_call(
        body, name=name, grid=(n_tiles,), out_shape=jax.ShapeDtypeStruct((M, D), BF16),
        in_specs=[row, vec, mod(k_shift), mod(k_scale)], out_specs=row, compiler_params=_params("parallel"),
    )(xp, nw, mods, mods)


def _resid_bwd(dxo, y_in, mods, *, k_gate, weight, n_xt, name):
    M, D = dxo.shape
    n_tiles = M // ROW_TILE
    n_types = 2 if n_tiles > n_xt else 1
    row = pl.BlockSpec((ROW_TILE, D), lambda i: (i, 0))

    def body(dxo_ref, y_ref, g_ref, dy_ref, dg_ref):
        i = pl.program_id(0)
        dxo_t = dxo_ref[...]
        g = g_ref[...] if weight is None else weight * g_ref[...]
        yw = y_ref[...] if weight is None else weight * y_ref[...]
        dy_ref[...] = (dxo_t * g).astype(BF16)
        _accumulate(dg_ref, jnp.sum(dxo_t * yw, axis=0, keepdims=True), (i == 0) | (i == n_xt))

    return pl.pallas_call(
        body, name=name, grid=(n_tiles,),
        out_shape=(jax.ShapeDtypeStruct((M, D), BF16), jax.ShapeDtypeStruct((n_types, 1, D), F32)),
        in_specs=[row, row, _mod_spec(D, k_gate, n_xt, n_tiles)],
        out_specs=(row, _acc_spec(D, n_xt, n_tiles)), compiler_params=_params("arbitrary"),
    )(dxo, y_in, mods)


def _normmod_bwd(dh, x_in, dxo, mods, nw, *, k_scale, n_xt, name):
    M, D = dh.shape
    n_tiles = M // ROW_TILE
    n_types = 2 if n_tiles > n_xt else 1
    dxo_tiles = dxo.shape[0] // ROW_TILE
    row = pl.BlockSpec((ROW_TILE, D), lambda i: (i, 0))
    dxo_row = pl.BlockSpec((ROW_TILE, D), lambda i: (jnp.minimum(i, dxo_tiles - 1), 0))
    vec = pl.BlockSpec((1, D), lambda i: (0, 0))

    def body(dh_ref, x_ref, dxo_ref, nw_ref, sc_ref, dx_ref, dsh_ref, dsc_ref, dnw_ref):
        i = pl.program_id(0)
        x = x_ref[...]
        dh_t = dh_ref[...]
        w = nw_ref[...]
        r = lax.rsqrt(jnp.mean(x * x, axis=-1, keepdims=True) + EPS)
        n = x * r
        dy = dh_t * (1.0 + sc_ref[...])
        dn = dy * w
        dx = r * (dn - n * jnp.mean(dn * n, axis=-1, keepdims=True))
        if dxo_tiles < n_tiles:
            dx_ref[...] = dx + jnp.where(i < dxo_tiles, dxo_ref[...], 0.0)
        else:
            dx_ref[...] = dx + dxo_ref[...]
        first = (i == 0) | (i == n_xt)
        _accumulate(dsh_ref, jnp.sum(dh_t, axis=0, keepdims=True), first)
        _accumulate(dsc_ref, jnp.sum(dh_t * (n * w), axis=0, keepdims=True), first)
        _accumulate(dnw_ref, jnp.sum(dy * n, axis=0, keepdims=True), i == 0)

    acc = _acc_spec(D, n_xt, n_tiles)
    return pl.pallas_call(
        body, name=name, grid=(n_tiles,),
        out_shape=(jax.ShapeDtypeStruct((M, D), F32), jax.ShapeDtypeStruct((n_types, 1, D), F32),
                   jax.ShapeDtypeStruct((n_types, 1, D), F32), jax.ShapeDtypeStruct((1, D), F32)),
        in_specs=[row, row, dxo_row, vec, _mod_spec(D, k_scale, n_xt, n_tiles)],
        out_specs=(row, acc, acc, vec), compiler_params=_params("arbitrary"),
    )(dh, x_in, dxo, nw, mods)


def _swiglu_fwd(z, name):
    M, F2 = z.shape
    Fh = F2 // 2
    tf = _pick(Fh, (1408, 1024, 512, 256, 128))
    nf = Fh // tf

    def body(a_ref, b_ref, g_ref):
        a = a_ref[...]
        g_ref[...] = ((a * _sigmoid(a)) * b_ref[...]).astype(BF16)

    return pl.pallas_call(
        body, name=name, grid=(M // ROW_TILE, nf), out_shape=jax.ShapeDtypeStruct((M, Fh), BF16),
        in_specs=[pl.BlockSpec((ROW_TILE, tf), lambda i, j: (i, j)),
                  pl.BlockSpec((ROW_TILE, tf), lambda i, j: (i, j + nf))],
        out_specs=pl.BlockSpec((ROW_TILE, tf), lambda i, j: (i, j)), compiler_params=_params("parallel", "parallel"),
    )(z, z)


def _swiglu_bwd(z, dg, name):
    M, F2 = z.shape
    Fh = F2 // 2
    tf = _pick(Fh, (1408, 1024, 512, 256, 128))
    nf = Fh // tf

    def body(a_ref, b_ref, dg_ref, dz_ref):
        j = pl.program_id(1)
        a = a_ref[...]
        sig = _sigmoid(a)
        dg_t = dg_ref[...]

        @pl.when(j < nf)
        def _():
            dz_ref[...] = (dg_t * b_ref[...] * (sig * (1.0 + a * (1.0 - sig)))).astype(BF16)

        @pl.when(j >= nf)
        def _():
            dz_ref[...] = (dg_t * (a * sig)).astype(BF16)

    return pl.pallas_call(
        body, name=name, grid=(M // ROW_TILE, 2 * nf), out_shape=jax.ShapeDtypeStruct((M, F2), BF16),
        in_specs=[pl.BlockSpec((ROW_TILE, tf), lambda i, j: (i, j % nf)),
                  pl.BlockSpec((ROW_TILE, tf), lambda i, j: (i, j % nf + nf)),
                  pl.BlockSpec((ROW_TILE, tf), lambda i, j: (i, j % nf))],
        out_specs=pl.BlockSpec((ROW_TILE, tf), lambda i, j: (i, j)), compiler_params=_params("parallel", "parallel"),
    )(z, z, dg)


def _final_loss(x3, target, fw, name):
    S, D = x3.shape
    row = pl.BlockSpec((ROW_TILE, D), lambda i: (i, 0))
    vec = pl.BlockSpec((1, D), lambda i: (0, 0))

    def body(x_ref, t_ref, fw_ref, dx_ref, loss_ref, dfw_ref):
        i = pl.program_id(0)
        x = x_ref[...]
        w = fw_ref[...]
        r = lax.rsqrt(jnp.mean(x * x, axis=-1, keepdims=True) + EPS)
        n = x * r
        err = n * w - t_ref[...]
        tile_loss = 0.5 * jnp.sum(jnp.mean(err * err, axis=-1, keepdims=True), axis=0, keepdims=True)
        dout = err / D
        dn = dout * w
        dx_ref[...] = r * (dn - n * jnp.mean(dn * n, axis=-1, keepdims=True))
        _accumulate(loss_ref, jnp.broadcast_to(tile_loss, (1, LANES)), i == 0)
        _accumulate(dfw_ref, jnp.sum(dout * n, axis=0, keepdims=True), i == 0)

    return pl.pallas_call(
        body, name=name, grid=(S // ROW_TILE,),
        out_shape=(jax.ShapeDtypeStruct((S, D), F32), jax.ShapeDtypeStruct((1, LANES), F32),
                   jax.ShapeDtypeStruct((1, D), F32)),
        in_specs=[row, row, vec],
        out_specs=(row, pl.BlockSpec((1, LANES), lambda i: (0, 0)), vec), compiler_params=_params("arbitrary"),
    )(x3, target, fw)


def _pair_swap(v):
    lane = lax.broadcasted_iota(jnp.int32, v.shape, v.ndim - 1)
    from_next = pltpu.roll(lane, 1, v.ndim - 1) == (lane ^ 1)
    return jnp.where(from_next, pltpu.roll(v, 1, v.ndim - 1), pltpu.roll(v, LANES - 1, v.ndim - 1))


def _qk_prep(zx, cos2, sin2, qw, kw, *, q_w, kv_w, scale, name):
    M = zx.shape[0]
    nqh, nkh = q_w // HEAD_DIM, kv_w // HEAD_DIM
    row = lambda w, c: pl.BlockSpec((ROW_TILE, w), lambda i: (i, c))
    vec = pl.BlockSpec((1, HEAD_DIM), lambda i: (0, 0))

    def rot(z, w, cos_t, sin_t):
        r = lax.rsqrt(jnp.mean(z * z, axis=-1, keepdims=True) + EPS)
        y = (z * r) * w
        return y * cos_t + _pair_swap(y) * sin_t

    def body(zq_ref, zk_ref, zv_ref, cos_ref, sin_ref, qw_ref, kw_ref, q_ref, k_ref, v_ref):
        cos_t, sin_t = cos_ref[...], sin_ref[...]
        for h in range(nqh):
            hs = slice(h * HEAD_DIM, (h + 1) * HEAD_DIM)
            q_ref[:, hs] = (rot(zq_ref[:, hs], qw_ref[...], cos_t, sin_t) * scale).astype(BF16)
        for h in range(nkh):
            hs = slice(h * HEAD_DIM, (h + 1) * HEAD_DIM)
            k_ref[:, hs] = rot(zk_ref[:, hs], kw_ref[...], cos_t, sin_t).astype(BF16)
        v_ref[...] = zv_ref[...].astype(BF16)

    return pl.pallas_call(
        body, name=name, grid=(M // ROW_TILE,),
        out_shape=(jax.ShapeDtypeStruct((M, q_w), BF16), jax.ShapeDtypeStruct((M, kv_w), BF16),
                   jax.ShapeDtypeStruct((M, kv_w), BF16)),
        in_specs=[row(q_w, 0), row(kv_w, q_w // kv_w), row(kv_w, q_w // kv_w + 1), row(HEAD_DIM, 0), row(HEAD_DIM, 0),
                  vec, vec],
        out_specs=(row(q_w, 0), row(kv_w, 0), row(kv_w, 0)), compiler_params=_params("parallel"),
    )(zx, zx, zx, cos2, sin2, qw, kw)


def _qk_prep_bwd(zx, dq, dk, dv, cos2, sin2, qw, kw, *, q_w, kv_w, scale, n_xt, name):
    M = zx.shape[0]
    nqh, nkh = q_w // HEAD_DIM, kv_w // HEAD_DIM
    dq_tiles = dq.shape[0] // ROW_TILE
    row = lambda w, c: pl.BlockSpec((ROW_TILE, w), lambda i: (i, c))
    vec = pl.BlockSpec((1, HEAD_DIM), lambda i: (0, 0))

    def unrot(z, d, w, cos_t, sin_t):
        r = lax.rsqrt(jnp.mean(z * z, axis=-1, keepdims=True) + EPS)
        n = z * r
        dy = d * cos_t - _pair_swap(d) * sin_t
        dn = dy * w
        dz = r * (dn - n * jnp.mean(dn * n, axis=-1, keepdims=True))
        return dz, jnp.sum(dy * n, axis=0, keepdims=True)

    def body(zq_ref, zk_ref, dq_ref, dk_ref, dv_ref, cos_ref, sin_ref, qw_ref, kw_ref, dz_ref, dqw_ref, dkw_ref):
        i = pl.program_id(0)
        cos_t, sin_t = cos_ref[...], sin_ref[...]
        is_x = i < n_xt
        dqw = jnp.zeros((1, HEAD_DIM), F32)
        dkw = jnp.zeros((1, HEAD_DIM), F32)
        for h in range(nqh):
            hs = slice(h * HEAD_DIM, (h + 1) * HEAD_DIM)
            d = jnp.where(is_x, dq_ref[:, hs], 0.0) * scale
            dz, dw = unrot(zq_ref[:, hs], d, qw_ref[...], cos_t, sin_t)
            dz_ref[:, hs] = dz.astype(BF16)
            dqw = dqw + dw
        for h in range(nkh):
            hs = slice(h * HEAD_DIM, (h + 1) * HEAD_DIM)
            dz, dw = unrot(zk_ref[:, hs], dk_ref[:, hs], kw_ref[...], cos_t, sin_t)
            dz_ref[:, q_w + h * HEAD_DIM:q_w + (h + 1) * HEAD_DIM] = dz.astype(BF16)
            dkw = dkw + dw
        dz_ref[:, q_w + kv_w:] = dv_ref[...].astype(BF16)
        _accumulate(dqw_ref, dqw, i == 0)
        _accumulate(dkw_ref, dkw, i == 0)

    return pl.pallas_call(
        body, name=name, grid=(M // ROW_TILE,),
        out_shape=(jax.ShapeDtypeStruct((M, q_w + 2 * kv_w), BF16), jax.ShapeDtypeStruct((1, HEAD_DIM), F32),
                   jax.ShapeDtypeStruct((1, HEAD_DIM), F32)),
        in_specs=[row(q_w, 0), row(kv_w, q_w // kv_w),
                  pl.BlockSpec((ROW_TILE, q_w), lambda i: (jnp.minimum(i, dq_tiles - 1), 0)),
                  row(kv_w, 0), row(kv_w, 0), row(HEAD_DIM, 0), row(HEAD_DIM, 0), vec, vec],
        out_specs=(row(q_w + 2 * kv_w, 0), vec, vec), compiler_params=_params("arbitrary"),
    )(zx, zx, dq, dk, dv, cos2, sin2, qw, kw)


def _lane_pick(tile, h):
    lane = lax.broadcasted_iota(jnp.int32, tile.shape, 1)
    return jnp.sum(jnp.where(lane == h, tile, 0.0), axis=-1, keepdims=True)


def _flash_fwd(q, k, v, *, S, name):
    NK, kv_w = k.shape
    G = kv_w // HEAD_DIM
    qpk = q.shape[1] // kv_w
    gw = qpk * HEAD_DIM
    tq = _pick(S, (512, 256, 128))
    tk = _pick(NK, (768, 512, 256, 128))
    nk = NK // tk

    def body(q_ref, k_ref, v_ref, o_ref, lse_ref, m_s, l_s, acc_s):
        ki = pl.program_id(2)

        @pl.when(ki == 0)
        def _():
            m_s[...] = jnp.full(m_s.shape, -1e30, F32)
            l_s[...] = jnp.zeros(l_s.shape, F32)
            acc_s[...] = jnp.zeros(acc_s.shape, F32)

        k_t, v_t = k_ref[...], v_ref[...]
        for h in range(qpk):
            s = lax.dot_general(q_ref[:, h * HEAD_DIM:(h + 1) * HEAD_DIM], k_t, NT, preferred_element_type=F32)
            m_prev = m_s[h]
            m_new = jnp.maximum(m_prev, jnp.max(s, axis=-1, keepdims=True))
            alpha = jnp.exp(m_prev - m_new)
            p = jnp.exp(s - m_new)
            l_s[h] = alpha * l_s[h] + jnp.sum(p, axis=-1, keepdims=True)
            acc_s[h] = alpha * acc_s[h] + lax.dot_general(p.astype(BF16), v_t, NN, preferred_element_type=F32)
            m_s[h] = m_new

        @pl.when(ki == nk - 1)
        def _():
            lane = lax.broadcasted_iota(jnp.int32, (tq, LANES), 1)
            lse = jnp.zeros((tq, LANES), F32)
            for h in range(qpk):
                l = l_s[h]
                o_ref[:, h * HEAD_DIM:(h + 1) * HEAD_DIM] = acc_s[h] / l
                lse = jnp.where(lane == h, m_s[h] + jnp.log(l), lse)
            lse_ref[...] = lse

    return pl.pallas_call(
        body, name=name, grid=(G, S // tq, nk),
        out_shape=(jax.ShapeDtypeStruct((S, G * gw), F32), jax.ShapeDtypeStruct((G, S, LANES), F32)),
        in_specs=[pl.BlockSpec((tq, gw), lambda g, i, j: (i, g)),
                  pl.BlockSpec((tk, HEAD_DIM), lambda g, i, j: (j, g)),
                  pl.BlockSpec((tk, HEAD_DIM), lambda g, i, j: (j, g))],
        out_specs=(pl.BlockSpec((tq, gw), lambda g, i, j: (i, g)),
                   pl.BlockSpec((None, tq, LANES), lambda g, i, j: (g, i, 0))),
        scratch_shapes=[pltpu.VMEM((qpk, tq, 1), F32), pltpu.VMEM((qpk, tq, 1), F32),
                        pltpu.VMEM((qpk, tq, HEAD_DIM), F32)],
        compiler_params=_params("parallel", "parallel", "arbitrary"),
    )(q, k, v)


def _attn_delta(do, o, *, G, name):
    S, q_w = o.shape
    gw = q_w // G
    qpk = gw // HEAD_DIM
    tq = _pick(S, (512, 256, 128))

    def body(do_ref, o_ref, d_ref):
        lane = lax.broadcasted_iota(jnp.int32, (tq, LANES), 1)
        out = jnp.zeros((tq, LANES), F32)
        for h in range(qpk):
            hs = slice(h * HEAD_DIM, (h + 1) * HEAD_DIM)
            out = jnp.where(lane == h, jnp.sum(do_ref[:, hs] * o_ref[:, hs], axis=-1, keepdims=True), out)
        d_ref[...] = out

    return pl.pallas_call(
        body, name=name, grid=(G, S // tq), out_shape=jax.ShapeDtypeStruct((G, S, LANES), F32),
        in_specs=[pl.BlockSpec((tq, gw), lambda g, i: (i, g)), pl.BlockSpec((tq, gw), lambda g, i: (i, g))],
        out_specs=pl.BlockSpec((None, tq, LANES), lambda g, i: (g, i, 0)),
        compiler_params=_params("parallel", "parallel"),
    )(do, o)


def _flash_dq(q, k, v, do, lse, delta, *, S, name):
    NK, kv_w = k.shape
    G = kv_w // HEAD_DIM
    qpk = q.shape[1] // kv_w
    gw = qpk * HEAD_DIM
    tq = _pick(S, (512, 256, 128))
    tk = _pick(NK, (768, 512, 256, 128))

    def body(q_ref, k_ref, v_ref, do_ref, lse_ref, dl_ref, dq_ref, lse_s, dl_s):
        ki = pl.program_id(2)

        @pl.when(ki == 0)
        def _():
            dq_ref[...] = jnp.zeros(dq_ref.shape, F32)
            for h in range(qpk):
                lse_s[h] = _lane_pick(lse_ref[...], h)
                dl_s[h] = _lane_pick(dl_ref[...], h)

        k_t, v_t = k_ref[...], v_ref[...]
        for h in range(qpk):
            hs = slice(h * HEAD_DIM, (h + 1) * HEAD_DIM)
            s = lax.dot_general(q_ref[:, hs], k_t, NT, preferred_element_type=F32)
            p = jnp.exp(s - lse_s[h])
            dp = lax.dot_general(do_ref[:, hs].astype(BF16), v_t, NT, preferred_element_type=F32)
            ds = (p * (dp - dl_s[h])).astype(BF16)
            dq_ref[:, hs] += lax.dot_general(ds, k_t, NN, preferred_element_type=F32)

    qspec = pl.BlockSpec((tq, gw), lambda g, i, j: (i, g))
    kspec = pl.BlockSpec((tk, HEAD_DIM), lambda g, i, j: (j, g))
    lspec = pl.BlockSpec((None, tq, LANES), lambda g, i, j: (g, i, 0))
    return pl.pallas_call(
        body, name=name, grid=(G, S // tq, NK // tk), out_shape=jax.ShapeDtypeStruct((S, G * gw), F32),
        in_specs=[qspec, kspec, kspec, qspec, lspec, lspec], out_specs=qspec,
        scratch_shapes=[pltpu.VMEM((qpk, tq, 1), F32), pltpu.VMEM((qpk, tq, 1), F32)],
        compiler_params=_params("parallel", "parallel", "arbitrary"),
    )(q, k, v, do, lse, delta)


def _flash_dkv(q, k, v, do, lse, delta, *, S, name):
    NK, kv_w = k.shape
    G = kv_w // HEAD_DIM
    qpk = q.shape[1] // kv_w
    gw = qpk * HEAD_DIM
    tq = _pick(S, (512, 256, 128))
    tk = _pick(NK, (768, 512, 256, 128))

    def body(q_ref, k_ref, v_ref, do_ref, lse_ref, dl_ref, dk_ref, dv_ref):
        qi = pl.program_id(2)

        @pl.when(qi == 0)
        def _():
            dk_ref[...] = jnp.zeros(dk_ref.shape, F32)
            dv_ref[...] = jnp.zeros(dv_ref.shape, F32)

        k_t, v_t = k_ref[...], v_ref[...]
        lse_t, dl_t = lse_ref[...], dl_ref[...]
        for h in range(qpk):
            hs = slice(h * HEAD_DIM, (h + 1) * HEAD_DIM)
            q_h = q_ref[:, hs]
            do_h = do_ref[:, hs].astype(BF16)
            s = lax.dot_general(q_h, k_t, NT, preferred_element_type=F32)
            p = jnp.exp(s - _lane_pick(lse_t, h))
            dv_ref[...] += lax.dot_general(p.astype(BF16), do_h, TN, preferred_element_type=F32)
            dp = lax.dot_general(do_h, v_t, NT, preferred_element_type=F32)
            ds = (p * (dp - _lane_pick(dl_t, h))).astype(BF16)
            dk_ref[...] += lax.dot_general(ds, q_h, TN, preferred_element_type=F32)

    qspec = pl.BlockSpec((tq, gw), lambda g, j, i: (i, g))
    kspec = pl.BlockSpec((tk, HEAD_DIM), lambda g, j, i: (j, g))
    lspec = pl.BlockSpec((None, tq, LANES), lambda g, j, i: (g, i, 0))
    return pl.pallas_call(
        body, name=name, grid=(G, NK // tk, S // tq),
        out_shape=(jax.ShapeDtypeStruct((NK, kv_w), F32), jax.ShapeDtypeStruct((NK, kv_w), F32)),
        in_specs=[qspec, kspec, kspec, qspec, lspec, lspec], out_specs=(kspec, kspec),
        compiler_params=_params("parallel", "parallel", "arbitrary"),
    )(q, k, v, do, lse, delta)


def _wide_specs(col0, width, bw):
    return [pl.BlockSpec((ROW_TILE, bw), functools.partial(lambda i, c: (i, c), c=col0 // bw + p))
            for p in range(width // bw)]


def _cat(refs):
    return refs[0][...] if len(refs) == 1 else jnp.concatenate([r[...] for r in refs], axis=1)


def _gelu(x):
    return 0.5 * x * (1.0 + lax.erf(x * (1.0 / math.sqrt(2.0))))


def _gelu_grad(x):
    return 0.5 * (1.0 + lax.erf(x * (1.0 / math.sqrt(2.0)))) + x * jnp.exp(-0.5 * x * x) * (1.0 / math.sqrt(2.0 * math.pi))


def _layernorm_stats(v):
    mu = jnp.mean(v, axis=-1, keepdims=True)
    xc = v - mu
    rstd = lax.rsqrt(jnp.mean(xc * xc, axis=-1, keepdims=True) + EPS)
    return xc * rstd, rstd


def _gmlp_fwd(zx, ln_w, ln_b, w_s, b_sb, *, S, col0, name):
    G, W = w_s.shape[0], ln_w.shape[1]
    gd = W // G
    bw = math.gcd(col0, W)
    n_parts = W // bw
    vec = pl.BlockSpec((1, W), lambda i: (0, 0))
    full3 = pl.BlockSpec((G, CHUNK, CHUNK), lambda i: (0, 0, 0))
    full3b = pl.BlockSpec((G, CHUNK, gd), lambda i: (0, 0, 0))

    def body(*refs):
        u_refs, v_refs = refs[:n_parts], refs[n_parts:2 * n_parts]
        lnw_ref, lnb_ref, ws_ref, bs_ref, gm_ref = refs[2 * n_parts:]
        u = _gelu(_cat(u_refs))
        vhat, _ = _layernorm_stats(_gelu(_cat(v_refs)))
        vn = (vhat * lnw_ref[...] + lnb_ref[...]).astype(BF16)
        for c in range(ROW_TILE // CHUNK):
            rs = slice(c * CHUNK, (c + 1) * CHUNK)
            for g in range(G):
                cs = slice(g * gd, (g + 1) * gd)
                mixed = lax.dot_general(ws_ref[g].astype(BF16), vn[rs, cs], NN, preferred_element_type=F32) + bs_ref[g]
                gm_ref[rs, cs] = (u[rs, cs] * mixed).astype(BF16)

    return pl.pallas_call(
        body, name=name, grid=(S // ROW_TILE,), out_shape=jax.ShapeDtypeStruct((S, W), BF16),
        in_specs=_wide_specs(col0, W, bw) + _wide_specs(col0 + W, W, bw) + [vec, vec, full3, full3b],
        out_specs=pl.BlockSpec((ROW_TILE, W), lambda i: (i, 0)), compiler_params=_params("parallel"),
    )(*([zx] * (2 * n_parts)), ln_w, ln_b, w_s, b_sb)


def _gmlp_bwd(zx, dgm, ln_w, ln_b, w_s, b_sb, *, S, col0, name):
    G, W = w_s.shape[0], ln_w.shape[1]
    gd = W // G
    bw = math.gcd(col0, W)
    n_parts = W // bw
    vec = pl.BlockSpec((1, W), lambda i: (0, 0))
    full3 = pl.BlockSpec((G, CHUNK, CHUNK), lambda i: (0, 0, 0))
    full3b = pl.BlockSpec((G, CHUNK, gd), lambda i: (0, 0, 0))
    row = pl.BlockSpec((ROW_TILE, W), lambda i: (i, 0))

    def body(*refs):
        u_refs, v_refs = refs[:n_parts], refs[n_parts:2 * n_parts]
        dgm_ref, lnw_ref, lnb_ref, ws_ref, bs_ref, dz_ref, dws_ref, dbs_ref, dlnw_ref, dlnb_ref, du_s, dvn_s = refs[2 * n_parts:]
        i = pl.program_id(0)
        upre, vpre = _cat(u_refs), _cat(v_refs)
        u = _gelu(upre)
        vhat, rstd = _layernorm_stats(_gelu(vpre))
        lnw = lnw_ref[...]
        vn = (vhat * lnw + lnb_ref[...]).astype(BF16)
        dgm_t = dgm_ref[...]

        @pl.when(i == 0)
        def _():
            dws_ref[...] = jnp.zeros(dws_ref.shape, F32)
            dbs_ref[...] = jnp.zeros(dbs_ref.shape, F32)

        for c in range(ROW_TILE // CHUNK):
            rs = slice(c * CHUNK, (c + 1) * CHUNK)
            for g in range(G):
                cs = slice(g * gd, (g + 1) * gd)
                ws_g = ws_ref[g].astype(BF16)
                vn_cg = vn[rs, cs]
                mixed = lax.dot_general(ws_g, vn_cg, NN, preferred_element_type=F32) + bs_ref[g]
                dgm_cg = dgm_t[rs, cs]
                du_s[rs, cs] = dgm_cg * mixed
                dmixed = dgm_cg * u[rs, cs]
                dmixed_b = dmixed.astype(BF16)
                dws_ref[g] += lax.dot_general(dmixed_b, vn_cg, NT, preferred_element_type=F32)
                dbs_ref[g] += dmixed
                dvn_s[rs, cs] = lax.dot_general(ws_g, dmixed_b, TN, preferred_element_type=F32)

        dvn = dvn_s[...]
        _accumulate(dlnw_ref, jnp.sum(dvn * vhat, axis=0, keepdims=True), i == 0)
        _accumulate(dlnb_ref, jnp.sum(dvn, axis=0, keepdims=True), i == 0)
        dvhat = dvn * lnw
        dv = rstd * (dvhat - jnp.mean(dvhat, axis=-1, keepdims=True)
                     - vhat * jnp.mean(dvhat * vhat, axis=-1, keepdims=True))
        dz_ref[:, :W] = (du_s[...] * _gelu_grad(upre)).astype(BF16)
        dz_ref[:, W:] = (dv * _gelu_grad(vpre)).astype(BF16)

    return pl.pallas_call(
        body, name=name, grid=(S // ROW_TILE,),
        out_shape=(jax.ShapeDtypeStruct((S, 2 * W), BF16), jax.ShapeDtypeStruct((G, CHUNK, CHUNK), F32),
                   jax.ShapeDtypeStruct((G, CHUNK, gd), F32), jax.ShapeDtypeStruct((1, W), F32),
                   jax.ShapeDtypeStruct((1, W), F32)),
        in_specs=_wide_specs(col0, W, bw) + _wide_specs(col0 + W, W, bw) + [row, vec, vec, full3, full3b],
        out_specs=(pl.BlockSpec((ROW_TILE, 2 * W), lambda i: (i, 0)), full3, full3b, vec, vec),
        scratch_shapes=[pltpu.VMEM((ROW_TILE, W), F32), pltpu.VMEM((ROW_TILE, W), F32)],
        compiler_params=_params("arbitrary"),
    )(*([zx] * (2 * n_parts)), dgm, ln_w, ln_b, w_s, b_sb)


def _merge_fwd(zx, a_br, g_br, b_gate, *, S, col0, name):
    D = a_br.shape[1]
    cw = min(math.gcd(col0, D), 1024)
    nc = D // cw
    c0 = col0 // cw
    blk = lambda f: pl.BlockSpec((ROW_TILE, cw), f)
    bias = lambda t: pl.BlockSpec((None, 1, cw), lambda i, j: (t, 0, j))

    def body(l0_ref, l1_ref, a_ref, g_ref, b0_ref, b1_ref, t_ref):
        g0 = _sigmoid(l0_ref[...] + b0_ref[...])
        g1 = _sigmoid(l1_ref[...] + b1_ref[...])
        t_ref[...] = (g0 * a_ref[...] + g1 * g_ref[...]).astype(BF16)

    return pl.pallas_call(
        body, name=name, grid=(S // ROW_TILE, nc), out_shape=jax.ShapeDtypeStruct((S, D), BF16),
        in_specs=[blk(lambda i, j: (i, c0 + j)), blk(lambda i, j: (i, c0 + nc + j)), blk(lambda i, j: (i, j)),
                  blk(lambda i, j: (i, j)), bias(0), bias(1)],
        out_specs=blk(lambda i, j: (i, j)), compiler_params=_params("parallel", "parallel"),
    )(zx, zx, a_br, g_br, b_gate, b_gate)


def _merge_bwd(dt, zx, a_br, g_br, b_gate, *, S, col0, name):
    D = a_br.shape[1]
    cw = min(math.gcd(col0, D), 1024)
    nc = D // cw
    c0 = col0 // cw
    blk = lambda f: pl.BlockSpec((ROW_TILE, cw), f)
    bias = lambda t: pl.BlockSpec((None, 1, cw), lambda j, i: (t, 0, j))
    own = blk(lambda j, i: (i, j))
    acc = pl.BlockSpec((1, cw), lambda j, i: (0, j))

    def body(dt_ref, l0_ref, l1_ref, a_ref, g_ref, b0_ref, b1_ref, da_ref, dg_ref, dl0_ref, dl1_ref, db0_ref, db1_ref):
        i = pl.program_id(1)
        dt_t = dt_ref[...]
        g0 = _sigmoid(l0_ref[...] + b0_ref[...])
        g1 = _sigmoid(l1_ref[...] + b1_ref[...])
        da_ref[...] = (dt_t * g0).astype(BF16)
        dg_ref[...] = (dt_t * g1).astype(BF16)
        dl0 = dt_t * a_ref[...] * (g0 * (1.0 - g0))
        dl1 = dt_t * g_ref[...] * (g1 * (1.0 - g1))
        dl0_ref[...] = dl0.astype(BF16)
        dl1_ref[...] = dl1.astype(BF16)
        _accumulate(db0_ref, jnp.sum(dl0, axis=0, keepdims=True), i == 0)
        _accumulate(db1_ref, jnp.sum(dl1, axis=0, keepdims=True), i == 0)

    sd = lambda dt_: jax.ShapeDtypeStruct((S, D), dt_)
    return pl.pallas_call(
        body, name=name, grid=(nc, S // ROW_TILE),
        out_shape=(sd(BF16), sd(BF16), sd(BF16), sd(BF16), jax.ShapeDtypeStruct((1, D), F32),
                   jax.ShapeDtypeStruct((1, D), F32)),
        in_specs=[own, blk(lambda j, i: (i, c0 + j)), blk(lambda j, i: (i, c0 + nc + j)), own, own, bias(0), bias(1)],
        out_specs=(own, own, own, own, acc, acc), compiler_params=_params("parallel", "arbitrary"),
    )(dt, zx, zx, a_br, g_br, b_gate, b_gate)


def _mod_fwd(cvecs, w_mod, name):
    R, D = cvecs.shape
    nsh = w_mod.shape[1]
    tn = _pick(nsh, (512, 256, 128))

    def body(c_ref, w_ref, o_ref):
        cv = c_ref[...]
        a = (cv * _sigmoid(cv)).astype(BF16)
        o_ref[...] = lax.dot_general(a, w_ref[...].astype(BF16), NN, preferred_element_type=F32)

    return pl.pallas_call(
        body, name=name, grid=(nsh // tn,), out_shape=jax.ShapeDtypeStruct((R, nsh), F32),
        in_specs=[pl.BlockSpec((R, D), lambda j: (0, 0)), pl.BlockSpec((D, tn), lambda j: (0, j))],
        out_specs=pl.BlockSpec((R, tn), lambda j: (0, j)), compiler_params=_params("parallel"),
    )(cvecs, w_mod)


def _mod_wgrad(cvecs, dm, name):
    R, D = cvecs.shape
    nsh = dm.shape[1]
    tn = _pick(nsh, (512, 256, 128))

    def body(c_ref, dm_ref, o_ref):
        cv = c_ref[...]
        a = (cv * _sigmoid(cv)).astype(BF16)
        o_ref[...] = lax.dot_general(a, dm_ref[...].astype(BF16), TN, preferred_element_type=F32)

    return pl.pallas_call(
        body, name=name, grid=(nsh // tn,), out_shape=jax.ShapeDtypeStruct((D, nsh), F32),
        in_specs=[pl.BlockSpec((R, D), lambda j: (0, 0)), pl.BlockSpec((R, tn), lambda j: (0, j))],
        out_specs=pl.BlockSpec((D, tn), lambda j: (0, j)), compiler_params=_params("parallel"),
    )(cvecs, dm)


def _mod_dgrad(dm, w_mod, name):
    R, nsh = dm.shape
    D = w_mod.shape[0]
    tn = _pick(D, (256, 128))

    def body(dm_ref, w_ref, o_ref):
        o_ref[...] = lax.dot_general(dm_ref[...].astype(BF16), w_ref[...].astype(BF16), NT, preferred_element_type=F32)

    return pl.pallas_call(
        body, name=name, grid=(D // tn,), out_shape=jax.ShapeDtypeStruct((R, D), F32),
        in_specs=[pl.BlockSpec((R, nsh), lambda j: (0, 0)), pl.BlockSpec((tn, nsh), lambda j: (j, 0))],
        out_specs=pl.BlockSpec((R, tn), lambda j: (0, j)), compiler_params=_params("parallel"),
    )(dm, w_mod)


def _adam_rows(R, C):
    return _pick(R, [t for t in (512, 256, 128, 64, 32, 16) if t * C * 4 <= ADAM_TILE_BYTES] or [16])


def _adamw(w, m, v, grads, name):
    R, C = w.shape
    tr = _adam_rows(R, C)
    n_g = len(grads)
    blk = pl.BlockSpec((tr, C), lambda i: (i, 0))
    c1 = 1.0 - ADAM_B1 ** ADAM_STEP
    c2 = 1.0 - ADAM_B2 ** ADAM_STEP

    def body(*refs):
        w_ref, m_ref, v_ref = refs[:3]
        g_refs = refs[3:3 + n_g]
        g_ref, d_ref, m2_ref, v2_ref = refs[3 + n_g:]
        g = g_refs[0][...].astype(F32)
        for r in g_refs[1:]:
            g = g + r[...].astype(F32)
        m2 = ADAM_B1 * m_ref[...] + (1.0 - ADAM_B1) * g
        v2 = ADAM_B2 * v_ref[...] + (1.0 - ADAM_B2) * (g * g)
        g_ref[...] = g
        m2_ref[...] = m2
        v2_ref[...] = v2
        d_ref[...] = -ADAM_LR * ((m2 / c1) / (jnp.sqrt(v2 / c2) + ADAM_EPS) + ADAM_WD * w_ref[...])

    out = jax.ShapeDtypeStruct((R, C), F32)
    return pl.pallas_call(
        body, name=name, grid=(R // tr,), out_shape=(out, out, out, out),
        in_specs=[blk] * (3 + n_g), out_specs=(blk, blk, blk, blk), compiler_params=_params("parallel"),
    )(w, m, v, *grads)


def _sum_slabs(slabs, name):
    n, R, C = slabs.shape
    tr = _adam_rows(R, C)

    def body(s_ref, o_ref):
        acc = s_ref[0].astype(F32)
        for k in range(1, n):
            acc = acc + s_ref[k].astype(F32)
        o_ref[...] = acc.astype(BF16)

    return pl.pallas_call(
        body, name=name, grid=(R // tr,), out_shape=jax.ShapeDtypeStruct((R, C), BF16),
        in_specs=[pl.BlockSpec((n, tr, C), lambda i: (0, i, 0))],
        out_specs=pl.BlockSpec((tr, C), lambda i: (i, 0)), compiler_params=_params("parallel"),
    )(slabs)


def _plane_peers():
    x, y = lax.axis_index("x"), lax.axis_index("y")
    return [(1 - x, y), (x, 1 - y), (1 - x, 1 - y)]


def _allgather_shards(shards, name):
    n = len(shards)

    def body(*refs):
        ins, outs = refs[:n], refs[n:2 * n]
        send_sems, recv_sems, local_sems = refs[2 * n:]
        x, y, c = lax.axis_index("x"), lax.axis_index("y"), lax.axis_index("c")
        me = 2 * x + y
        peers = _plane_peers()

        def remote(w, k, slab):
            px, py = peers[k]
            return pltpu.make_async_remote_copy(
                src_ref=ins[w], dst_ref=outs[w].at[slab], send_sem=send_sems.at[3 * w + k],
                recv_sem=recv_sems.at[3 * w + k], device_id=(px, py, c), device_id_type=MESH)

        local = [pltpu.make_async_copy(ins[w], outs[w].at[me], local_sems.at[w]) for w in range(n)]
        sends = [remote(w, k, me) for w in range(n) for k in range(3)]
        for cp in local + sends:
            cp.start()
        for w in range(n):
            for k, (px, py) in enumerate(peers):
                remote(w, k, 2 * px + py).wait_recv()
        for cp in sends:
            cp.wait_send()
        for cp in local:
            cp.wait()

    return pl.pallas_call(
        body, name=name,
        out_shape=tuple(jax.ShapeDtypeStruct((N_SHARDS,) + s.shape, s.dtype) for s in shards),
        in_specs=[HBM_SPEC] * n, out_specs=tuple([HBM_SPEC] * n),
        scratch_shapes=[pltpu.SemaphoreType.DMA((3 * n,)), pltpu.SemaphoreType.DMA((3 * n,)),
                        pltpu.SemaphoreType.DMA((n,))],
    )(*shards)


def _scatter_slabs(fulls, name):
    n = len(fulls)

    def body(*refs):
        ins, outs = refs[:n], refs[n:2 * n]
        send_sems, recv_sems, local_sems = refs[2 * n:]
        x, y, c = lax.axis_index("x"), lax.axis_index("y"), lax.axis_index("c")
        me = 2 * x + y
        peers = _plane_peers()

        def remote(w, k):
            px, py = peers[k]
            return pltpu.make_async_remote_copy(
                src_ref=ins[w].at[2 * px + py], dst_ref=outs[w].at[k], send_sem=send_sems.at[3 * w + k],
                recv_sem=recv_sems.at[3 * w + k], device_id=(px, py, c), device_id_type=MESH)

        local = [pltpu.make_async_copy(ins[w].at[me], outs[w].at[3], local_sems.at[w]) for w in range(n)]
        sends = [remote(w, k) for w in range(n) for k in range(3)]
        for cp in local + sends:
            cp.start()
        for cp in sends:
            cp.wait_recv()
        for cp in sends:
            cp.wait_send()
        for cp in local:
            cp.wait()

    return pl.pallas_call(
        body, name=name, out_shape=tuple(jax.ShapeDtypeStruct(f.shape, f.dtype) for f in fulls),
        in_specs=[HBM_SPEC] * n, out_specs=tuple([HBM_SPEC] * n),
        scratch_shapes=[pltpu.SemaphoreType.DMA((3 * n,)), pltpu.SemaphoreType.DMA((3 * n,)),
                        pltpu.SemaphoreType.DMA((n,))],
    )(*fulls)


def _sibling_exchange(blocks, name):
    n = len(blocks)

    def body(*refs):
        ins, outs = refs[:n], refs[n:2 * n]
        send_sems, recv_sems = refs[2 * n:]
        sibling = (lax.axis_index("x"), lax.axis_index("y"), 1 - lax.axis_index("c"))
        sends = [pltpu.make_async_remote_copy(src_ref=ins[w], dst_ref=outs[w], send_sem=send_sems.at[w],
                                              recv_sem=recv_sems.at[w], device_id=sibling, device_id_type=MESH)
                 for w in range(n)]
        for cp in sends:
            cp.start()
        for cp in sends:
            cp.wait_recv()
        for cp in sends:
            cp.wait_send()

    return pl.pallas_call(
        body, name=name, out_shape=tuple(jax.ShapeDtypeStruct(b.shape, b.dtype) for b in blocks),
        in_specs=[HBM_SPEC] * n, out_specs=tuple([HBM_SPEC] * n),
        scratch_shapes=[pltpu.SemaphoreType.DMA((n,)), pltpu.SemaphoreType.DMA((n,))],
    )(*blocks)


def _allgather_devices(block, name):
    m_per, n_cols = block.shape

    def body(x_ref, out_ref, send_sems, recv_sems, local_sem):
        x, y, c = lax.axis_index("x"), lax.axis_index("y"), lax.axis_index("c")
        me, sibling = (x, y, c), (x, y, 1 - c)
        chips = _plane_peers()

        def rows(px, py, pc):
            return out_ref.at[pl.ds((4 * px + 2 * py + pc) * m_per, m_per), :]

        def copy(k, blk, to, src=None):
            return pltpu.make_async_remote_copy(
                src_ref=rows(*blk) if src is None else src, dst_ref=rows(*blk), send_sem=send_sems.at[k],
                recv_sem=recv_sems.at[k], device_id=to, device_id_type=MESH)

        mine = pltpu.make_async_copy(x_ref, rows(*me), local_sem)
        mine.start()
        first = [copy(0, me, sibling, src=x_ref)]
        first += [copy(1 + j, me, (*chip, c), src=x_ref) for j, chip in enumerate(chips)]
        for cp in first:
            cp.start()
        passed = [copy(4 + j, (*chip, c), sibling) for j, chip in enumerate(chips)]
        for j, chip in enumerate(chips):
            copy(1 + j, (*chip, c), me).wait_recv()
            passed[j].start()
        copy(0, sibling, me).wait_recv()
        for j, chip in enumerate(chips):
            copy(4 + j, (*chip, 1 - c), me).wait_recv()
        for cp in first + passed:
            cp.wait_send()
        mine.wait()

    out = pl.pallas_call(
        body, name=name, out_shape=jax.ShapeDtypeStruct((N_DEV * m_per, n_cols), block.dtype),
        in_specs=[VMEM_SPEC], out_specs=VMEM_SPEC,
        scratch_shapes=[pltpu.SemaphoreType.DMA((7,)), pltpu.SemaphoreType.DMA((7,)), pltpu.SemaphoreType.DMA],
        compiler_params=pltpu.CompilerParams(vmem_limit_bytes=VMEM_LIMIT_BYTES),
    )(block)
    return out.reshape(N_DEV, m_per, n_cols)


def _pack(arrays, width):
    rows = []
    for a in arrays:
        flat = a.reshape(-1).astype(F32)
        pad = (-flat.shape[0]) % width
        rows.append(jnp.pad(flat, (0, pad)).reshape(-1, width))
    out = jnp.concatenate(rows, axis=0)
    return jnp.pad(out, ((0, (-out.shape[0]) % 16), (0, 0)))


def _unpack(packed, shapes, width):
    out, r = [], 0
    for shp in shapes:
        size = math.prod(shp)
        n_rows = -(-size // width)
        out.append(packed[r:r + n_rows].reshape(-1)[:size].reshape(shp))
        r += n_rows
    return out


def _rope_tables(S, C):
    rows = S // GRID_W
    axis_dim = HEAD_DIM // 2
    row = jnp.broadcast_to(jnp.arange(rows, dtype=F32)[:, None], (rows, GRID_W)).reshape(-1)
    col = jnp.broadcast_to(jnp.arange(GRID_W, dtype=F32)[None, :], (rows, GRID_W)).reshape(-1)
    inv_freq = ROPE_THETA ** (-jnp.arange(0, axis_dim, 2, dtype=F32) / axis_dim)
    ang = jnp.concatenate([row[:, None] * inv_freq, col[:, None] * inv_freq], axis=-1)
    cos, sin = jnp.cos(ang), jnp.sin(ang)
    cos2 = jnp.repeat(cos, 2, axis=-1)
    sin2 = jnp.stack([-sin, sin], axis=-1).reshape(S, HEAD_DIM)
    cos2 = jnp.concatenate([cos2, jnp.ones((C, HEAD_DIM), F32)], axis=0)
    sin2 = jnp.concatenate([sin2, jnp.zeros((C, HEAD_DIM), F32)], axis=0)
    return cos2, sin2


def kernel(x, c, ctx, c_ctx, w_mod, b_mod, norm_w, w_ffn1_in, w_ffn1_out, w_ffn2_in, w_ffn2_out, w_in, b_gate, q_norm_w, k_norm_w, gmlp_ln_w, gmlp_ln_b, w_spatial, b_spatial, w_branch_attn, w_branch_gmlp, w_out, final_norm_w, loss_target, m_c_ctx, m_w_mod, m_b_mod, m_norm_w, m_w_ffn1_in, m_w_ffn1_out, m_w_ffn2_in, m_w_ffn2_out, m_w_in, m_b_gate, m_q_norm_w, m_k_norm_w, m_gmlp_ln_w, m_gmlp_ln_b, m_w_spatial, m_b_spatial, m_w_branch_attn, m_w_branch_gmlp, m_w_out, m_final_norm_w, v_c_ctx, v_w_mod, v_b_mod, v_norm_w, v_w_ffn1_in, v_w_ffn1_out, v_w_ffn2_in, v_w_ffn2_out, v_w_in, v_b_gate, v_q_norm_w, v_k_norm_w, v_gmlp_ln_w, v_gmlp_ln_b, v_w_spatial, v_b_spatial, v_w_branch_attn, v_w_branch_gmlp, v_w_out, v_final_norm_w):
    _, S, D = x.shape
    C = ctx.shape[1]
    NTOK = S + C
    n_xt = S // ROW_TILE
    q_w, kv_w = N_Q_HEADS * HEAD_DIM, N_KV_HEADS * HEAD_DIM
    W = gmlp_ln_w.shape[1]
    v_end = q_w + 2 * kv_w
    gv_end = v_end + 2 * W
    scale = HEAD_DIM ** -0.5
    dev = 4 * lax.axis_index("x") + 2 * lax.axis_index("y") + lax.axis_index("c")
    shard = 2 * lax.axis_index("x") + lax.axis_index("y")

    c_all = _allgather_devices(jnp.pad(c, ((0, 7), (0, 0))), "gather_c")[:, 0, :]
    cvecs = jnp.concatenate([c_all, jnp.pad(c_ctx[None, :], ((0, 7), (0, 0)))], axis=0)
    w_mod_l = w_mod[0]
    n_modsh = w_mod_l.shape[1]
    mod_part = _mod_fwd(cvecs, w_mod_l, "mod_fwd")
    mod_all = _allgather_devices(mod_part, "gather_mod")[0::2]
    mod_full = jnp.transpose(mod_all, (1, 0, 2)).reshape(16, N_SHARDS * n_modsh) + b_mod
    mx = lax.dynamic_index_in_dim(mod_full, dev, 0, keepdims=False).reshape(N_MOD, D)
    mc = mod_full[8].reshape(N_MOD, D)
    mods = jnp.concatenate([mx, mc], axis=0).reshape(2 * N_MOD, 1, D)

    col_sharded = [w_ffn1_in[0], w_ffn2_in[0], w_in[0]]
    row_sharded = [w_ffn1_out[0], w_ffn2_out[0], w_branch_attn[0], w_branch_gmlp[0], w_out[0]]
    gathered = _allgather_shards([w.astype(BF16) for w in col_sharded + row_sharded], "gather_weights")
    wf1i, wf2i, win = gathered[:3]
    wf1o, wf2o, wba, wbg, wo = [g.reshape(-1, g.shape[-1]) for g in gathered[3:]]
    nw_all = _allgather_devices(_pack([norm_w[0], b_gate[0]], norm_w.shape[-1]), "gather_vecs")[0::2]
    n_vsh = norm_w.shape[-1]
    nw = jnp.transpose(nw_all[:, 0:3, :], (1, 0, 2)).reshape(3, D)
    bg = jnp.transpose(nw_all[:, 3:5, :], (1, 0, 2)).reshape(2, 1, D)
    nw0, nw1, nw2 = nw[0:1], nw[1:2], nw[2:3]

    cos2, sin2 = _rope_tables(S, C)
    b_sb = jnp.broadcast_to(b_spatial[0][:, :, None], (GMLP_GROUPS, CHUNK, W // GMLP_GROUPS))
    w_s = w_spatial[0]
    fw = final_norm_w[None, :]

    tok0 = jnp.concatenate([x[0], ctx[0]], axis=0)
    h1 = _normmod(tok0, mods, nw0, k_shift=0, k_scale=1, n_xt=n_xt, name="ffn1_norm")
    z1 = _matmul(h1, wf1i, form="nn", b_shards=N_SHARDS, name="ffn1_in")
    g1 = _swiglu_fwd(z1, "ffn1_act")
    y1 = _matmul(g1, wf1o, form="nn", name="ffn1_out", tn=1024)
    tok1, h2 = _normmod(tok0, mods, nw1, k_shift=3, k_scale=4, n_xt=n_xt, name="mix_norm",
                        resid=(y1, 2, MACARON_WEIGHT))
    zx = _matmul(h2, win, form="nn", b_shards=N_SHARDS, name="mix_in")
    qt, kt, vt = _qk_prep(zx, cos2, sin2, q_norm_w, k_norm_w, q_w=q_w, kv_w=kv_w, scale=scale, name="qk_prep")
    attn, lse = _flash_fwd(qt, kt, vt, S=S, name="attn_fwd")
    gm = _gmlp_fwd(zx, gmlp_ln_w, gmlp_ln_b, w_s, b_sb, S=S, col0=v_end, name="gmlp_fwd")
    a_br = _matmul(attn, wba, form="nn", name="branch_attn", tm=512)
    g_br = _matmul(gm, wbg, form="nn", name="branch_gmlp", tm=512)
    t_mix = _merge_fwd(zx, a_br, g_br, bg, S=S, col0=gv_end, name="merge_fwd")
    y_mix = _matmul(t_mix, wo, form="nn", name="mix_out", tm=512)
    x2, h3 = _normmod(tok1, mods, nw2, k_shift=6, k_scale=7, n_xt=n_xt, name="ffn2_norm", resid=(y_mix, 5, None),
                      rows=S)
    z2 = _matmul(h3, wf2i, form="nn", b_shards=N_SHARDS, name="ffn2_in")
    g2 = _swiglu_fwd(z2, "ffn2_act")
    y2 = _matmul(g2, wf2o, form="nn", name="ffn2_out", tn=1024)
    x3 = _resid_only(x2, y2, mods, k_gate=8, name="ffn2_resid")

    dx3, loss_row, dfw = _final_loss(x3, loss_target[0], fw, "loss")

    dy2, dgate8 = _resid_bwd(dx3, y2, mods, k_gate=8, weight=MACARON_WEIGHT, n_xt=n_xt, name="ffn2_resid_bwd")
    dg2 = _matmul(dy2, wf2o, form="nt", name="ffn2_out_dgrad")
    gw_f2o = _matmul(g2, dy2, form="tn", out_dtype=BF16, name="ffn2_out_wgrad", tn=1024)
    dz2 = _swiglu_bwd(z2, dg2, "ffn2_act_bwd")
    dh3 = _matmul(dz2, wf2i, form="nt", b_shards=N_SHARDS, name="ffn2_in_dgrad", tn=1024)
    gw_f2i = _matmul(h3, dz2, form="tn", out_dtype=BF16, out_shards=N_SHARDS, name="ffn2_in_wgrad")
    dx2, dsh6, dsc7, dnw2 = _normmod_bwd(dh3, x2, dx3, mods, nw2, k_scale=7, n_xt=n_xt, name="ffn2_norm_bwd")

    dy_mix, dgate5 = _resid_bwd(dx2, y_mix, mods, k_gate=5, weight=None, n_xt=n_xt, name="mix_resid_bwd")
    dt = _matmul(dy_mix, wo, form="nt", name="mix_out_dgrad", tm=512)
    gw_wo = _matmul(t_mix, dy_mix, form="tn", out_dtype=BF16, name="mix_out_wgrad", tn=1024)
    d_abr, d_gbr, dl0, dl1, dbg0, dbg1 = _merge_bwd(dt, zx, a_br, g_br, bg, S=S, col0=gv_end, name="merge_bwd")
    d_attn = _matmul(d_abr, wba, form="nt", name="branch_attn_dgrad", tm=512)
    gw_wba = _matmul(attn, d_abr, form="tn", out_dtype=BF16, name="branch_attn_wgrad", tn=1024)
    d_gm = _matmul(d_gbr, wbg, form="nt", name="branch_gmlp_dgrad", tm=512)
    gw_wbg = _matmul(gm, d_gbr, form="tn", out_dtype=BF16, name="branch_gmlp_wgrad", tn=1024)
    dz_gm, dws, dbs_wide, dlnw, dlnb = _gmlp_bwd(zx, d_gm, gmlp_ln_w, gmlp_ln_b, w_s, b_sb, S=S, col0=v_end,
                                                 name="gmlp_bwd")
    delta = _attn_delta(d_attn, attn, G=N_KV_HEADS, name="attn_delta")
    dq = _flash_dq(qt, kt, vt, d_attn, lse, delta, S=S, name="attn_dq")
    dk, dv = _flash_dkv(qt, kt, vt, d_attn, lse, delta, S=S, name="attn_dkv")
    dz_qkv, dqw, dkw = _qk_prep_bwd(zx, dq, dk, dv, cos2, sin2, q_norm_w, k_norm_w, q_w=q_w, kv_w=kv_w,
                                    scale=scale, n_xt=n_xt, name="qk_prep_bwd")
    ctx_pad = ((0, C), (0, 0))
    dzx = jnp.concatenate([dz_qkv, jnp.pad(dz_gm, ctx_pad), jnp.pad(dl0, ctx_pad), jnp.pad(dl1, ctx_pad)], axis=1)
    dh2 = _matmul(dzx, win, form="nt", b_shards=N_SHARDS, name="mix_in_dgrad", tn=1024)
    gw_win = _matmul(h2, dzx, form="tn", out_dtype=BF16, out_shards=N_SHARDS, name="mix_in_wgrad")
    dtok1, dsh3, dsc4, dnw1 = _normmod_bwd(dh2, tok1, dx2, mods, nw1, k_scale=4, n_xt=n_xt, name="mix_norm_bwd")

    dy1, dgate2 = _resid_bwd(dtok1, y1, mods, k_gate=2, weight=MACARON_WEIGHT, n_xt=n_xt, name="ffn1_resid_bwd")
    dg1 = _matmul(dy1, wf1o, form="nt", name="ffn1_out_dgrad")
    gw_f1o = _matmul(g1, dy1, form="tn", out_dtype=BF16, name="ffn1_out_wgrad", tn=1024)
    dz1 = _swiglu_bwd(z1, dg1, "ffn1_act_bwd")
    dh1 = _matmul(dz1, wf1i, form="nt", b_shards=N_SHARDS, name="ffn1_in_dgrad", tn=1024)
    gw_f1i = _matmul(h1, dz1, form="tn", out_dtype=BF16, out_shards=N_SHARDS, name="ffn1_in_wgrad")
    dtok0, dsh0, dsc1, dnw0 = _normmod_bwd(dh1, tok0, dtok1, mods, nw0, k_scale=1, n_xt=n_xt, name="ffn1_norm_bwd")
    grad_x = dtok0[:S][None]

    big_w = [w_ffn1_in, w_ffn2_in, w_in, w_ffn1_out, w_ffn2_out, w_branch_attn, w_branch_gmlp, w_out]
    big_m = [m_w_ffn1_in, m_w_ffn2_in, m_w_in, m_w_ffn1_out, m_w_ffn2_out, m_w_branch_attn, m_w_branch_gmlp, m_w_out]
    big_v = [v_w_ffn1_in, v_w_ffn2_in, v_w_in, v_w_ffn1_out, v_w_ffn2_out, v_w_branch_attn, v_w_branch_gmlp, v_w_out]
    big_names = ["w_ffn1_in", "w_ffn2_in", "w_in", "w_ffn1_out", "w_ffn2_out", "w_branch_attn", "w_branch_gmlp", "w_out"]
    full_grads = [gw_f1i, gw_f2i, gw_win] + [g.reshape(N_SHARDS, g.shape[0] // N_SHARDS, g.shape[1])
                                             for g in (gw_f1o, gw_f2o, gw_wba, gw_wbg, gw_wo)]
    received = _scatter_slabs(full_grads, "scatter_grads")
    plane_sums = [_sum_slabs(r, "plane_sum_" + nm) for r, nm in zip(received, big_names)]
    sibling_sums = _sibling_exchange(plane_sums, "sibling_grads")
    big_out = {}
    for nm, w_, m_, v_, pa, pb in zip(big_names, big_w, big_m, big_v, plane_sums, sibling_sums):
        res = _adamw(w_[0], m_[0], v_[0], [pa, pb], "adamw_" + nm)
        big_out[nm] = [r[None] for r in res]

    zeros9 = jnp.zeros((1, D), F32)
    dmx = jnp.concatenate([dsh0[0], dsc1[0], dgate2[0], dsh3[0], dsc4[0], dgate5[0], dsh6[0], dsc7[0], dgate8[0]], axis=0)
    dmc = jnp.concatenate([dsh0[1], dsc1[1], dgate2[1], dsh3[1], dsc4[1], zeros9, zeros9, zeros9, zeros9], axis=0)
    dbs = jnp.sum(dbs_wide, axis=-1)
    parts = [dmx, dmc, jnp.concatenate([dnw0, dnw1, dnw2], axis=0), jnp.concatenate([dbg0, dbg1], axis=0),
             dqw, dkw, dlnw, dlnb, dws, dbs, dfw, loss_row[:, :1]]
    part_shapes = [p.shape for p in parts]
    small_all = _allgather_devices(_pack(parts, D), "gather_small")
    dmx_all = small_all[:, 0:N_MOD, :].reshape(N_DEV, N_MOD * D)
    small_sum = small_all[0]
    for d in range(1, N_DEV):
        small_sum = small_sum + small_all[d]
    (_, dmc_sum, g_nw, g_bg, g_qw, g_kw, g_lnw, g_lnb, g_ws, g_bs, g_fw, loss_sum) = _unpack(small_sum, part_shapes, D)
    loss = loss_sum[0, 0]
    dmx_sum = small_sum[0:N_MOD].reshape(1, N_MOD * D)
    g_b_mod = dmx_sum + dmc_sum.reshape(1, N_MOD * D)
    dm_rows = jnp.concatenate([dmx_all, jnp.pad(dmc_sum.reshape(1, N_MOD * D), ((0, 7), (0, 0)))], axis=0)
    dm_sh = lax.dynamic_slice_in_dim(dm_rows, shard * n_modsh, n_modsh, axis=1)
    g_w_mod = _mod_wgrad(cvecs, dm_sh, "mod_wgrad")
    dsc_part = _mod_dgrad(dm_sh, w_mod_l, "mod_dgrad")
    dsc_all = _allgather_devices(dsc_part, "gather_dsilu")[0::2, 8, :]
    dscc = ((dsc_all[0] + dsc_all[1]) + dsc_all[2]) + dsc_all[3]
    sg = _sigmoid(c_ctx)
    g_c_ctx = dscc * (sg * (1.0 + c_ctx * (1.0 - sg)))
    g_norm_w = lax.dynamic_slice_in_dim(g_nw, shard * n_vsh, n_vsh, axis=1)[None]
    g_b_gate = lax.dynamic_slice_in_dim(g_bg, shard * n_vsh, n_vsh, axis=1)[None]

    w_mod_out = [r[None] for r in _adamw(w_mod_l, m_w_mod[0], v_w_mod[0], [g_w_mod], "adamw_w_mod")]

    small_names = ["c_ctx", "b_mod", "norm_w", "b_gate", "q_norm_w", "k_norm_w", "gmlp_ln_w", "gmlp_ln_b",
                   "w_spatial", "b_spatial", "final_norm_w"]
    small_w = [c_ctx, b_mod, norm_w, b_gate, q_norm_w, k_norm_w, gmlp_ln_w, gmlp_ln_b, w_spatial, b_spatial, final_norm_w]
    small_m = [m_c_ctx, m_b_mod, m_norm_w, m_b_gate, m_q_norm_w, m_k_norm_w, m_gmlp_ln_w, m_gmlp_ln_b, m_w_spatial,
               m_b_spatial, m_final_norm_w]
    small_v = [v_c_ctx, v_b_mod, v_norm_w, v_b_gate, v_q_norm_w, v_k_norm_w, v_gmlp_ln_w, v_gmlp_ln_b, v_w_spatial,
               v_b_spatial, v_final_norm_w]
    small_g = [g_c_ctx, g_b_mod, g_norm_w, g_b_gate, g_qw, g_kw, g_lnw, g_lnb, g_ws, g_bs, g_fw]
    small_shapes = [w_.shape for w_ in small_w]
    packed = [_pack(group, D) for group in (small_w, small_m, small_v, small_g)]
    small_res = [_unpack(r, small_shapes, D) for r in _adamw(*packed[:3], [packed[3]], "adamw_small")]
    small_out = {nm: [small_res[t][i] for t in range(4)] for i, nm in enumerate(small_names)}

    order = ["c_ctx", "w_mod", "b_mod", "norm_w", "w_ffn1_in", "w_ffn1_out", "w_ffn2_in", "w_ffn2_out", "w_in", "b_gate",
             "q_norm_w", "k_norm_w", "gmlp_ln_w", "gmlp_ln_b", "w_spatial", "b_spatial", "w_branch_attn", "w_branch_gmlp",
             "w_out", "final_norm_w"]
    results = {**big_out, **small_out, "w_mod": w_mod_out}
    outs = [loss, grad_x]
    for t in range(4):
        outs += [results[nm][t] for nm in order]
    return tuple(outs)


def _resid_only(xp, y_in, mods, *, k_gate, name):
    M, D = xp.shape
    row = pl.BlockSpec((ROW_TILE, D), lambda i: (i, 0))

    def body(xp_ref, y_ref, g_ref, x_ref):
        x_ref[...] = xp_ref[...] + (MACARON_WEIGHT * g_ref[...]) * y_ref[...]

    return pl.pallas_call(
        body, name=name, grid=(M // ROW_TILE,), out_shape=jax.ShapeDtypeStruct((M, D), F32),
        in_specs=[row, row, pl.BlockSpec((None, 1, D), lambda i: (k_gate, 0, 0))], out_specs=row,
        compiler_params=_params("parallel"),
    )(xp, y_in, mods)
```

```python
import functools
import math

import jax
import jax.numpy as jnp
from jax import lax
from jax.experimental import pallas as pl
from jax.experimental.pallas import tpu as pltpu

F32 = jnp.float32
BF16 = jnp.bfloat16
MESH = pl.DeviceIdType.MESH
HBM_SPEC = pl.BlockSpec(memory_space=pltpu.HBM)
VMEM_SPEC = pl.BlockSpec(memory_space=pltpu.VMEM)

HEAD_DIM = 128
N_Q_HEADS = 16
N_KV_HEADS = 4
GMLP_GROUPS = 16
CHUNK = 128
GRID_W = 64
ROPE_THETA = 10000.0
EPS = 1e-6
MACARON_WEIGHT = 0.5
N_MOD = 9

ADAM_LR = 0.001
ADAM_B1 = 0.9
ADAM_B2 = 0.999
ADAM_EPS = 1e-08
ADAM_WD = 0.01
ADAM_STEP = 10

N_SHARDS = 4
N_DEV = 8
ROW_TILE = 256
LANES = 128
VMEM_LIMIT_BYTES = 48 * 1024 * 1024
ADAM_TILE_BYTES = 1 << 20

NN = (((1,), (0,)), ((), ()))
NT = (((1,), (1,)), ((), ()))
TN = (((0,), (0,)), ((), ()))


def _pick(n, cands):
    for t in cands:
        if t <= n and n % t == 0:
            return t
    raise ValueError(f"no tile for {n} among {cands}")


def _params(*sem):
    return pltpu.CompilerParams(dimension_semantics=sem or None, vmem_limit_bytes=VMEM_LIMIT_BYTES)


def _sigmoid(x):
    return 1.0 / (1.0 + jnp.exp(-x))


def _matmul(a, b, *, form, name, out_dtype=F32, b_shards=1, out_shards=1, tm=None, tn=None, tk=None, order="ji",
            rider=None):
    if form == "nn":
        M, K = a.shape
        N = b.shape[-1] * b_shards
    elif form == "nt":
        M, K = a.shape
        N = b.shape[-2]
    else:
        K, M = a.shape
        N = b.shape[-1]
    n_sh = N // b_shards if form == "nn" else N // out_shards
    k_sh = K // b_shards if form == "nt" else K
    tm = _pick(M, ((tm,) if tm else ()) + (1024, 768, 512, 256, 128, 64, 32, 16))
    tn = _pick(n_sh, ((tn,) if tn else ()) + (1408, 1024, 512, 256, 128))
    tk = tk or _pick(k_sh, (1024, 768, 512, 256, 128) if form == "tn" else (2816, 2048, 1408, 1024, 768, 512, 256, 128))
    nk = K // tk
    nb = n_sh // tn
    kb = k_sh // tk
    dims = {"nn": NN, "nt": NT, "tn": TN}[form]

    def ij(g0, g1):
        return (g0, g1) if order == "ij" else (g1, g0)

    def a_map(g0, g1, k):
        i, _ = ij(g0, g1)
        return (k, i) if form == "tn" else (i, k)

    def b_map(g0, g1, k):
        _, j = ij(g0, g1)
        if form == "nn":
            return (j // nb, k, j % nb) if b_shards > 1 else (k, j)
        if form == "nt":
            return (k // kb, j, k % kb) if b_shards > 1 else (j, k)
        return (k, j)

    def o_map(g0, g1, k):
        i, j = ij(g0, g1)
        return (j // nb, i, j % nb) if out_shards > 1 else (i, j)

    a_block = (tk, tm) if form == "tn" else (tm, tk)
    b_block = (tk, tn) if form in ("nn", "tn") else (tn, tk)
    if b_shards > 1:
        b_block = (None,) + b_block
    o_block = (None, tm, tn) if out_shards > 1 else (tm, tn)
    o_shape = (out_shards, M, n_sh) if out_shards > 1 else (M, N)

    n_i, n_j = M // tm, N // tn
    grid = (n_i, n_j, nk) if order == "ij" else (n_j, n_i, nk)
    n_ride = len(rider.arrays) if rider else 0
    n_acc = 1 if nk > 1 else 0

    def body(*refs):
        a_ref, b_ref = refs[:2]
        ride_in = refs[2:2 + n_ride]
        o_ref = refs[2 + n_ride]
        ride_out = refs[3 + n_ride:3 + 2 * n_ride]
        scratch = refs[3 + 2 * n_ride:]
        ride_sems = scratch[n_acc:]
        pid = [pl.program_id(d) for d in range(3)]
        if rider:
            @pl.when((pid[0] == 0) & (pid[1] == 0) & (pid[2] == 0))
            def _():
                rider.start(ride_in, ride_out, ride_sems)

        part = lax.dot_general(a_ref[...].astype(BF16), b_ref[...].astype(BF16), dims, preferred_element_type=F32)
        if nk == 1:
            o_ref[...] = part.astype(o_ref.dtype)
        else:
            acc = scratch[0]
            k = pid[2]

            @pl.when(k == 0)
            def _():
                acc[...] = part

            @pl.when(k > 0)
            def _():
                acc[...] += part

            @pl.when(k == nk - 1)
            def _():
                o_ref[...] = acc[...].astype(o_ref.dtype)

        if rider:
            @pl.when((pid[0] == grid[0] - 1) & (pid[1] == grid[1] - 1) & (pid[2] == nk - 1))
            def _():
                rider.finish(ride_in, ride_out, ride_sems)

    main_shape = jax.ShapeDtypeStruct(o_shape, out_dtype)
    res = pl.pallas_call(
        body,
        name=name,
        out_shape=(main_shape, *rider.out_shapes) if rider else main_shape,
        grid=grid,
        in_specs=[pl.BlockSpec(a_block, a_map), pl.BlockSpec(b_block, b_map)] + [HBM_SPEC] * n_ride,
        out_specs=(pl.BlockSpec(o_block, o_map), *[HBM_SPEC] * n_ride) if rider else pl.BlockSpec(o_block, o_map),
        scratch_shapes=([pltpu.VMEM((tm, tn), F32)] if nk > 1 else []) + (rider.sems if rider else []),
        compiler_params=_params(*(("arbitrary",) * 3 if rider else ("parallel", "parallel", "arbitrary"))),
    )(a, b, *(rider.arrays if rider else ()))
    return (res[0], list(res[1:])) if rider else res


def _type_of(i, n_xt, n_tiles):
    return jnp.where(i >= n_xt, 1, 0) if n_tiles > n_xt else 0


def _mod_spec(D, k, n_xt, n_tiles):
    return pl.BlockSpec((None, 1, D), lambda i: (_type_of(i, n_xt, n_tiles) * N_MOD + k, 0, 0))


def _acc_spec(D, n_xt, n_tiles):
    return pl.BlockSpec((None, 1, D), lambda i: (_type_of(i, n_xt, n_tiles), 0, 0))


def _accumulate(ref, value, first):
    @pl.when(first)
    def _():
        ref[...] = value

    @pl.when(jnp.logical_not(first))
    def _():
        ref[...] += value


def _normmod(xp, mods, nw, *, k_shift, k_scale, n_xt, name, resid=None, rows=None):
    M, D = rows or xp.shape[0], xp.shape[1]
    n_tiles = M // ROW_TILE
    row = pl.BlockSpec((ROW_TILE, D), lambda i: (i, 0))
    vec = pl.BlockSpec((1, D), lambda i: (0, 0))
    mod = functools.partial(_mod_spec, D, n_xt=n_xt, n_tiles=n_tiles)

    if resid is not None:
        y_in, k_gate, weight = resid

        def body(xp_ref, y_ref, g_ref, nw_ref, sh_ref, sc_ref, x_ref, h_ref):
            g = g_ref[...] if weight is None else weight * g_ref[...]
            x = xp_ref[...] + g * y_ref[...]
            x_ref[...] = x
            r = lax.rsqrt(jnp.mean(x * x, axis=-1, keepdims=True) + EPS)
            y = (x * r) * nw_ref[...]
            h_ref[...] = (y * (1.0 + sc_ref[...]) + sh_ref[...]).astype(BF16)

        return pl.pallas_call(
            body, name=name, grid=(n_tiles,),
            out_shape=(jax.ShapeDtypeStruct((M, D), F32), jax.ShapeDtypeStruct((M, D), BF16)),
            in_specs=[row, row, mod(k_gate), vec, mod(k_shift), mod(k_scale)],
            out_specs=(row, row), compiler_params=_params("parallel"),
        )(xp, y_in, mods, nw, mods, mods)

    def body(xp_ref, nw_ref, sh_ref, sc_ref, h_ref):
        x = xp_ref[...]
        r = lax.rsqrt(jnp.mean(x * x, axis=-1, keepdims=True) + EPS)
        y = (x * r) * nw_ref[...]
        h_ref[...] = (y * (1.0 + sc_ref[...]) + sh_ref[...]).astype(BF16)

    return pl.pallas_call(
        body, name=name, grid=(n_tiles,), out_shape=jax.ShapeDtypeStruct((M, D), BF16),
        in_specs=[row, vec, mod(k_shift), mod(k_scale)], out_specs=row, compiler_params=_params("parallel"),
    )(xp, nw, mods, mods)


def _resid_bwd(dxo, y_in, mods, *, k_gate, weight, n_xt, name):
    M, D = dxo.shape
    n_tiles = M // ROW_TILE
    n_types = 2 if n_tiles > n_xt else 1
    row = pl.BlockSpec((ROW_TILE, D), lambda i: (i, 0))

    def body(dxo_ref, y_ref, g_ref, dy_ref, dg_ref):
        i = pl.program_id(0)
        dxo_t = dxo_ref[...]
        g = g_ref[...] if weight is None else weight * g_ref[...]
        yw = y_ref[...] if weight is None else weight * y_ref[...]
        dy_ref[...] = (dxo_t * g).astype(BF16)
        _accumulate(dg_ref, jnp.sum(dxo_t * yw, axis=0, keepdims=True), (i == 0) | (i == n_xt))

    return pl.pallas_call(
        body, name=name, grid=(n_tiles,),
        out_shape=(jax.ShapeDtypeStruct((M, D), BF16), jax.ShapeDtypeStruct((n_types, 1, D), F32)),
        in_specs=[row, row, _mod_spec(D, k_gate, n_xt, n_tiles)],
        out_specs=(row, _acc_spec(D, n_xt, n_tiles)), compiler_params=_params("arbitrary"),
    )(dxo, y_in, mods)


def _normmod_bwd(dh, x_in, dxo, mods, nw, *, k_scale, n_xt, name):
    M, D = dh.shape
    n_tiles = M // ROW_TILE
    n_types = 2 if n_tiles > n_xt else 1
    dxo_tiles = dxo.shape[0] // ROW_TILE
    row = pl.BlockSpec((ROW_TILE, D), lambda i: (i, 0))
    dxo_row = pl.BlockSpec((ROW_TILE, D), lambda i: (jnp.minimum(i, dxo_tiles - 1), 0))
    vec = pl.BlockSpec((1, D), lambda i: (0, 0))

    def body(dh_ref, x_ref, dxo_ref, nw_ref, sc_ref, dx_ref, dsh_ref, dsc_ref, dnw_ref):
        i = pl.program_id(0)
        x = x_ref[...]
        dh_t = dh_ref[...]
        w = nw_ref[...]
        r = lax.rsqrt(jnp.mean(x * x, axis=-1, keepdims=True) + EPS)
        n = x * r
        dy = dh_t * (1.0 + sc_ref[...])
        dn = dy * w
        dx = r * (dn - n * jnp.mean(dn * n, axis=-1, keepdims=True))
        if dxo_tiles < n_tiles:
            dx_ref[...] = dx + jnp.where(i < dxo_tiles, dxo_ref[...], 0.0)
        else:
            dx_ref[...] = dx + dxo_ref[...]
        first = (i == 0) | (i == n_xt)
        _accumulate(dsh_ref, jnp.sum(dh_t, axis=0, keepdims=True), first)
        _accumulate(dsc_ref, jnp.sum(dh_t * (n * w), axis=0, keepdims=True), first)
        _accumulate(dnw_ref, jnp.sum(dy * n, axis=0, keepdims=True), i == 0)

    acc = _acc_spec(D, n_xt, n_tiles)
    return pl.pallas_call(
        body, name=name, grid=(n_tiles,),
        out_shape=(jax.ShapeDtypeStruct((M, D), F32), jax.ShapeDtypeStruct((n_types, 1, D), F32),
                   jax.ShapeDtypeStruct((n_types, 1, D), F32), jax.ShapeDtypeStruct((1, D), F32)),
        in_specs=[row, row, dxo_row, vec, _mod_spec(D, k_scale, n_xt, n_tiles)],
        out_specs=(row, acc, acc, vec), compiler_params=_params("arbitrary"),
    )(dh, x_in, dxo, nw, mods)


def _swiglu_fwd(z, name):
    M, F2 = z.shape
    Fh = F2 // 2
    tf = _pick(Fh, (1408, 1024, 512, 256, 128))
    nf = Fh // tf

    def body(a_ref, b_ref, g_ref):
        a = a_ref[...]
        g_ref[...] = ((a * _sigmoid(a)) * b_ref[...]).astype(BF16)

    return pl.pallas_call(
        body, name=name, grid=(M // ROW_TILE, nf), out_shape=jax.ShapeDtypeStruct((M, Fh), BF16),
        in_specs=[pl.BlockSpec((ROW_TILE, tf), lambda i, j: (i, j)),
                  pl.BlockSpec((ROW_TILE, tf), lambda i, j: (i, j + nf))],
        out_specs=pl.BlockSpec((ROW_TILE, tf), lambda i, j: (i, j)), compiler_params=_params("parallel", "parallel"),
    )(z, z)


def _swiglu_bwd(z, dg, name):
    M, F2 = z.shape
    Fh = F2 // 2
    tf = _pick(Fh, (1408, 1024, 512, 256, 128))
    nf = Fh // tf

    def body(a_ref, b_ref, dg_ref, dz_ref):
        j = pl.program_id(1)
        a = a_ref[...]
        sig = _sigmoid(a)
        dg_t = dg_ref[...]

        @pl.when(j < nf)
        def _():
            dz_ref[...] = (dg_t * b_ref[...] * (sig * (1.0 + a * (1.0 - sig)))).astype(BF16)

        @pl.when(j >= nf)
        def _():
            dz_ref[...] = (dg_t * (a * sig)).astype(BF16)

    return pl.pallas_call(
        body, name=name, grid=(M // ROW_TILE, 2 * nf), out_shape=jax.ShapeDtypeStruct((M, F2), BF16),
        in_specs=[pl.BlockSpec((ROW_TILE, tf), lambda i, j: (i, j % nf)),
                  pl.BlockSpec((ROW_TILE, tf), lambda i, j: (i, j % nf + nf)),
                  pl.BlockSpec((ROW_TILE, tf), lambda i, j: (i, j % nf))],
        out_specs=pl.BlockSpec((ROW_TILE, tf), lambda i, j: (i, j)), compiler_params=_params("parallel", "parallel"),
    )(z, z, dg)


def _final_loss(x3, target, fw, name):
    S, D = x3.shape
    row = pl.BlockSpec((ROW_TILE, D), lambda i: (i, 0))
    vec = pl.BlockSpec((1, D), lambda i: (0, 0))

    def body(x_ref, t_ref, fw_ref, dx_ref, loss_ref, dfw_ref):
        i = pl.program_id(0)
        x = x_ref[...]
        w = fw_ref[...]
        r = lax.rsqrt(jnp.mean(x * x, axis=-1, keepdims=True) + EPS)
        n = x * r
        err = n * w - t_ref[...]
        tile_loss = 0.5 * jnp.sum(jnp.mean(err * err, axis=-1, keepdims=True), axis=0, keepdims=True)
        dout = err / D
        dn = dout * w
        dx_ref[...] = r * (dn - n * jnp.mean(dn * n, axis=-1, keepdims=True))
        _accumulate(loss_ref, jnp.broadcast_to(tile_loss, (1, LANES)), i == 0)
        _accumulate(dfw_ref, jnp.sum(dout * n, axis=0, keepdims=True), i == 0)

    return pl.pallas_call(
        body, name=name, grid=(S // ROW_TILE,),
        out_shape=(jax.ShapeDtypeStruct((S, D), F32), jax.ShapeDtypeStruct((1, LANES), F32),
                   jax.ShapeDtypeStruct((1, D), F32)),
        in_specs=[row, row, vec],
        out_specs=(row, pl.BlockSpec((1, LANES), lambda i: (0, 0)), vec), compiler_params=_params("arbitrary"),
    )(x3, target, fw)


def _pair_swap(v):
    lane = lax.broadcasted_iota(jnp.int32, v.shape, v.ndim - 1)
    from_next = pltpu.roll(lane, 1, v.ndim - 1) == (lane ^ 1)
    return jnp.where(from_next, pltpu.roll(v, 1, v.ndim - 1), pltpu.roll(v, LANES - 1, v.ndim - 1))


def _qk_prep(zx, cos2, sin2, qw, kw, *, q_w, kv_w, scale, name):
    M = zx.shape[0]
    nqh, nkh = q_w // HEAD_DIM, kv_w // HEAD_DIM
    row = lambda w, c: pl.BlockSpec((ROW_TILE, w), lambda i: (i, c))
    vec = pl.BlockSpec((1, HEAD_DIM), lambda i: (0, 0))

    def rot(z, w, cos_t, sin_t):
        r = lax.rsqrt(jnp.mean(z * z, axis=-1, keepdims=True) + EPS)
        y = (z * r) * w
        return y * cos_t + _pair_swap(y) * sin_t

    def body(zq_ref, zk_ref, zv_ref, cos_ref, sin_ref, qw_ref, kw_ref, q_ref, k_ref, v_ref):
        cos_t, sin_t = cos_ref[...], sin_ref[...]
        for h in range(nqh):
            hs = slice(h * HEAD_DIM, (h + 1) * HEAD_DIM)
            q_ref[:, hs] = (rot(zq_ref[:, hs], qw_ref[...], cos_t, sin_t) * scale).astype(BF16)
        for h in range(nkh):
            hs = slice(h * HEAD_DIM, (h + 1) * HEAD_DIM)
            k_ref[:, hs] = rot(zk_ref[:, hs], kw_ref[...], cos_t, sin_t).astype(BF16)
        v_ref[...] = zv_ref[...].astype(BF16)

    return pl.pallas_call(
        body, name=name, grid=(M // ROW_TILE,),
        out_shape=(jax.ShapeDtypeStruct((M, q_w), BF16), jax.ShapeDtypeStruct((M, kv_w), BF16),
                   jax.ShapeDtypeStruct((M, kv_w), BF16)),
        in_specs=[row(q_w, 0), row(kv_w, q_w // kv_w), row(kv_w, q_w // kv_w + 1), row(HEAD_DIM, 0), row(HEAD_DIM, 0),
                  vec, vec],
        out_specs=(row(q_w, 0), row(kv_w, 0), row(kv_w, 0)), compiler_params=_params("parallel"),
    )(zx, zx, zx, cos2, sin2, qw, kw)


def _qk_prep_bwd(zx, dq, dk, dv, cos2, sin2, qw, kw, *, q_w, kv_w, scale, n_xt, name):
    M = zx.shape[0]
    nqh, nkh = q_w // HEAD_DIM, kv_w // HEAD_DIM
    dq_tiles = dq.shape[0] // ROW_TILE
    row = lambda w, c: pl.BlockSpec((ROW_TILE, w), lambda i: (i, c))
    vec = pl.BlockSpec((1, HEAD_DIM), lambda i: (0, 0))

    def unrot(z, d, w, cos_t, sin_t):
        r = lax.rsqrt(jnp.mean(z * z, axis=-1, keepdims=True) + EPS)
        n = z * r
        dy = d * cos_t - _pair_swap(d) * sin_t
        dn = dy * w
        dz = r * (dn - n * jnp.mean(dn * n, axis=-1, keepdims=True))
        return dz, jnp.sum(dy * n, axis=0, keepdims=True)

    def body(zq_ref, zk_ref, dq_ref, dk_ref, dv_ref, cos_ref, sin_ref, qw_ref, kw_ref, dz_ref, dqw_ref, dkw_ref):
        i = pl.program_id(0)
        cos_t, sin_t = cos_ref[...], sin_ref[...]
        is_x = i < n_xt
        dqw = jnp.zeros((1, HEAD_DIM), F32)
        dkw = jnp.zeros((1, HEAD_DIM), F32)
        for h in range(nqh):
            hs = slice(h * HEAD_DIM, (h + 1) * HEAD_DIM)
            d = jnp.where(is_x, dq_ref[:, hs], 0.0) * scale
            dz, dw = unrot(zq_ref[:, hs], d, qw_ref[...], cos_t, sin_t)
            dz_ref[:, hs] = dz.astype(BF16)
            dqw = dqw + dw
        for h in range(nkh):
            hs = slice(h * HEAD_DIM, (h + 1) * HEAD_DIM)
            dz, dw = unrot(zk_ref[:, hs], dk_ref[:, hs], kw_ref[...], cos_t, sin_t)
            dz_ref[:, q_w + h * HEAD_DIM:q_w + (h + 1) * HEAD_DIM] = dz.astype(BF16)
            dkw = dkw + dw
        dz_ref[:, q_w + kv_w:] = dv_ref[...].astype(BF16)
        _accumulate(dqw_ref, dqw, i == 0)
        _accumulate(dkw_ref, dkw, i == 0)

    return pl.pallas_call(
        body, name=name, grid=(M // ROW_TILE,),
        out_shape=(jax.ShapeDtypeStruct((M, q_w + 2 * kv_w), BF16), jax.ShapeDtypeStruct((1, HEAD_DIM), F32),
                   jax.ShapeDtypeStruct((1, HEAD_DIM), F32)),
        in_specs=[row(q_w, 0), row(kv_w, q_w // kv_w),
                  pl.BlockSpec((ROW_TILE, q_w), lambda i: (jnp.minimum(i, dq_tiles - 1), 0)),
                  row(kv_w, 0), row(kv_w, 0), row(HEAD_DIM, 0), row(HEAD_DIM, 0), vec, vec],
        out_specs=(row(q_w + 2 * kv_w, 0), vec, vec), compiler_params=_params("arbitrary"),
    )(zx, zx, dq, dk, dv, cos2, sin2, qw, kw)


def _lane_pick(tile, h):
    lane = lax.broadcasted_iota(jnp.int32, tile.shape, 1)
    return jnp.sum(jnp.where(lane == h, tile, 0.0), axis=-1, keepdims=True)


def _flash_fwd(q, k, v, *, S, name):
    NK, kv_w = k.shape
    G = kv_w // HEAD_DIM
    qpk = q.shape[1] // kv_w
    gw = qpk * HEAD_DIM
    tq = _pick(S, (512, 256, 128))
    tk = _pick(NK, (768, 512, 256, 128))
    nk = NK // tk

    def body(q_ref, k_ref, v_ref, o_ref, lse_ref, m_s, l_s, acc_s):
        ki = pl.program_id(2)

        @pl.when(ki == 0)
        def _():
            m_s[...] = jnp.full(m_s.shape, -1e30, F32)
            l_s[...] = jnp.zeros(l_s.shape, F32)
            acc_s[...] = jnp.zeros(acc_s.shape, F32)

        k_t, v_t = k_ref[...], v_ref[...]
        for h in range(qpk):
            s = lax.dot_general(q_ref[:, h * HEAD_DIM:(h + 1) * HEAD_DIM], k_t, NT, preferred_element_type=F32)
            m_prev = m_s[h]
            m_new = jnp.maximum(m_prev, jnp.max(s, axis=-1, keepdims=True))
            alpha = jnp.exp(m_prev - m_new)
            p = jnp.exp(s - m_new)
            l_s[h] = alpha * l_s[h] + jnp.sum(p, axis=-1, keepdims=True)
            acc_s[h] = alpha * acc_s[h] + lax.dot_general(p.astype(BF16), v_t, NN, preferred_element_type=F32)
            m_s[h] = m_new

        @pl.when(ki == nk - 1)
        def _():
            lane = lax.broadcasted_iota(jnp.int32, (tq, LANES), 1)
            lse = jnp.zeros((tq, LANES), F32)
            for h in range(qpk):
                l = l_s[h]
                o_ref[:, h * HEAD_DIM:(h + 1) * HEAD_DIM] = acc_s[h] / l
                lse = jnp.where(lane == h, m_s[h] + jnp.log(l), lse)
            lse_ref[...] = lse

    return pl.pallas_call(
        body, name=name, grid=(G, S // tq, nk),
        out_shape=(jax.ShapeDtypeStruct((S, G * gw), F32), jax.ShapeDtypeStruct((G, S, LANES), F32)),
        in_specs=[pl.BlockSpec((tq, gw), lambda g, i, j: (i, g)),
                  pl.BlockSpec((tk, HEAD_DIM), lambda g, i, j: (j, g)),
                  pl.BlockSpec((tk, HEAD_DIM), lambda g, i, j: (j, g))],
        out_specs=(pl.BlockSpec((tq, gw), lambda g, i, j: (i, g)),
                   pl.BlockSpec((None, tq, LANES), lambda g, i, j: (g, i, 0))),
        scratch_shapes=[pltpu.VMEM((qpk, tq, 1), F32), pltpu.VMEM((qpk, tq, 1), F32),
                        pltpu.VMEM((qpk, tq, HEAD_DIM), F32)],
        compiler_params=_params("parallel", "parallel", "arbitrary"),
    )(q, k, v)


def _attn_delta(do, o, *, G, name):
    S, q_w = o.shape
    gw = q_w // G
    qpk = gw // HEAD_DIM
    tq = _pick(S, (512, 256, 128))

    def body(do_ref, o_ref, d_ref):
        lane = lax.broadcasted_iota(jnp.int32, (tq, LANES), 1)
        out = jnp.zeros((tq, LANES), F32)
        for h in range(qpk):
            hs = slice(h * HEAD_DIM, (h + 1) * HEAD_DIM)
            out = jnp.where(lane == h, jnp.sum(do_ref[:, hs] * o_ref[:, hs], axis=-1, keepdims=True), out)
        d_ref[...] = out

    return pl.pallas_call(
        body, name=name, grid=(G, S // tq), out_shape=jax.ShapeDtypeStruct((G, S, LANES), F32),
        in_specs=[pl.BlockSpec((tq, gw), lambda g, i: (i, g)), pl.BlockSpec((tq, gw), lambda g, i: (i, g))],
        out_specs=pl.BlockSpec((None, tq, LANES), lambda g, i: (g, i, 0)),
        compiler_params=_params("parallel", "parallel"),
    )(do, o)


def _flash_bwd(q, k, v, do, lse, delta, *, S, name):
    NK, kv_w = k.shape
    G = kv_w // HEAD_DIM
    qpk = q.shape[1] // kv_w
    gw = qpk * HEAD_DIM
    tq = _pick(S, (512, 256, 128))
    tk = _pick(NK, (768, 512, 256, 128))

    def body(q_ref, k_ref, v_ref, do_ref, lse_ref, dl_ref, dq_ref, dk_ref, dv_ref, lse_s, dl_s):
        qi, ki = pl.program_id(1), pl.program_id(2)

        @pl.when((qi == 0) & (ki == 0))
        def _():
            dk_ref[...] = jnp.zeros(dk_ref.shape, F32)
            dv_ref[...] = jnp.zeros(dv_ref.shape, F32)

        @pl.when(ki == 0)
        def _():
            dq_ref[...] = jnp.zeros(dq_ref.shape, F32)
            for h in range(qpk):
                lse_s[h] = _lane_pick(lse_ref[...], h)
                dl_s[h] = _lane_pick(dl_ref[...], h)

        k_t, v_t = k_ref[...], v_ref[...]
        dk_acc = jnp.zeros((tk, HEAD_DIM), F32)
        dv_acc = jnp.zeros((tk, HEAD_DIM), F32)
        for h in range(qpk):
            hs = slice(h * HEAD_DIM, (h + 1) * HEAD_DIM)
            q_h = q_ref[:, hs]
            do_h = do_ref[:, hs].astype(BF16)
            s = lax.dot_general(q_h, k_t, NT, preferred_element_type=F32)
            p = jnp.exp(s - lse_s[h])
            dv_acc = dv_acc + lax.dot_general(p.astype(BF16), do_h, TN, preferred_element_type=F32)
            dp = lax.dot_general(do_h, v_t, NT, preferred_element_type=F32)
            ds = (p * (dp - dl_s[h])).astype(BF16)
            dq_ref[:, hs] += lax.dot_general(ds, k_t, NN, preferred_element_type=F32)
            dk_acc = dk_acc + lax.dot_general(ds, q_h, TN, preferred_element_type=F32)
        rows = pl.ds(pl.multiple_of(ki * tk, tk), tk)
        dk_ref[rows, :] += dk_acc
        dv_ref[rows, :] += dv_acc

    qspec = pl.BlockSpec((tq, gw), lambda g, i, j: (i, g))
    kspec = pl.BlockSpec((tk, HEAD_DIM), lambda g, i, j: (j, g))
    lspec = pl.BlockSpec((None, tq, LANES), lambda g, i, j: (g, i, 0))
    group = pl.BlockSpec((NK, HEAD_DIM), lambda g, i, j: (0, g))
    kv_shape = jax.ShapeDtypeStruct((NK, kv_w), F32)
    return pl.pallas_call(
        body, name=name, grid=(G, S // tq, NK // tk),
        out_shape=(jax.ShapeDtypeStruct((S, G * gw), F32), kv_shape, kv_shape),
        in_specs=[qspec, kspec, kspec, qspec, lspec, lspec], out_specs=(qspec, group, group),
        scratch_shapes=[pltpu.VMEM((qpk, tq, 1), F32), pltpu.VMEM((qpk, tq, 1), F32)],
        compiler_params=_params("arbitrary", "arbitrary", "arbitrary"),
    )(q, k, v, do, lse, delta)


def _wide_specs(col0, width, bw):
    return [pl.BlockSpec((ROW_TILE, bw), functools.partial(lambda i, c: (i, c), c=col0 // bw + p))
            for p in range(width // bw)]


def _cat(refs):
    return refs[0][...] if len(refs) == 1 else jnp.concatenate([r[...] for r in refs], axis=1)


def _gelu(x):
    return 0.5 * x * (1.0 + lax.erf(x * (1.0 / math.sqrt(2.0))))


def _gelu_grad(x):
    return 0.5 * (1.0 + lax.erf(x * (1.0 / math.sqrt(2.0)))) + x * jnp.exp(-0.5 * x * x) * (1.0 / math.sqrt(2.0 * math.pi))


def _layernorm_stats(v):
    mu = jnp.mean(v, axis=-1, keepdims=True)
    xc = v - mu
    rstd = lax.rsqrt(jnp.mean(xc * xc, axis=-1, keepdims=True) + EPS)
    return xc * rstd, rstd


def _gmlp_fwd(zx, ln_w, ln_b, w_s, b_sb, *, S, col0, name):
    G, W = w_s.shape[0], ln_w.shape[1]
    gd = W // G
    bw = math.gcd(col0, W)
    n_parts = W // bw
    vec = pl.BlockSpec((1, W), lambda i: (0, 0))
    full3 = pl.BlockSpec((G, CHUNK, CHUNK), lambda i: (0, 0, 0))
    full3b = pl.BlockSpec((G, CHUNK, gd), lambda i: (0, 0, 0))

    def body(*refs):
        u_refs, v_refs = refs[:n_parts], refs[n_parts:2 * n_parts]
        lnw_ref, lnb_ref, ws_ref, bs_ref, gm_ref = refs[2 * n_parts:]
        u = _gelu(_cat(u_refs))
        vhat, _ = _layernorm_stats(_gelu(_cat(v_refs)))
        vn = (vhat * lnw_ref[...] + lnb_ref[...]).astype(BF16)
        for c in range(ROW_TILE // CHUNK):
            rs = slice(c * CHUNK, (c + 1) * CHUNK)
            for g in range(G):
                cs = slice(g * gd, (g + 1) * gd)
                mixed = lax.dot_general(ws_ref[g].astype(BF16), vn[rs, cs], NN, preferred_element_type=F32) + bs_ref[g]
                gm_ref[rs, cs] = (u[rs, cs] * mixed).astype(BF16)

    return pl.pallas_call(
        body, name=name, grid=(S // ROW_TILE,), out_shape=jax.ShapeDtypeStruct((S, W), BF16),
        in_specs=_wide_specs(col0, W, bw) + _wide_specs(col0 + W, W, bw) + [vec, vec, full3, full3b],
        out_specs=pl.BlockSpec((ROW_TILE, W), lambda i: (i, 0)), compiler_params=_params("parallel"),
    )(*([zx] * (2 * n_parts)), ln_w, ln_b, w_s, b_sb)


def _gmlp_bwd(zx, dgm, ln_w, ln_b, w_s, b_sb, *, S, col0, name):
    G, W = w_s.shape[0], ln_w.shape[1]
    gd = W // G
    bw = math.gcd(col0, W)
    n_parts = W // bw
    vec = pl.BlockSpec((1, W), lambda i: (0, 0))
    full3 = pl.BlockSpec((G, CHUNK, CHUNK), lambda i: (0, 0, 0))
    full3b = pl.BlockSpec((G, CHUNK, gd), lambda i: (0, 0, 0))
    row = pl.BlockSpec((ROW_TILE, W), lambda i: (i, 0))

    def body(*refs):
        u_refs, v_refs = refs[:n_parts], refs[n_parts:2 * n_parts]
        dgm_ref, lnw_ref, lnb_ref, ws_ref, bs_ref, dz_ref, dws_ref, dbs_ref, dlnw_ref, dlnb_ref, du_s, dvn_s = refs[2 * n_parts:]
        i = pl.program_id(0)
        upre, vpre = _cat(u_refs), _cat(v_refs)
        u = _gelu(upre)
        vhat, rstd = _layernorm_stats(_gelu(vpre))
        lnw = lnw_ref[...]
        vn = (vhat * lnw + lnb_ref[...]).astype(BF16)
        dgm_t = dgm_ref[...]

        @pl.when(i == 0)
        def _():
            dws_ref[...] = jnp.zeros(dws_ref.shape, F32)
            dbs_ref[...] = jnp.zeros(dbs_ref.shape, F32)

        for c in range(ROW_TILE // CHUNK):
            rs = slice(c * CHUNK, (c + 1) * CHUNK)
            for g in range(G):
                cs = slice(g * gd, (g + 1) * gd)
                ws_g = ws_ref[g].astype(BF16)
                vn_cg = vn[rs, cs]
                mixed = lax.dot_general(ws_g, vn_cg, NN, preferred_element_type=F32) + bs_ref[g]
                dgm_cg = dgm_t[rs, cs]
                du_s[rs, cs] = dgm_cg * mixed
                dmixed = dgm_cg * u[rs, cs]
                dmixed_b = dmixed.astype(BF16)
                dws_ref[g] += lax.dot_general(dmixed_b, vn_cg, NT, preferred_element_type=F32)
                dbs_ref[g] += dmixed
                dvn_s[rs, cs] = lax.dot_general(ws_g, dmixed_b, TN, preferred_element_type=F32)

        dvn = dvn_s[...]
        _accumulate(dlnw_ref, jnp.sum(dvn * vhat, axis=0, keepdims=True), i == 0)
        _accumulate(dlnb_ref, jnp.sum(dvn, axis=0, keepdims=True), i == 0)
        dvhat = dvn * lnw
        dv = rstd * (dvhat - jnp.mean(dvhat, axis=-1, keepdims=True)
                     - vhat * jnp.mean(dvhat * vhat, axis=-1, keepdims=True))
        dz_ref[:, :W] = (du_s[...] * _gelu_grad(upre)).astype(BF16)
        dz_ref[:, W:] = (dv * _gelu_grad(vpre)).astype(BF16)

    return pl.pallas_call(
        body, name=name, grid=(S // ROW_TILE,),
        out_shape=(jax.ShapeDtypeStruct((S, 2 * W), BF16), jax.ShapeDtypeStruct((G, CHUNK, CHUNK), F32),
                   jax.ShapeDtypeStruct((G, CHUNK, gd), F32), jax.ShapeDtypeStruct((1, W), F32),
                   jax.ShapeDtypeStruct((1, W), F32)),
        in_specs=_wide_specs(col0, W, bw) + _wide_specs(col0 + W, W, bw) + [row, vec, vec, full3, full3b],
        out_specs=(pl.BlockSpec((ROW_TILE, 2 * W), lambda i: (i, 0)), full3, full3b, vec, vec),
        scratch_shapes=[pltpu.VMEM((ROW_TILE, W), F32), pltpu.VMEM((ROW_TILE, W), F32)],
        compiler_params=_params("arbitrary"),
    )(*([zx] * (2 * n_parts)), dgm, ln_w, ln_b, w_s, b_sb)


def _merge_fwd(zx, a_br, g_br, b_gate, *, S, col0, name):
    D = a_br.shape[1]
    cw = min(math.gcd(col0, D), 1024)
    nc = D // cw
    c0 = col0 // cw
    blk = lambda f: pl.BlockSpec((ROW_TILE, cw), f)
    bias = lambda t: pl.BlockSpec((None, 1, cw), lambda i, j: (t, 0, j))

    def body(l0_ref, l1_ref, a_ref, g_ref, b0_ref, b1_ref, t_ref):
        g0 = _sigmoid(l0_ref[...] + b0_ref[...])
        g1 = _sigmoid(l1_ref[...] + b1_ref[...])
        t_ref[...] = (g0 * a_ref[...] + g1 * g_ref[...]).astype(BF16)

    return pl.pallas_call(
        body, name=name, grid=(S // ROW_TILE, nc), out_shape=jax.ShapeDtypeStruct((S, D), BF16),
        in_specs=[blk(lambda i, j: (i, c0 + j)), blk(lambda i, j: (i, c0 + nc + j)), blk(lambda i, j: (i, j)),
                  blk(lambda i, j: (i, j)), bias(0), bias(1)],
        out_specs=blk(lambda i, j: (i, j)), compiler_params=_params("parallel", "parallel"),
    )(zx, zx, a_br, g_br, b_gate, b_gate)


def _merge_bwd(dt, zx, a_br, g_br, b_gate, *, S, col0, name):
    D = a_br.shape[1]
    cw = min(math.gcd(col0, D), 1024)
    nc = D // cw
    c0 = col0 // cw
    blk = lambda f: pl.BlockSpec((ROW_TILE, cw), f)
    bias = lambda t: pl.BlockSpec((None, 1, cw), lambda j, i: (t, 0, j))
    own = blk(lambda j, i: (i, j))
    acc = pl.BlockSpec((1, cw), lambda j, i: (0, j))

    def body(dt_ref, l0_ref, l1_ref, a_ref, g_ref, b0_ref, b1_ref, da_ref, dg_ref, dl0_ref, dl1_ref, db0_ref, db1_ref):
        i = pl.program_id(1)
        dt_t = dt_ref[...]
        g0 = _sigmoid(l0_ref[...] + b0_ref[...])
        g1 = _sigmoid(l1_ref[...] + b1_ref[...])
        da_ref[...] = (dt_t * g0).astype(BF16)
        dg_ref[...] = (dt_t * g1).astype(BF16)
        dl0 = dt_t * a_ref[...] * (g0 * (1.0 - g0))
        dl1 = dt_t * g_ref[...] * (g1 * (1.0 - g1))
        dl0_ref[...] = dl0.astype(BF16)
        dl1_ref[...] = dl1.astype(BF16)
        _accumulate(db0_ref, jnp.sum(dl0, axis=0, keepdims=True), i == 0)
        _accumulate(db1_ref, jnp.sum(dl1, axis=0, keepdims=True), i == 0)

    sd = lambda dt_: jax.ShapeDtypeStruct((S, D), dt_)
    return pl.pallas_call(
        body, name=name, grid=(nc, S // ROW_TILE),
        out_shape=(sd(BF16), sd(BF16), sd(BF16), sd(BF16), jax.ShapeDtypeStruct((1, D), F32),
                   jax.ShapeDtypeStruct((1, D), F32)),
        in_specs=[own, blk(lambda j, i: (i, c0 + j)), blk(lambda j, i: (i, c0 + nc + j)), own, own, bias(0), bias(1)],
        out_specs=(own, own, own, own, acc, acc), compiler_params=_params("parallel", "arbitrary"),
    )(dt, zx, zx, a_br, g_br, b_gate, b_gate)


def _mod_fwd(cvecs, w_mod, name):
    R, D = cvecs.shape
    nsh = w_mod.shape[1]
    tn = _pick(nsh, (512, 256, 128))

    def body(c_ref, w_ref, o_ref):
        cv = c_ref[...]
        a = (cv * _sigmoid(cv)).astype(BF16)
        o_ref[...] = lax.dot_general(a, w_ref[...].astype(BF16), NN, preferred_element_type=F32)

    return pl.pallas_call(
        body, name=name, grid=(nsh // tn,), out_shape=jax.ShapeDtypeStruct((R, nsh), F32),
        in_specs=[pl.BlockSpec((R, D), lambda j: (0, 0)), pl.BlockSpec((D, tn), lambda j: (0, j))],
        out_specs=pl.BlockSpec((R, tn), lambda j: (0, j)), compiler_params=_params("parallel"),
    )(cvecs, w_mod)


def _mod_wgrad(cvecs, dm, name):
    R, D = cvecs.shape
    nsh = dm.shape[1]
    tn = _pick(nsh, (512, 256, 128))

    def body(c_ref, dm_ref, o_ref):
        cv = c_ref[...]
        a = (cv * _sigmoid(cv)).astype(BF16)
        o_ref[...] = lax.dot_general(a, dm_ref[...].astype(BF16), TN, preferred_element_type=F32)

    return pl.pallas_call(
        body, name=name, grid=(nsh // tn,), out_shape=jax.ShapeDtypeStruct((D, nsh), F32),
        in_specs=[pl.BlockSpec((R, D), lambda j: (0, 0)), pl.BlockSpec((R, tn), lambda j: (0, j))],
        out_specs=pl.BlockSpec((D, tn), lambda j: (0, j)), compiler_params=_params("parallel"),
    )(cvecs, dm)


def _mod_dgrad(dm, w_mod, name):
    R, nsh = dm.shape
    D = w_mod.shape[0]
    tn = _pick(D, (256, 128))

    def body(dm_ref, w_ref, o_ref):
        o_ref[...] = lax.dot_general(dm_ref[...].astype(BF16), w_ref[...].astype(BF16), NT, preferred_element_type=F32)

    return pl.pallas_call(
        body, name=name, grid=(D // tn,), out_shape=jax.ShapeDtypeStruct((R, D), F32),
        in_specs=[pl.BlockSpec((R, nsh), lambda j: (0, 0)), pl.BlockSpec((tn, nsh), lambda j: (j, 0))],
        out_specs=pl.BlockSpec((R, tn), lambda j: (0, j)), compiler_params=_params("parallel"),
    )(dm, w_mod)


def _adam_rows(R, C):
    return _pick(R, [t for t in (512, 256, 128, 64, 32, 16) if t * C * 4 <= ADAM_TILE_BYTES] or [16])


def _adamw(w, m, v, grads, name):
    R, C = w.shape
    tr = _adam_rows(R, C)
    n_g = len(grads)
    blk = pl.BlockSpec((tr, C), lambda i: (i, 0))
    c1 = 1.0 - ADAM_B1 ** ADAM_STEP
    c2 = 1.0 - ADAM_B2 ** ADAM_STEP

    def body(*refs):
        w_ref, m_ref, v_ref = refs[:3]
        g_refs = refs[3:3 + n_g]
        g_ref, d_ref, m2_ref, v2_ref = refs[3 + n_g:]
        g = g_refs[0][...].astype(F32)
        for r in g_refs[1:]:
            g = g + r[...].astype(F32)
        m2 = ADAM_B1 * m_ref[...] + (1.0 - ADAM_B1) * g
        v2 = ADAM_B2 * v_ref[...] + (1.0 - ADAM_B2) * (g * g)
        g_ref[...] = g
        m2_ref[...] = m2
        v2_ref[...] = v2
        d_ref[...] = -ADAM_LR * ((m2 / c1) / (jnp.sqrt(v2 / c2) + ADAM_EPS) + ADAM_WD * w_ref[...])

    out = jax.ShapeDtypeStruct((R, C), F32)
    return pl.pallas_call(
        body, name=name, grid=(R // tr,), out_shape=(out, out, out, out),
        in_specs=[blk] * (3 + n_g), out_specs=(blk, blk, blk, blk), compiler_params=_params("parallel"),
    )(w, m, v, *grads)


def _sum_slabs(slabs, name):
    n, R, C = slabs.shape
    tr = _adam_rows(R, C)

    def body(s_ref, o_ref):
        acc = s_ref[0].astype(F32)
        for k in range(1, n):
            acc = acc + s_ref[k].astype(F32)
        o_ref[...] = acc.astype(BF16)

    return pl.pallas_call(
        body, name=name, grid=(R // tr,), out_shape=jax.ShapeDtypeStruct((R, C), BF16),
        in_specs=[pl.BlockSpec((n, tr, C), lambda i: (0, i, 0))],
        out_specs=pl.BlockSpec((tr, C), lambda i: (i, 0)), compiler_params=_params("parallel"),
    )(slabs)


def _plane_peers():
    x, y = lax.axis_index("x"), lax.axis_index("y")
    return [(1 - x, y), (x, 1 - y), (1 - x, 1 - y)]


class _Rider:
    def __init__(self, arrays, out_shapes, copies):
        n = len(arrays)
        self.arrays, self.out_shapes = list(arrays), list(out_shapes)
        self.sems = [pltpu.SemaphoreType.DMA((3 * n,)), pltpu.SemaphoreType.DMA((3 * n,)), pltpu.SemaphoreType.DMA((n,))]
        self._copies = copies

    def start(self, ins, outs, sems):
        local, sends, _ = self._copies(ins, outs, sems)
        for cp in local + sends:
            cp.start()

    def finish(self, ins, outs, sems):
        local, sends, recvs = self._copies(ins, outs, sems)
        for cp in recvs:
            cp.wait_recv()
        for cp in sends:
            cp.wait_send()
        for cp in local:
            cp.wait()


def _gather_rider(shards):
    n = len(shards)

    def copies(ins, outs, sems):
        send_sems, recv_sems, local_sems = sems
        x, y, c = lax.axis_index("x"), lax.axis_index("y"), lax.axis_index("c")
        me = 2 * x + y
        peers = _plane_peers()

        def remote(w, k, slab):
            px, py = peers[k]
            return pltpu.make_async_remote_copy(
                src_ref=ins[w], dst_ref=outs[w].at[slab], send_sem=send_sems.at[3 * w + k],
                recv_sem=recv_sems.at[3 * w + k], device_id=(px, py, c), device_id_type=MESH)

        local = [pltpu.make_async_copy(ins[w], outs[w].at[me], local_sems.at[w]) for w in range(n)]
        sends = [remote(w, k, me) for w in range(n) for k in range(3)]
        recvs = [remote(w, k, 2 * px + py) for w in range(n) for k, (px, py) in enumerate(peers)]
        return local, sends, recvs

    return _Rider(shards, [jax.ShapeDtypeStruct((N_SHARDS,) + s.shape, s.dtype) for s in shards], copies)


def _scatter_rider(fulls):
    n = len(fulls)

    def copies(ins, outs, sems):
        send_sems, recv_sems, local_sems = sems
        x, y, c = lax.axis_index("x"), lax.axis_index("y"), lax.axis_index("c")
        me = 2 * x + y
        peers = _plane_peers()

        def remote(w, k):
            px, py = peers[k]
            return pltpu.make_async_remote_copy(
                src_ref=ins[w].at[2 * px + py], dst_ref=outs[w].at[k], send_sem=send_sems.at[3 * w + k],
                recv_sem=recv_sems.at[3 * w + k], device_id=(px, py, c), device_id_type=MESH)

        local = [pltpu.make_async_copy(ins[w].at[me], outs[w].at[3], local_sems.at[w]) for w in range(n)]
        sends = [remote(w, k) for w in range(n) for k in range(3)]
        return local, sends, sends

    return _Rider(fulls, [jax.ShapeDtypeStruct(f.shape, f.dtype) for f in fulls], copies)


def _comm_call(rider, name):
    n = len(rider.arrays)

    def body(*refs):
        ins, outs, sems = refs[:n], refs[n:2 * n], refs[2 * n:]
        rider.start(ins, outs, sems)
        rider.finish(ins, outs, sems)

    return list(pl.pallas_call(
        body, name=name, out_shape=tuple(rider.out_shapes), in_specs=[HBM_SPEC] * n, out_specs=tuple([HBM_SPEC] * n),
        scratch_shapes=rider.sems,
    )(*rider.arrays))


def _sibling_exchange(blocks, name):
    n = len(blocks)

    def body(*refs):
        ins, outs = refs[:n], refs[n:2 * n]
        send_sems, recv_sems = refs[2 * n:]
        sibling = (lax.axis_index("x"), lax.axis_index("y"), 1 - lax.axis_index("c"))
        sends = [pltpu.make_async_remote_copy(src_ref=ins[w], dst_ref=outs[w], send_sem=send_sems.at[w],
                                              recv_sem=recv_sems.at[w], device_id=sibling, device_id_type=MESH)
                 for w in range(n)]
        for cp in sends:
            cp.start()
        for cp in sends:
            cp.wait_recv()
        for cp in sends:
            cp.wait_send()

    return pl.pallas_call(
        body, name=name, out_shape=tuple(jax.ShapeDtypeStruct(b.shape, b.dtype) for b in blocks),
        in_specs=[HBM_SPEC] * n, out_specs=tuple([HBM_SPEC] * n),
        scratch_shapes=[pltpu.SemaphoreType.DMA((n,)), pltpu.SemaphoreType.DMA((n,))],
    )(*blocks)


def _allgather_devices(block, name):
    m_per, n_cols = block.shape

    def body(x_ref, out_ref, send_sems, recv_sems, local_sem):
        x, y, c = lax.axis_index("x"), lax.axis_index("y"), lax.axis_index("c")
        me, sibling = (x, y, c), (x, y, 1 - c)
        chips = _plane_peers()

        def rows(px, py, pc):
            return out_ref.at[pl.ds((4 * px + 2 * py + pc) * m_per, m_per), :]

        def copy(k, blk, to, src=None):
            return pltpu.make_async_remote_copy(
                src_ref=rows(*blk) if src is None else src, dst_ref=rows(*blk), send_sem=send_sems.at[k],
                recv_sem=recv_sems.at[k], device_id=to, device_id_type=MESH)

        mine = pltpu.make_async_copy(x_ref, rows(*me), local_sem)
        mine.start()
        first = [copy(0, me, sibling, src=x_ref)]
        first += [copy(1 + j, me, (*chip, c), src=x_ref) for j, chip in enumerate(chips)]
        for cp in first:
            cp.start()
        passed = [copy(4 + j, (*chip, c), sibling) for j, chip in enumerate(chips)]
        for j, chip in enumerate(chips):
            copy(1 + j, (*chip, c), me).wait_recv()
            passed[j].start()
        copy(0, sibling, me).wait_recv()
        for j, chip in enumerate(chips):
            copy(4 + j, (*chip, 1 - c), me).wait_recv()
        for cp in first + passed:
            cp.wait_send()
        mine.wait()

    out = pl.pallas_call(
        body, name=name, out_shape=jax.ShapeDtypeStruct((N_DEV * m_per, n_cols), block.dtype),
        in_specs=[VMEM_SPEC], out_specs=VMEM_SPEC,
        scratch_shapes=[pltpu.SemaphoreType.DMA((7,)), pltpu.SemaphoreType.DMA((7,)), pltpu.SemaphoreType.DMA],
        compiler_params=pltpu.CompilerParams(vmem_limit_bytes=VMEM_LIMIT_BYTES),
    )(block)
    return out.reshape(N_DEV, m_per, n_cols)


PACK_ROWS = 16


def _pack_rows(shape, width):
    return -(-math.prod(shape) // (width * PACK_ROWS)) * PACK_ROWS


def _pack(arrays, width):
    rows = []
    for a in arrays:
        flat = a.reshape(-1).astype(F32)
        n_rows = _pack_rows(a.shape, width)
        rows.append(jnp.pad(flat, (0, n_rows * width - flat.shape[0])).reshape(n_rows, width))
    return jnp.concatenate(rows, axis=0)


def _unpack(packed, shapes, width):
    out, r = [], 0
    for shp in shapes:
        size = math.prod(shp)
        n_rows = _pack_rows(shp, width)
        out.append(packed[r:r + n_rows].reshape(-1)[:size].reshape(shp))
        r += n_rows
    return out


def _rope_tables(S, C):
    rows = S // GRID_W
    axis_dim = HEAD_DIM // 2
    row = jnp.broadcast_to(jnp.arange(rows, dtype=F32)[:, None], (rows, GRID_W)).reshape(-1)
    col = jnp.broadcast_to(jnp.arange(GRID_W, dtype=F32)[None, :], (rows, GRID_W)).reshape(-1)
    inv_freq = ROPE_THETA ** (-jnp.arange(0, axis_dim, 2, dtype=F32) / axis_dim)
    ang = jnp.concatenate([row[:, None] * inv_freq, col[:, None] * inv_freq], axis=-1)
    cos, sin = jnp.cos(ang), jnp.sin(ang)
    cos2 = jnp.repeat(cos, 2, axis=-1)
    sin2 = jnp.stack([-sin, sin], axis=-1).reshape(S, HEAD_DIM)
    cos2 = jnp.concatenate([cos2, jnp.ones((C, HEAD_DIM), F32)], axis=0)
    sin2 = jnp.concatenate([sin2, jnp.zeros((C, HEAD_DIM), F32)], axis=0)
    return cos2, sin2


def kernel(x, c, ctx, c_ctx, w_mod, b_mod, norm_w, w_ffn1_in, w_ffn1_out, w_ffn2_in, w_ffn2_out, w_in, b_gate, q_norm_w, k_norm_w, gmlp_ln_w, gmlp_ln_b, w_spatial, b_spatial, w_branch_attn, w_branch_gmlp, w_out, final_norm_w, loss_target, m_c_ctx, m_w_mod, m_b_mod, m_norm_w, m_w_ffn1_in, m_w_ffn1_out, m_w_ffn2_in, m_w_ffn2_out, m_w_in, m_b_gate, m_q_norm_w, m_k_norm_w, m_gmlp_ln_w, m_gmlp_ln_b, m_w_spatial, m_b_spatial, m_w_branch_attn, m_w_branch_gmlp, m_w_out, m_final_norm_w, v_c_ctx, v_w_mod, v_b_mod, v_norm_w, v_w_ffn1_in, v_w_ffn1_out, v_w_ffn2_in, v_w_ffn2_out, v_w_in, v_b_gate, v_q_norm_w, v_k_norm_w, v_gmlp_ln_w, v_gmlp_ln_b, v_w_spatial, v_b_spatial, v_w_branch_attn, v_w_branch_gmlp, v_w_out, v_final_norm_w):
    _, S, D = x.shape
    C = ctx.shape[1]
    NTOK = S + C
    n_xt = S // ROW_TILE
    q_w, kv_w = N_Q_HEADS * HEAD_DIM, N_KV_HEADS * HEAD_DIM
    W = gmlp_ln_w.shape[1]
    v_end = q_w + 2 * kv_w
    gv_end = v_end + 2 * W
    scale = HEAD_DIM ** -0.5
    dev = 4 * lax.axis_index("x") + 2 * lax.axis_index("y") + lax.axis_index("c")
    shard = 2 * lax.axis_index("x") + lax.axis_index("y")

    c_all = _allgather_devices(jnp.pad(c, ((0, 7), (0, 0))), "gather_c")[:, 0, :]
    cvecs = jnp.concatenate([c_all, jnp.pad(c_ctx[None, :], ((0, 7), (0, 0)))], axis=0)
    w_mod_l = w_mod[0]
    n_modsh = w_mod_l.shape[1]
    mod_part = _mod_fwd(cvecs, w_mod_l, "mod_fwd")
    mod_all = _allgather_devices(mod_part, "gather_mod")[0::2]
    mod_full = jnp.transpose(mod_all, (1, 0, 2)).reshape(16, N_SHARDS * n_modsh) + b_mod
    mx = lax.dynamic_index_in_dim(mod_full, dev, 0, keepdims=False).reshape(N_MOD, D)
    mc = mod_full[8].reshape(N_MOD, D)
    mods = jnp.concatenate([mx, mc], axis=0).reshape(2 * N_MOD, 1, D)

    shard_of = lambda w: w[0].astype(BF16)
    rows_of = lambda g: g.reshape(-1, g.shape[-1])
    (wf1i,) = _comm_call(_gather_rider([shard_of(w_ffn1_in)]), "gather_ffn1_in")
    n_vsh = norm_w.shape[-1]
    nw_all = _allgather_devices(_pack([norm_w[0], b_gate[0]], n_vsh), "gather_vecs")[0::2]
    nw = jnp.transpose(nw_all[:, 0:3, :], (1, 0, 2)).reshape(3, D)
    bg = jnp.transpose(nw_all[:, PACK_ROWS:PACK_ROWS + 2, :], (1, 0, 2)).reshape(2, 1, D)
    nw0, nw1, nw2 = nw[0:1], nw[1:2], nw[2:3]

    cos2, sin2 = _rope_tables(S, C)
    b_sb = jnp.broadcast_to(b_spatial[0][:, :, None], (GMLP_GROUPS, CHUNK, W // GMLP_GROUPS))
    w_s = w_spatial[0]
    fw = final_norm_w[None, :]

    tok0 = jnp.concatenate([x[0], ctx[0]], axis=0)
    h1 = _normmod(tok0, mods, nw0, k_shift=0, k_scale=1, n_xt=n_xt, name="ffn1_norm")
    z1, (wf1o_g,) = _matmul(h1, wf1i, form="nn", b_shards=N_SHARDS, name="ffn1_in",
                            rider=_gather_rider([shard_of(w_ffn1_out)]))
    wf1o = rows_of(wf1o_g)
    g1 = _swiglu_fwd(z1, "ffn1_act")
    y1, (win,) = _matmul(g1, wf1o, form="nn", name="ffn1_out", tn=1024, rider=_gather_rider([shard_of(w_in)]))
    tok1, h2 = _normmod(tok0, mods, nw1, k_shift=3, k_scale=4, n_xt=n_xt, name="mix_norm",
                        resid=(y1, 2, MACARON_WEIGHT))
    later = [w_ffn2_in, w_ffn2_out, w_branch_attn, w_branch_gmlp, w_out]
    zx, (wf2i, wf2o_g, wba_g, wbg_g, wo_g) = _matmul(h2, win, form="nn", b_shards=N_SHARDS, name="mix_in",
                                                     rider=_gather_rider([shard_of(w) for w in later]))
    wf2o, wba, wbg, wo = rows_of(wf2o_g), rows_of(wba_g), rows_of(wbg_g), rows_of(wo_g)
    qt, kt, vt = _qk_prep(zx, cos2, sin2, q_norm_w, k_norm_w, q_w=q_w, kv_w=kv_w, scale=scale, name="qk_prep")
    attn, lse = _flash_fwd(qt, kt, vt, S=S, name="attn_fwd")
    gm = _gmlp_fwd(zx, gmlp_ln_w, gmlp_ln_b, w_s, b_sb, S=S, col0=v_end, name="gmlp_fwd")
    a_br = _matmul(attn, wba, form="nn", name="branch_attn", tm=512)
    g_br = _matmul(gm, wbg, form="nn", name="branch_gmlp", tm=512)
    t_mix = _merge_fwd(zx, a_br, g_br, bg, S=S, col0=gv_end, name="merge_fwd")
    y_mix = _matmul(t_mix, wo, form="nn", name="mix_out", tm=512)
    x2, h3 = _normmod(tok1, mods, nw2, k_shift=6, k_scale=7, n_xt=n_xt, name="ffn2_norm", resid=(y_mix, 5, None),
                      rows=S)
    z2 = _matmul(h3, wf2i, form="nn", b_shards=N_SHARDS, name="ffn2_in")
    g2 = _swiglu_fwd(z2, "ffn2_act")
    y2 = _matmul(g2, wf2o, form="nn", name="ffn2_out", tn=1024)
    x3 = _resid_only(x2, y2, mods, k_gate=8, name="ffn2_resid")

    dx3, loss_row, dfw = _final_loss(x3, loss_target[0], fw, "loss")

    dy2, dgate8 = _resid_bwd(dx3, y2, mods, k_gate=8, weight=MACARON_WEIGHT, n_xt=n_xt, name="ffn2_resid_bwd")
    dg2 = _matmul(dy2, wf2o, form="nt", name="ffn2_out_dgrad")
    gw_f2o = _matmul(g2, dy2, form="tn", out_dtype=BF16, name="ffn2_out_wgrad", tn=1024)
    dz2 = _swiglu_bwd(z2, dg2, "ffn2_act_bwd")
    dh3 = _matmul(dz2, wf2i, form="nt", b_shards=N_SHARDS, name="ffn2_in_dgrad", tn=1024)
    gw_f2i = _matmul(h3, dz2, form="tn", out_dtype=BF16, out_shards=N_SHARDS, name="ffn2_in_wgrad")
    dx2, dsh6, dsc7, dnw2 = _normmod_bwd(dh3, x2, dx3, mods, nw2, k_scale=7, n_xt=n_xt, name="ffn2_norm_bwd")

    dy_mix, dgate5 = _resid_bwd(dx2, y_mix, mods, k_gate=5, weight=None, n_xt=n_xt, name="mix_resid_bwd")
    dt = _matmul(dy_mix, wo, form="nt", name="mix_out_dgrad", tm=512)
    gw_wo = _matmul(t_mix, dy_mix, form="tn", out_dtype=BF16, name="mix_out_wgrad", tn=1024)
    d_abr, d_gbr, dl0, dl1, dbg0, dbg1 = _merge_bwd(dt, zx, a_br, g_br, bg, S=S, col0=gv_end, name="merge_bwd")
    d_attn = _matmul(d_abr, wba, form="nt", name="branch_attn_dgrad", tm=512)
    gw_wba = _matmul(attn, d_abr, form="tn", out_dtype=BF16, name="branch_attn_wgrad", tn=1024)
    d_gm = _matmul(d_gbr, wbg, form="nt", name="branch_gmlp_dgrad", tm=512)
    gw_wbg = _matmul(gm, d_gbr, form="tn", out_dtype=BF16, name="branch_gmlp_wgrad", tn=1024)
    dz_gm, dws, dbs_wide, dlnw, dlnb = _gmlp_bwd(zx, d_gm, gmlp_ln_w, gmlp_ln_b, w_s, b_sb, S=S, col0=v_end,
                                                 name="gmlp_bwd")
    delta = _attn_delta(d_attn, attn, G=N_KV_HEADS, name="attn_delta")
    dq, dk, dv = _flash_bwd(qt, kt, vt, d_attn, lse, delta, S=S, name="attn_bwd")
    dz_qkv, dqw, dkw = _qk_prep_bwd(zx, dq, dk, dv, cos2, sin2, q_norm_w, k_norm_w, q_w=q_w, kv_w=kv_w,
                                    scale=scale, n_xt=n_xt, name="qk_prep_bwd")
    ctx_pad = ((0, C), (0, 0))
    dzx = jnp.concatenate([dz_qkv, jnp.pad(dz_gm, ctx_pad), jnp.pad(dl0, ctx_pad), jnp.pad(dl1, ctx_pad)], axis=1)
    slabs_of = lambda g: g.reshape(N_SHARDS, g.shape[0] // N_SHARDS, g.shape[1])
    dh2, (rc_f2i,) = _matmul(dzx, win, form="nt", b_shards=N_SHARDS, name="mix_in_dgrad", tn=1024,
                             rider=_scatter_rider([gw_f2i]))
    gw_win, (rc_f2o, rc_wo, rc_wba, rc_wbg) = _matmul(
        h2, dzx, form="tn", out_dtype=BF16, out_shards=N_SHARDS, name="mix_in_wgrad",
        rider=_scatter_rider([slabs_of(g) for g in (gw_f2o, gw_wo, gw_wba, gw_wbg)]))
    dtok1, dsh3, dsc4, dnw1 = _normmod_bwd(dh2, tok1, dx2, mods, nw1, k_scale=4, n_xt=n_xt, name="mix_norm_bwd")

    dy1, dgate2 = _resid_bwd(dtok1, y1, mods, k_gate=2, weight=MACARON_WEIGHT, n_xt=n_xt, name="ffn1_resid_bwd")
    dg1 = _matmul(dy1, wf1o, form="nt", name="ffn1_out_dgrad")
    gw_f1o = _matmul(g1, dy1, form="tn", out_dtype=BF16, name="ffn1_out_wgrad", tn=1024)
    dz1 = _swiglu_bwd(z1, dg1, "ffn1_act_bwd")
    dh1, (rc_win,) = _matmul(dz1, wf1i, form="nt", b_shards=N_SHARDS, name="ffn1_in_dgrad", tn=1024,
                             rider=_scatter_rider([gw_win]))
    gw_f1i, (rc_f1o,) = _matmul(h1, dz1, form="tn", out_dtype=BF16, out_shards=N_SHARDS, name="ffn1_in_wgrad",
                                rider=_scatter_rider([slabs_of(gw_f1o)]))
    dtok0, dsh0, dsc1, dnw0 = _normmod_bwd(dh1, tok0, dtok1, mods, nw0, k_scale=1, n_xt=n_xt, name="ffn1_norm_bwd")
    grad_x = dtok0[:S][None]
    (rc_f1i,) = _comm_call(_scatter_rider([gw_f1i]), "scatter_ffn1_in")

    big_w = [w_ffn1_in, w_ffn2_in, w_in, w_ffn1_out, w_ffn2_out, w_branch_attn, w_branch_gmlp, w_out]
    big_m = [m_w_ffn1_in, m_w_ffn2_in, m_w_in, m_w_ffn1_out, m_w_ffn2_out, m_w_branch_attn, m_w_branch_gmlp, m_w_out]
    big_v = [v_w_ffn1_in, v_w_ffn2_in, v_w_in, v_w_ffn1_out, v_w_ffn2_out, v_w_branch_attn, v_w_branch_gmlp, v_w_out]
    big_names = ["w_ffn1_in", "w_ffn2_in", "w_in", "w_ffn1_out", "w_ffn2_out", "w_branch_attn", "w_branch_gmlp", "w_out"]
    received = [rc_f1i, rc_f2i, rc_win, rc_f1o, rc_f2o, rc_wba, rc_wbg, rc_wo]
    plane_sums = [_sum_slabs(r, "plane_sum_" + nm) for r, nm in zip(received, big_names)]
    sibling_sums = _sibling_exchange(plane_sums, "sibling_grads")
    big_out = {}
    for nm, w_, m_, v_, pa, pb in zip(big_names, big_w, big_m, big_v, plane_sums, sibling_sums):
        res = _adamw(w_[0], m_[0], v_[0], [pa, pb], "adamw_" + nm)
        big_out[nm] = [r[None] for r in res]

    zeros9 = jnp.zeros((1, D), F32)
    dmx = jnp.concatenate([dsh0[0], dsc1[0], dgate2[0], dsh3[0], dsc4[0], dgate5[0], dsh6[0], dsc7[0], dgate8[0]], axis=0)
    dmc = jnp.concatenate([dsh0[1], dsc1[1], dgate2[1], dsh3[1], dsc4[1], zeros9, zeros9, zeros9, zeros9], axis=0)
    dbs = jnp.sum(dbs_wide, axis=-1)
    parts = [dmx, dmc, jnp.concatenate([dnw0, dnw1, dnw2], axis=0), jnp.concatenate([dbg0, dbg1], axis=0),
             dqw, dkw, dlnw, dlnb, dws, dbs, dfw, loss_row[:, :1]]
    part_shapes = [p.shape for p in parts]
    small_all = _allgather_devices(_pack(parts, D), "gather_small")
    dmx_all = small_all[:, 0:N_MOD, :].reshape(N_DEV, N_MOD * D)
    small_sum = small_all[0]
    for d in range(1, N_DEV):
        small_sum = small_sum + small_all[d]
    (_, dmc_sum, g_nw, g_bg, g_qw, g_kw, g_lnw, g_lnb, g_ws, g_bs, g_fw, loss_sum) = _unpack(small_sum, part_shapes, D)
    loss = loss_sum[0, 0]
    dmx_sum = small_sum[0:N_MOD].reshape(1, N_MOD * D)
    g_b_mod = dmx_sum + dmc_sum.reshape(1, N_MOD * D)
    dm_rows = jnp.concatenate([dmx_all, jnp.pad(dmc_sum.reshape(1, N_MOD * D), ((0, 7), (0, 0)))], axis=0)
    dm_sh = lax.dynamic_slice_in_dim(dm_rows, shard * n_modsh, n_modsh, axis=1)
    g_w_mod = _mod_wgrad(cvecs, dm_sh, "mod_wgrad")
    dsc_part = _mod_dgrad(dm_sh, w_mod_l, "mod_dgrad")
    dsc_all = _allgather_devices(dsc_part, "gather_dsilu")[0::2, 8, :]
    dscc = ((dsc_all[0] + dsc_all[1]) + dsc_all[2]) + dsc_all[3]
    sg = _sigmoid(c_ctx)
    g_c_ctx = dscc * (sg * (1.0 + c_ctx * (1.0 - sg)))
    g_norm_w = lax.dynamic_slice_in_dim(g_nw, shard * n_vsh, n_vsh, axis=1)[None]
    g_b_gate = lax.dynamic_slice_in_dim(g_bg, shard * n_vsh, n_vsh, axis=1)[None]

    w_mod_out = [r[None] for r in _adamw(w_mod_l, m_w_mod[0], v_w_mod[0], [g_w_mod], "adamw_w_mod")]

    small_names = ["c_ctx", "b_mod", "norm_w", "b_gate", "q_norm_w", "k_norm_w", "gmlp_ln_w", "gmlp_ln_b",
                   "w_spatial", "b_spatial", "final_norm_w"]
    small_w = [c_ctx, b_mod, norm_w, b_gate, q_norm_w, k_norm_w, gmlp_ln_w, gmlp_ln_b, w_spatial, b_spatial, final_norm_w]
    small_m = [m_c_ctx, m_b_mod, m_norm_w, m_b_gate, m_q_norm_w, m_k_norm_w, m_gmlp_ln_w, m_gmlp_ln_b, m_w_spatial,
               m_b_spatial, m_final_norm_w]
    small_v = [v_c_ctx, v_b_mod, v_norm_w, v_b_gate, v_q_norm_w, v_k_norm_w, v_gmlp_ln_w, v_gmlp_ln_b, v_w_spatial,
               v_b_spatial, v_final_norm_w]
    small_g = [g_c_ctx, g_b_mod, g_norm_w, g_b_gate, g_qw, g_kw, g_lnw, g_lnb, g_ws, g_bs, g_fw]
    small_shapes = [w_.shape for w_ in small_w]
    packed = [_pack(group, D) for group in (small_w, small_m, small_v, small_g)]
    small_res = [_unpack(r, small_shapes, D) for r in _adamw(*packed[:3], [packed[3]], "adamw_small")]
    small_out = {nm: [small_res[t][i] for t in range(4)] for i, nm in enumerate(small_names)}

    order = ["c_ctx", "w_mod", "b_mod", "norm_w", "w_ffn1_in", "w_ffn1_out", "w_ffn2_in", "w_ffn2_out", "w_in", "b_gate",
             "q_norm_w", "k_norm_w", "gmlp_ln_w", "gmlp_ln_b", "w_spatial", "b_spatial", "w_branch_attn", "w_branch_gmlp",
             "w_out", "final_norm_w"]
    results = {**big_out, **small_out, "w_mod": w_mod_out}
    outs = [loss, grad_x]
    for t in range(4):
        outs += [results[nm][t] for nm in order]
    return tuple(outs)


def _resid_only(xp, y_in, mods, *, k_gate, name):
    M, D = xp.shape
    row = pl.BlockSpec((ROW_TILE, D), lambda i: (i, 0))

    def body(xp_ref, y_ref, g_ref, x_ref):
        x_ref[...] = xp_ref[...] + (MACARON_WEIGHT * g_ref[...]) * y_ref[...]

    return pl.pallas_call(
        body, name=name, grid=(M // ROW_TILE,), out_shape=jax.ShapeDtypeStruct((M, D), F32),
        in_specs=[row, row, pl.BlockSpec((None, 1, D), lambda i: (k_gate, 0, 0))], out_specs=row,
        compiler_params=_params("parallel"),
    )(xp, y_in, mods)
```

```python
import functools
import math

import jax
import jax.numpy as jnp
from jax import lax
from jax.experimental import pallas as pl
from jax.experimental.pallas import tpu as pltpu

F32 = jnp.float32
BF16 = jnp.bfloat16
MESH = pl.DeviceIdType.MESH
HBM_SPEC = pl.BlockSpec(memory_space=pltpu.HBM)
VMEM_SPEC = pl.BlockSpec(memory_space=pltpu.VMEM)

HEAD_DIM = 128
N_Q_HEADS = 16
N_KV_HEADS = 4
GMLP_GROUPS = 16
CHUNK = 128
GRID_W = 64
ROPE_THETA = 10000.0
EPS = 1e-6
MACARON_WEIGHT = 0.5
N_MOD = 9

ADAM_LR = 0.001
ADAM_B1 = 0.9
ADAM_B2 = 0.999
ADAM_EPS = 1e-08
ADAM_WD = 0.01
ADAM_STEP = 10

N_SHARDS = 4
N_DEV = 8
ROW_TILE = 256
KEY_TILES = (2816, 768, 512, 256, 128)
LANES = 128
VMEM_LIMIT_BYTES = 48 * 1024 * 1024
ADAM_TILE_BYTES = 1 << 20

NN = (((1,), (0,)), ((), ()))
NT = (((1,), (1,)), ((), ()))
TN = (((0,), (0,)), ((), ()))


def _pick(n, cands):
    for t in cands:
        if t <= n and n % t == 0:
            return t
    raise ValueError(f"no tile for {n} among {cands}")


def _params(*sem):
    return pltpu.CompilerParams(dimension_semantics=sem or None, vmem_limit_bytes=VMEM_LIMIT_BYTES)


def _sigmoid(x):
    return 1.0 / (1.0 + jnp.exp(-x))


def _matmul(a, b, *, form, name, out_dtype=F32, a_shards=1, b_shards=1, out_shards=1, tm=None, tn=None, tk=None,
            order="ji", rider=None):
    if form == "nn":
        M, K = a.shape
        N = b.shape[-1] * b_shards
    elif form == "nt":
        M, K = a.shape[-2], a.shape[-1] * a_shards
        N = b.shape[-2]
    else:
        K, M = a.shape
        N = b.shape[-1] * b_shards
    n_b = N // b_shards if form in ("nn", "tn") else N
    n_o = N // out_shards
    k_a = K // a_shards
    k_b = K // b_shards if form == "nt" else K
    tm = _pick(M, ((tm,) if tm else ()) + (1024, 768, 512, 256, 128, 64, 32, 16))
    tn = _pick(math.gcd(n_b, n_o), ((tn,) if tn else ()) + (1408, 1024, 512, 256, 128))
    tk = tk or _pick(math.gcd(k_a, k_b),
                     (1024, 768, 512, 256, 128) if form == "tn" else (2816, 2048, 1408, 1024, 768, 512, 256, 128))
    nk = K // tk
    nbb, nbo = n_b // tn, n_o // tn
    ka, kb = k_a // tk, k_b // tk
    dims = {"nn": NN, "nt": NT, "tn": TN}[form]

    def ij(g0, g1):
        return (g0, g1) if order == "ij" else (g1, g0)

    def a_map(g0, g1, k):
        i, _ = ij(g0, g1)
        if form == "tn":
            return (k, i)
        return (k // ka, i, k % ka) if a_shards > 1 else (i, k)

    def b_map(g0, g1, k):
        _, j = ij(g0, g1)
        if form == "nt":
            return (k // kb, j, k % kb) if b_shards > 1 else (j, k)
        return (j // nbb, k, j % nbb) if b_shards > 1 else (k, j)

    def o_map(g0, g1, k):
        i, j = ij(g0, g1)
        return (j // nbo, i, j % nbo) if out_shards > 1 else (i, j)

    a_block = (tk, tm) if form == "tn" else (tm, tk)
    if a_shards > 1:
        a_block = (None,) + a_block
    b_block = (tk, tn) if form in ("nn", "tn") else (tn, tk)
    if b_shards > 1:
        b_block = (None,) + b_block
    o_block = (None, tm, tn) if out_shards > 1 else (tm, tn)
    o_shape = (out_shards, M, n_o) if out_shards > 1 else (M, N)

    n_i, n_j = M // tm, N // tn
    grid = (n_i, n_j, nk) if order == "ij" else (n_j, n_i, nk)
    n_ride = len(rider.arrays) if rider else 0
    n_acc = 1 if nk > 1 else 0

    def body(*refs):
        a_ref, b_ref = refs[:2]
        ride_in = refs[2:2 + n_ride]
        o_ref = refs[2 + n_ride]
        ride_out = refs[3 + n_ride:3 + 2 * n_ride]
        scratch = refs[3 + 2 * n_ride:]
        ride_sems = scratch[n_acc:]
        pid = [pl.program_id(d) for d in range(3)]
        if rider:
            @pl.when((pid[0] == 0) & (pid[1] == 0) & (pid[2] == 0))
            def _():
                rider.start(ride_in, ride_out, ride_sems)

        part = lax.dot_general(a_ref[...].astype(BF16), b_ref[...].astype(BF16), dims, preferred_element_type=F32)
        if nk == 1:
            o_ref[...] = part.astype(o_ref.dtype)
        else:
            acc = scratch[0]
            k = pid[2]

            @pl.when(k == 0)
            def _():
                acc[...] = part

            @pl.when(k > 0)
            def _():
                acc[...] += part

            @pl.when(k == nk - 1)
            def _():
                o_ref[...] = acc[...].astype(o_ref.dtype)

        if rider:
            @pl.when((pid[0] == grid[0] - 1) & (pid[1] == grid[1] - 1) & (pid[2] == nk - 1))
            def _():
                rider.finish(ride_in, ride_out, ride_sems)

    main_shape = jax.ShapeDtypeStruct(o_shape, out_dtype)
    res = pl.pallas_call(
        body,
        name=name,
        out_shape=(main_shape, *rider.out_shapes) if rider else main_shape,
        grid=grid,
        in_specs=[pl.BlockSpec(a_block, a_map), pl.BlockSpec(b_block, b_map)] + [HBM_SPEC] * n_ride,
        out_specs=(pl.BlockSpec(o_block, o_map), *[HBM_SPEC] * n_ride) if rider else pl.BlockSpec(o_block, o_map),
        scratch_shapes=([pltpu.VMEM((tm, tn), F32)] if nk > 1 else []) + (rider.sems if rider else []),
        compiler_params=_params(*(("arbitrary",) * 3 if rider else ("parallel", "parallel", "arbitrary"))),
    )(a, b, *(rider.arrays if rider else ()))
    return (res[0], list(res[1:])) if rider else res


def _type_of(i, n_xt, n_tiles):
    return jnp.where(i >= n_xt, 1, 0) if n_tiles > n_xt else 0


def _mod_spec(D, k, n_xt, n_tiles):
    return pl.BlockSpec((None, 1, D), lambda i: (_type_of(i, n_xt, n_tiles) * N_MOD + k, 0, 0))


def _acc_spec(D, n_xt, n_tiles):
    return pl.BlockSpec((None, 1, D), lambda i: (_type_of(i, n_xt, n_tiles), 0, 0))


def _accumulate(ref, value, first):
    @pl.when(first)
    def _():
        ref[...] = value

    @pl.when(jnp.logical_not(first))
    def _():
        ref[...] += value


def _normmod(xp, mods, nw, *, k_shift, k_scale, n_xt, name, resid=None, rows=None):
    M, D = rows or xp.shape[0], xp.shape[1]
    n_tiles = M // ROW_TILE
    row = pl.BlockSpec((ROW_TILE, D), lambda i: (i, 0))
    vec = pl.BlockSpec((1, D), lambda i: (0, 0))
    mod = functools.partial(_mod_spec, D, n_xt=n_xt, n_tiles=n_tiles)

    if resid is not None:
        y_in, k_gate, weight = resid

        def body(xp_ref, y_ref, g_ref, nw_ref, sh_ref, sc_ref, x_ref, h_ref):
            g = g_ref[...] if weight is None else weight * g_ref[...]
            x = xp_ref[...] + g * y_ref[...]
            x_ref[...] = x
            r = lax.rsqrt(jnp.mean(x * x, axis=-1, keepdims=True) + EPS)
            y = (x * r) * nw_ref[...]
            h_ref[...] = (y * (1.0 + sc_ref[...]) + sh_ref[...]).astype(BF16)

        return pl.pallas_call(
            body, name=name, grid=(n_tiles,),
            out_shape=(jax.ShapeDtypeStruct((M, D), F32), jax.ShapeDtypeStruct((M, D), BF16)),
            in_specs=[row, row, mod(k_gate), vec, mod(k_shift), mod(k_scale)],
            out_specs=(row, row), compiler_params=_params("parallel"),
        )(xp, y_in, mods, nw, mods, mods)

    def body(xp_ref, nw_ref, sh_ref, sc_ref, h_ref):
        x = xp_ref[...]
        r = lax.rsqrt(jnp.mean(x * x, axis=-1, keepdims=True) + EPS)
        y = (x * r) * nw_ref[...]
        h_ref[...] = (y * (1.0 + sc_ref[...]) + sh_ref[...]).astype(BF16)

    return pl.pallas_call(
        body, name=name, grid=(n_tiles,), out_shape=jax.ShapeDtypeStruct((M, D), BF16),
        in_specs=[row, vec, mod(k_shift), mod(k_scale)], out_specs=row, compiler_params=_params("parallel"),
    )(xp, nw, mods, mods)


def _resid_bwd(dxo, y_in, mods, *, k_gate, weight, n_xt, name):
    M, D = dxo.shape
    n_tiles = M // ROW_TILE
    n_types = 2 if n_tiles > n_xt else 1
    row = pl.BlockSpec((ROW_TILE, D), lambda i: (i, 0))

    def body(dxo_ref, y_ref, g_ref, dy_ref, dg_ref):
        i = pl.program_id(0)
        dxo_t = dxo_ref[...]
        g = g_ref[...] if weight is None else weight * g_ref[...]
        yw = y_ref[...] if weight is None else weight * y_ref[...]
        dy_ref[...] = (dxo_t * g).astype(BF16)
        _accumulate(dg_ref, jnp.sum(dxo_t * yw, axis=0, keepdims=True), (i == 0) | (i == n_xt))

    return pl.pallas_call(
        body, name=name, grid=(n_tiles,),
        out_shape=(jax.ShapeDtypeStruct((M, D), BF16), jax.ShapeDtypeStruct((n_types, 1, D), F32)),
        in_specs=[row, row, _mod_spec(D, k_gate, n_xt, n_tiles)],
        out_specs=(row, _acc_spec(D, n_xt, n_tiles)), compiler_params=_params("arbitrary"),
    )(dxo, y_in, mods)


def _normmod_bwd(dh, x_in, dxo, mods, nw, *, k_scale, n_xt, name):
    M, D = dh.shape
    n_tiles = M // ROW_TILE
    n_types = 2 if n_tiles > n_xt else 1
    dxo_tiles = dxo.shape[0] // ROW_TILE
    row = pl.BlockSpec((ROW_TILE, D), lambda i: (i, 0))
    dxo_row = pl.BlockSpec((ROW_TILE, D), lambda i: (jnp.minimum(i, dxo_tiles - 1), 0))
    vec = pl.BlockSpec((1, D), lambda i: (0, 0))

    def body(dh_ref, x_ref, dxo_ref, nw_ref, sc_ref, dx_ref, dsh_ref, dsc_ref, dnw_ref):
        i = pl.program_id(0)
        x = x_ref[...]
        dh_t = dh_ref[...]
        w = nw_ref[...]
        r = lax.rsqrt(jnp.mean(x * x, axis=-1, keepdims=True) + EPS)
        n = x * r
        dy = dh_t * (1.0 + sc_ref[...])
        dn = dy * w
        dx = r * (dn - n * jnp.mean(dn * n, axis=-1, keepdims=True))
        if dxo_tiles < n_tiles:
            dx_ref[...] = dx + jnp.where(i < dxo_tiles, dxo_ref[...], 0.0)
        else:
            dx_ref[...] = dx + dxo_ref[...]
        first = (i == 0) | (i == n_xt)
        _accumulate(dsh_ref, jnp.sum(dh_t, axis=0, keepdims=True), first)
        _accumulate(dsc_ref, jnp.sum(dh_t * (n * w), axis=0, keepdims=True), first)
        _accumulate(dnw_ref, jnp.sum(dy * n, axis=0, keepdims=True), i == 0)

    acc = _acc_spec(D, n_xt, n_tiles)
    return pl.pallas_call(
        body, name=name, grid=(n_tiles,),
        out_shape=(jax.ShapeDtypeStruct((M, D), F32), jax.ShapeDtypeStruct((n_types, 1, D), F32),
                   jax.ShapeDtypeStruct((n_types, 1, D), F32), jax.ShapeDtypeStruct((1, D), F32)),
        in_specs=[row, row, dxo_row, vec, _mod_spec(D, k_scale, n_xt, n_tiles)],
        out_specs=(row, acc, acc, vec), compiler_params=_params("arbitrary"),
    )(dh, x_in, dxo, nw, mods)


def _swiglu_fwd(z, name):
    _, M, Fh = z.shape
    tf = _pick(Fh, (1408, 1024, 512, 256, 128))

    def body(z_ref, g_ref):
        a = z_ref[0]
        g_ref[...] = ((a * _sigmoid(a)) * z_ref[1]).astype(BF16)

    return pl.pallas_call(
        body, name=name, grid=(M // ROW_TILE, Fh // tf), out_shape=jax.ShapeDtypeStruct((M, Fh), BF16),
        in_specs=[pl.BlockSpec((2, ROW_TILE, tf), lambda i, j: (0, i, j))],
        out_specs=pl.BlockSpec((ROW_TILE, tf), lambda i, j: (i, j)), compiler_params=_params("parallel", "parallel"),
    )(z)


def _swiglu_bwd(z, dg, name):
    _, M, Fh = z.shape
    tf = _pick(Fh, (1408, 1024, 512, 256, 128))

    def body(z_ref, dg_ref, dz_ref):
        a = z_ref[0]
        sig = _sigmoid(a)
        dg_t = dg_ref[...]
        dz_ref[0] = (dg_t * z_ref[1] * (sig * (1.0 + a * (1.0 - sig)))).astype(BF16)
        dz_ref[1] = (dg_t * (a * sig)).astype(BF16)

    pair = pl.BlockSpec((2, ROW_TILE, tf), lambda i, j: (0, i, j))
    return pl.pallas_call(
        body, name=name, grid=(M // ROW_TILE, Fh // tf), out_shape=jax.ShapeDtypeStruct((2, M, Fh), BF16),
        in_specs=[pair, pl.BlockSpec((ROW_TILE, tf), lambda i, j: (i, j))],
        out_specs=pair, compiler_params=_params("parallel", "parallel"),
    )(z, dg)


def _final_loss(x3, target, fw, name):
    S, D = x3.shape
    row = pl.BlockSpec((ROW_TILE, D), lambda i: (i, 0))
    vec = pl.BlockSpec((1, D), lambda i: (0, 0))

    def body(x_ref, t_ref, fw_ref, dx_ref, loss_ref, dfw_ref):
        i = pl.program_id(0)
        x = x_ref[...]
        w = fw_ref[...]
        r = lax.rsqrt(jnp.mean(x * x, axis=-1, keepdims=True) + EPS)
        n = x * r
        err = n * w - t_ref[...]
        tile_loss = 0.5 * jnp.sum(jnp.mean(err * err, axis=-1, keepdims=True), axis=0, keepdims=True)
        dout = err / D
        dn = dout * w
        dx_ref[...] = r * (dn - n * jnp.mean(dn * n, axis=-1, keepdims=True))
        _accumulate(loss_ref, jnp.broadcast_to(tile_loss, (1, LANES)), i == 0)
        _accumulate(dfw_ref, jnp.sum(dout * n, axis=0, keepdims=True), i == 0)

    return pl.pallas_call(
        body, name=name, grid=(S // ROW_TILE,),
        out_shape=(jax.ShapeDtypeStruct((S, D), F32), jax.ShapeDtypeStruct((1, LANES), F32),
                   jax.ShapeDtypeStruct((1, D), F32)),
        in_specs=[row, row, vec],
        out_specs=(row, pl.BlockSpec((1, LANES), lambda i: (0, 0)), vec), compiler_params=_params("arbitrary"),
    )(x3, target, fw)


def _pair_swap(v):
    lane = lax.broadcasted_iota(jnp.int32, v.shape, v.ndim - 1)
    from_next = pltpu.roll(lane, 1, v.ndim - 1) == (lane ^ 1)
    return jnp.where(from_next, pltpu.roll(v, 1, v.ndim - 1), pltpu.roll(v, LANES - 1, v.ndim - 1))


def _qk_prep(zx, cos2, sin2, qw, kw, *, q_w, kv_w, scale, name):
    M = zx.shape[0]
    nqh, nkh = q_w // HEAD_DIM, kv_w // HEAD_DIM
    row = lambda w, c: pl.BlockSpec((ROW_TILE, w), lambda i: (i, c))
    vec = pl.BlockSpec((1, HEAD_DIM), lambda i: (0, 0))

    def rot(z, w, cos_t, sin_t):
        r = lax.rsqrt(jnp.mean(z * z, axis=-1, keepdims=True) + EPS)
        y = (z * r) * w
        return y * cos_t + _pair_swap(y) * sin_t

    def body(zq_ref, zk_ref, zv_ref, cos_ref, sin_ref, qw_ref, kw_ref, q_ref, k_ref, v_ref):
        cos_t, sin_t = cos_ref[...], sin_ref[...]
        for h in range(nqh):
            hs = slice(h * HEAD_DIM, (h + 1) * HEAD_DIM)
            q_ref[:, hs] = (rot(zq_ref[:, hs], qw_ref[...], cos_t, sin_t) * scale).astype(BF16)
        for h in range(nkh):
            hs = slice(h * HEAD_DIM, (h + 1) * HEAD_DIM)
            k_ref[:, hs] = rot(zk_ref[:, hs], kw_ref[...], cos_t, sin_t).astype(BF16)
        v_ref[...] = zv_ref[...].astype(BF16)

    return pl.pallas_call(
        body, name=name, grid=(M // ROW_TILE,),
        out_shape=(jax.ShapeDtypeStruct((M, q_w), BF16), jax.ShapeDtypeStruct((M, kv_w), BF16),
                   jax.ShapeDtypeStruct((M, kv_w), BF16)),
        in_specs=[row(q_w, 0), row(kv_w, q_w // kv_w), row(kv_w, q_w // kv_w + 1), row(HEAD_DIM, 0), row(HEAD_DIM, 0),
                  vec, vec],
        out_specs=(row(q_w, 0), row(kv_w, 0), row(kv_w, 0)), compiler_params=_params("parallel"),
    )(zx, zx, zx, cos2, sin2, qw, kw)


def _qk_prep_bwd(zx, dq, dk, dv, cos2, sin2, qw, kw, *, q_w, kv_w, scale, n_xt, name):
    M = zx.shape[0]
    nqh, nkh = q_w // HEAD_DIM, kv_w // HEAD_DIM
    dq_tiles = dq.shape[0] // ROW_TILE
    row = lambda w, c: pl.BlockSpec((ROW_TILE, w), lambda i: (i, c))
    vec = pl.BlockSpec((1, HEAD_DIM), lambda i: (0, 0))

    def unrot(z, d, w, cos_t, sin_t):
        r = lax.rsqrt(jnp.mean(z * z, axis=-1, keepdims=True) + EPS)
        n = z * r
        dy = d * cos_t - _pair_swap(d) * sin_t
        dn = dy * w
        dz = r * (dn - n * jnp.mean(dn * n, axis=-1, keepdims=True))
        return dz, jnp.sum(dy * n, axis=0, keepdims=True)

    def body(zq_ref, zk_ref, dq_ref, dk_ref, dv_ref, cos_ref, sin_ref, qw_ref, kw_ref, dz_ref, dqw_ref, dkw_ref):
        i = pl.program_id(0)
        cos_t, sin_t = cos_ref[...], sin_ref[...]
        is_x = i < n_xt
        dqw = jnp.zeros((1, HEAD_DIM), F32)
        dkw = jnp.zeros((1, HEAD_DIM), F32)
        for h in range(nqh):
            hs = slice(h * HEAD_DIM, (h + 1) * HEAD_DIM)
            d = jnp.where(is_x, dq_ref[:, hs], 0.0) * scale
            dz, dw = unrot(zq_ref[:, hs], d, qw_ref[...], cos_t, sin_t)
            dz_ref[:, hs] = dz.astype(BF16)
            dqw = dqw + dw
        for h in range(nkh):
            hs = slice(h * HEAD_DIM, (h + 1) * HEAD_DIM)
            dz, dw = unrot(zk_ref[:, hs], dk_ref[:, hs], kw_ref[...], cos_t, sin_t)
            dz_ref[:, q_w + h * HEAD_DIM:q_w + (h + 1) * HEAD_DIM] = dz.astype(BF16)
            dkw = dkw + dw
        dz_ref[:, q_w + kv_w:] = dv_ref[...].astype(BF16)
        _accumulate(dqw_ref, dqw, i == 0)
        _accumulate(dkw_ref, dkw, i == 0)

    return pl.pallas_call(
        body, name=name, grid=(M // ROW_TILE,),
        out_shape=(jax.ShapeDtypeStruct((M, q_w + 2 * kv_w), BF16), jax.ShapeDtypeStruct((1, HEAD_DIM), F32),
                   jax.ShapeDtypeStruct((1, HEAD_DIM), F32)),
        in_specs=[row(q_w, 0), row(kv_w, q_w // kv_w),
                  pl.BlockSpec((ROW_TILE, q_w), lambda i: (jnp.minimum(i, dq_tiles - 1), 0)),
                  row(kv_w, 0), row(kv_w, 0), row(HEAD_DIM, 0), row(HEAD_DIM, 0), vec, vec],
        out_specs=(row(q_w + 2 * kv_w, 0), vec, vec), compiler_params=_params("arbitrary"),
    )(zx, zx, dq, dk, dv, cos2, sin2, qw, kw)


def _lane_pick(tile, h):
    lane = lax.broadcasted_iota(jnp.int32, tile.shape, 1)
    return jnp.sum(jnp.where(lane == h, tile, 0.0), axis=-1, keepdims=True)


def _flash_fwd(q, k, v, *, S, name):
    NK, kv_w = k.shape
    G = kv_w // HEAD_DIM
    qpk = q.shape[1] // kv_w
    gw = qpk * HEAD_DIM
    tq = _pick(S, (512, 256, 128))
    tk = _pick(NK, KEY_TILES)
    nk = NK // tk

    def body(q_ref, k_ref, v_ref, o_ref, lse_ref, m_s, l_s, acc_s):
        ki = pl.program_id(2)

        @pl.when(ki == 0)
        def _():
            m_s[...] = jnp.full(m_s.shape, -1e30, F32)
            l_s[...] = jnp.zeros(l_s.shape, F32)
            acc_s[...] = jnp.zeros(acc_s.shape, F32)

        k_t, v_t = k_ref[...], v_ref[...]
        for h in range(qpk):
            s = lax.dot_general(q_ref[:, h * HEAD_DIM:(h + 1) * HEAD_DIM], k_t, NT, preferred_element_type=F32)
            m_prev = m_s[h]
            m_new = jnp.maximum(m_prev, jnp.max(s, axis=-1, keepdims=True))
            alpha = jnp.exp(m_prev - m_new)
            p = jnp.exp(s - m_new)
            l_s[h] = alpha * l_s[h] + jnp.sum(p, axis=-1, keepdims=True)
            acc_s[h] = alpha * acc_s[h] + lax.dot_general(p.astype(BF16), v_t, NN, preferred_element_type=F32)
            m_s[h] = m_new

        @pl.when(ki == nk - 1)
        def _():
            lane = lax.broadcasted_iota(jnp.int32, (tq, LANES), 1)
            lse = jnp.zeros((tq, LANES), F32)
            for h in range(qpk):
                l = l_s[h]
                o_ref[:, h * HEAD_DIM:(h + 1) * HEAD_DIM] = acc_s[h] / l
                lse = jnp.where(lane == h, m_s[h] + jnp.log(l), lse)
            lse_ref[...] = lse

    return pl.pallas_call(
        body, name=name, grid=(G, S // tq, nk),
        out_shape=(jax.ShapeDtypeStruct((S, G * gw), F32), jax.ShapeDtypeStruct((G, S, LANES), F32)),
        in_specs=[pl.BlockSpec((tq, gw), lambda g, i, j: (i, g)),
                  pl.BlockSpec((tk, HEAD_DIM), lambda g, i, j: (j, g)),
                  pl.BlockSpec((tk, HEAD_DIM), lambda g, i, j: (j, g))],
        out_specs=(pl.BlockSpec((tq, gw), lambda g, i, j: (i, g)),
                   pl.BlockSpec((None, tq, LANES), lambda g, i, j: (g, i, 0))),
        scratch_shapes=[pltpu.VMEM((qpk, tq, 1), F32), pltpu.VMEM((qpk, tq, 1), F32),
                        pltpu.VMEM((qpk, tq, HEAD_DIM), F32)],
        compiler_params=_params("parallel", "parallel", "arbitrary"),
    )(q, k, v)


def _attn_delta(do, o, *, G, name):
    S, q_w = o.shape
    gw = q_w // G
    qpk = gw // HEAD_DIM
    tq = _pick(S, (512, 256, 128))

    def body(do_ref, o_ref, d_ref):
        lane = lax.broadcasted_iota(jnp.int32, (tq, LANES), 1)
        out = jnp.zeros((tq, LANES), F32)
        for h in range(qpk):
            hs = slice(h * HEAD_DIM, (h + 1) * HEAD_DIM)
            out = jnp.where(lane == h, jnp.sum(do_ref[:, hs] * o_ref[:, hs], axis=-1, keepdims=True), out)
        d_ref[...] = out

    return pl.pallas_call(
        body, name=name, grid=(G, S // tq), out_shape=jax.ShapeDtypeStruct((G, S, LANES), F32),
        in_specs=[pl.BlockSpec((tq, gw), lambda g, i: (i, g)), pl.BlockSpec((tq, gw), lambda g, i: (i, g))],
        out_specs=pl.BlockSpec((None, tq, LANES), lambda g, i: (g, i, 0)),
        compiler_params=_params("parallel", "parallel"),
    )(do, o)


def _flash_bwd(q, k, v, do, lse, delta, *, S, name):
    NK, kv_w = k.shape
    G = kv_w // HEAD_DIM
    qpk = q.shape[1] // kv_w
    gw = qpk * HEAD_DIM
    tq = _pick(S, (512, 256, 128))
    tk = _pick(NK, KEY_TILES)

    def body(q_ref, k_ref, v_ref, do_ref, lse_ref, dl_ref, dq_ref, dk_ref, dv_ref, lse_s, dl_s):
        qi, ki = pl.program_id(1), pl.program_id(2)

        @pl.when((qi == 0) & (ki == 0))
        def _():
            dk_ref[...] = jnp.zeros(dk_ref.shape, F32)
            dv_ref[...] = jnp.zeros(dv_ref.shape, F32)

        @pl.when(ki == 0)
        def _():
            dq_ref[...] = jnp.zeros(dq_ref.shape, F32)
            for h in range(qpk):
                lse_s[h] = _lane_pick(lse_ref[...], h)
                dl_s[h] = _lane_pick(dl_ref[...], h)

        k_t, v_t = k_ref[...], v_ref[...]
        dk_acc = jnp.zeros((tk, HEAD_DIM), F32)
        dv_acc = jnp.zeros((tk, HEAD_DIM), F32)
        for h in range(qpk):
            hs = slice(h * HEAD_DIM, (h + 1) * HEAD_DIM)
            q_h = q_ref[:, hs]
            do_h = do_ref[:, hs].astype(BF16)
            s = lax.dot_general(q_h, k_t, NT, preferred_element_type=F32)
            p = jnp.exp(s - lse_s[h])
            dv_acc = dv_acc + lax.dot_general(p.astype(BF16), do_h, TN, preferred_element_type=F32)
            dp = lax.dot_general(do_h, v_t, NT, preferred_element_type=F32)
            ds = (p * (dp - dl_s[h])).astype(BF16)
            dq_ref[:, hs] += lax.dot_general(ds, k_t, NN, preferred_element_type=F32)
            dk_acc = dk_acc + lax.dot_general(ds, q_h, TN, preferred_element_type=F32)
        rows = pl.ds(pl.multiple_of(ki * tk, tk), tk)
        dk_ref[rows, :] += dk_acc
        dv_ref[rows, :] += dv_acc

    qspec = pl.BlockSpec((tq, gw), lambda g, i, j: (i, g))
    kspec = pl.BlockSpec((tk, HEAD_DIM), lambda g, i, j: (j, g))
    lspec = pl.BlockSpec((None, tq, LANES), lambda g, i, j: (g, i, 0))
    group = pl.BlockSpec((NK, HEAD_DIM), lambda g, i, j: (0, g))
    kv_shape = jax.ShapeDtypeStruct((NK, kv_w), F32)
    return pl.pallas_call(
        body, name=name, grid=(G, S // tq, NK // tk),
        out_shape=(jax.ShapeDtypeStruct((S, G * gw), F32), kv_shape, kv_shape),
        in_specs=[qspec, kspec, kspec, qspec, lspec, lspec], out_specs=(qspec, group, group),
        scratch_shapes=[pltpu.VMEM((qpk, tq, 1), F32), pltpu.VMEM((qpk, tq, 1), F32)],
        compiler_params=_params("arbitrary", "arbitrary", "arbitrary"),
    )(q, k, v, do, lse, delta)


def _wide_specs(col0, width, bw):
    return [pl.BlockSpec((ROW_TILE, bw), functools.partial(lambda i, c: (i, c), c=col0 // bw + p))
            for p in range(width // bw)]


def _cat(refs):
    return refs[0][...] if len(refs) == 1 else jnp.concatenate([r[...] for r in refs], axis=1)


def _gelu(x):
    return 0.5 * x * (1.0 + lax.erf(x * (1.0 / math.sqrt(2.0))))


def _gelu_grad(x):
    return 0.5 * (1.0 + lax.erf(x * (1.0 / math.sqrt(2.0)))) + x * jnp.exp(-0.5 * x * x) * (1.0 / math.sqrt(2.0 * math.pi))


def _layernorm_stats(v):
    mu = jnp.mean(v, axis=-1, keepdims=True)
    xc = v - mu
    rstd = lax.rsqrt(jnp.mean(xc * xc, axis=-1, keepdims=True) + EPS)
    return xc * rstd, rstd


def _gmlp_fwd(zx, ln_w, ln_b, w_s, b_sb, *, S, col0, name):
    G, W = w_s.shape[0], ln_w.shape[1]
    gd = W // G
    bw = math.gcd(col0, W)
    n_parts = W // bw
    vec = pl.BlockSpec((1, W), lambda i: (0, 0))
    full3 = pl.BlockSpec((G, CHUNK, CHUNK), lambda i: (0, 0, 0))
    full3b = pl.BlockSpec((G, CHUNK, gd), lambda i: (0, 0, 0))

    def body(*refs):
        u_refs, v_refs = refs[:n_parts], refs[n_parts:2 * n_parts]
        lnw_ref, lnb_ref, ws_ref, bs_ref, gm_ref = refs[2 * n_parts:]
        u = _gelu(_cat(u_refs))
        vhat, _ = _layernorm_stats(_gelu(_cat(v_refs)))
        vn = (vhat * lnw_ref[...] + lnb_ref[...]).astype(BF16)
        for c in range(ROW_TILE // CHUNK):
            rs = slice(c * CHUNK, (c + 1) * CHUNK)
            for g in range(G):
                cs = slice(g * gd, (g + 1) * gd)
                mixed = lax.dot_general(ws_ref[g].astype(BF16), vn[rs, cs], NN, preferred_element_type=F32) + bs_ref[g]
                gm_ref[rs, cs] = (u[rs, cs] * mixed).astype(BF16)

    return pl.pallas_call(
        body, name=name, grid=(S // ROW_TILE,), out_shape=jax.ShapeDtypeStruct((S, W), BF16),
        in_specs=_wide_specs(col0, W, bw) + _wide_specs(col0 + W, W, bw) + [vec, vec, full3, full3b],
        out_specs=pl.BlockSpec((ROW_TILE, W), lambda i: (i, 0)), compiler_params=_params("parallel"),
    )(*([zx] * (2 * n_parts)), ln_w, ln_b, w_s, b_sb)


def _gmlp_bwd(zx, dgm, ln_w, ln_b, w_s, b_sb, *, S, col0, name):
    G, W = w_s.shape[0], ln_w.shape[1]
    gd = W // G
    bw = math.gcd(col0, W)
    n_parts = W // bw
    vec = pl.BlockSpec((1, W), lambda i: (0, 0))
    full3 = pl.BlockSpec((G, CHUNK, CHUNK), lambda i: (0, 0, 0))
    full3b = pl.BlockSpec((G, CHUNK, gd), lambda i: (0, 0, 0))
    row = pl.BlockSpec((ROW_TILE, W), lambda i: (i, 0))

    def body(*refs):
        u_refs, v_refs = refs[:n_parts], refs[n_parts:2 * n_parts]
        dgm_ref, lnw_ref, lnb_ref, ws_ref, bs_ref, dz_ref, dws_ref, dbs_ref, dlnw_ref, dlnb_ref, du_s, dvn_s = refs[2 * n_parts:]
        i = pl.program_id(0)
        upre, vpre = _cat(u_refs), _cat(v_refs)
        u = _gelu(upre)
        vhat, rstd = _layernorm_stats(_gelu(vpre))
        lnw = lnw_ref[...]
        vn = (vhat * lnw + lnb_ref[...]).astype(BF16)
        dgm_t = dgm_ref[...]

        @pl.when(i == 0)
        def _():
            dws_ref[...] = jnp.zeros(dws_ref.shape, F32)
            dbs_ref[...] = jnp.zeros(dbs_ref.shape, F32)

        for c in range(ROW_TILE // CHUNK):
            rs = slice(c * CHUNK, (c + 1) * CHUNK)
            for g in range(G):
                cs = slice(g * gd, (g + 1) * gd)
                ws_g = ws_ref[g].astype(BF16)
                vn_cg = vn[rs, cs]
                mixed = lax.dot_general(ws_g, vn_cg, NN, preferred_element_type=F32) + bs_ref[g]
                dgm_cg = dgm_t[rs, cs]
                du_s[rs, cs] = dgm_cg * mixed
                dmixed = dgm_cg * u[rs, cs]
                dmixed_b = dmixed.astype(BF16)
                dws_ref[g] += lax.dot_general(dmixed_b, vn_cg, NT, preferred_element_type=F32)
                dbs_ref[g] += dmixed
                dvn_s[rs, cs] = lax.dot_general(ws_g, dmixed_b, TN, preferred_element_type=F32)

        dvn = dvn_s[...]
        _accumulate(dlnw_ref, jnp.sum(dvn * vhat, axis=0, keepdims=True), i == 0)
        _accumulate(dlnb_ref, jnp.sum(dvn, axis=0, keepdims=True), i == 0)
        dvhat = dvn * lnw
        dv = rstd * (dvhat - jnp.mean(dvhat, axis=-1, keepdims=True)
                     - vhat * jnp.mean(dvhat * vhat, axis=-1, keepdims=True))
        dz_ref[:, :W] = (du_s[...] * _gelu_grad(upre)).astype(BF16)
        dz_ref[:, W:] = (dv * _gelu_grad(vpre)).astype(BF16)

    return pl.pallas_call(
        body, name=name, grid=(S // ROW_TILE,),
        out_shape=(jax.ShapeDtypeStruct((S, 2 * W), BF16), jax.ShapeDtypeStruct((G, CHUNK, CHUNK), F32),
                   jax.ShapeDtypeStruct((G, CHUNK, gd), F32), jax.ShapeDtypeStruct((1, W), F32),
                   jax.ShapeDtypeStruct((1, W), F32)),
        in_specs=_wide_specs(col0, W, bw) + _wide_specs(col0 + W, W, bw) + [row, vec, vec, full3, full3b],
        out_specs=(pl.BlockSpec((ROW_TILE, 2 * W), lambda i: (i, 0)), full3, full3b, vec, vec),
        scratch_shapes=[pltpu.VMEM((ROW_TILE, W), F32), pltpu.VMEM((ROW_TILE, W), F32)],
        compiler_params=_params("arbitrary"),
    )(*([zx] * (2 * n_parts)), dgm, ln_w, ln_b, w_s, b_sb)


def _merge_fwd(zx, a_br, g_br, b_gate, *, S, col0, name):
    D = a_br.shape[1]
    cw = min(math.gcd(col0, D), 1024)
    nc = D // cw
    c0 = col0 // cw
    blk = lambda f: pl.BlockSpec((ROW_TILE, cw), f)
    bias = lambda t: pl.BlockSpec((None, 1, cw), lambda i, j: (t, 0, j))

    def body(l0_ref, l1_ref, a_ref, g_ref, b0_ref, b1_ref, t_ref):
        g0 = _sigmoid(l0_ref[...] + b0_ref[...])
        g1 = _sigmoid(l1_ref[...] + b1_ref[...])
        t_ref[...] = (g0 * a_ref[...] + g1 * g_ref[...]).astype(BF16)

    return pl.pallas_call(
        body, name=name, grid=(S // ROW_TILE, nc), out_shape=jax.ShapeDtypeStruct((S, D), BF16),
        in_specs=[blk(lambda i, j: (i, c0 + j)), blk(lambda i, j: (i, c0 + nc + j)), blk(lambda i, j: (i, j)),
                  blk(lambda i, j: (i, j)), bias(0), bias(1)],
        out_specs=blk(lambda i, j: (i, j)), compiler_params=_params("parallel", "parallel"),
    )(zx, zx, a_br, g_br, b_gate, b_gate)


def _merge_bwd(dt, zx, a_br, g_br, b_gate, *, S, col0, name):
    D = a_br.shape[1]
    cw = min(math.gcd(col0, D), 1024)
    nc = D // cw
    c0 = col0 // cw
    blk = lambda f: pl.BlockSpec((ROW_TILE, cw), f)
    bias = lambda t: pl.BlockSpec((None, 1, cw), lambda j, i: (t, 0, j))
    own = blk(lambda j, i: (i, j))
    acc = pl.BlockSpec((1, cw), lambda j, i: (0, j))

    def body(dt_ref, l0_ref, l1_ref, a_ref, g_ref, b0_ref, b1_ref, da_ref, dg_ref, dl0_ref, dl1_ref, db0_ref, db1_ref):
        i = pl.program_id(1)
        dt_t = dt_ref[...]
        g0 = _sigmoid(l0_ref[...] + b0_ref[...])
        g1 = _sigmoid(l1_ref[...] + b1_ref[...])
        da_ref[...] = (dt_t * g0).astype(BF16)
        dg_ref[...] = (dt_t * g1).astype(BF16)
        dl0 = dt_t * a_ref[...] * (g0 * (1.0 - g0))
        dl1 = dt_t * g_ref[...] * (g1 * (1.0 - g1))
        dl0_ref[...] = dl0.astype(BF16)
        dl1_ref[...] = dl1.astype(BF16)
        _accumulate(db0_ref, jnp.sum(dl0, axis=0, keepdims=True), i == 0)
        _accumulate(db1_ref, jnp.sum(dl1, axis=0, keepdims=True), i == 0)

    sd = lambda dt_: jax.ShapeDtypeStruct((S, D), dt_)
    return pl.pallas_call(
        body, name=name, grid=(nc, S // ROW_TILE),
        out_shape=(sd(BF16), sd(BF16), sd(BF16), sd(BF16), jax.ShapeDtypeStruct((1, D), F32),
                   jax.ShapeDtypeStruct((1, D), F32)),
        in_specs=[own, blk(lambda j, i: (i, c0 + j)), blk(lambda j, i: (i, c0 + nc + j)), own, own, bias(0), bias(1)],
        out_specs=(own, own, own, own, acc, acc), compiler_params=_params("parallel", "arbitrary"),
    )(dt, zx, zx, a_br, g_br, b_gate, b_gate)


def _mod_fwd(cvecs, w_mod, name):
    R, D = cvecs.shape
    nsh = w_mod.shape[1]
    tn = _pick(nsh, (512, 256, 128))

    def body(c_ref, w_ref, o_ref):
        cv = c_ref[...]
        a = (cv * _sigmoid(cv)).astype(BF16)
        o_ref[...] = lax.dot_general(a, w_ref[...].astype(BF16), NN, preferred_element_type=F32)

    return pl.pallas_call(
        body, name=name, grid=(nsh // tn,), out_shape=jax.ShapeDtypeStruct((R, nsh), F32),
        in_specs=[pl.BlockSpec((R, D), lambda j: (0, 0)), pl.BlockSpec((D, tn), lambda j: (0, j))],
        out_specs=pl.BlockSpec((R, tn), lambda j: (0, j)), compiler_params=_params("parallel"),
    )(cvecs, w_mod)


def _mod_wgrad(cvecs, dm, name):
    R, D = cvecs.shape
    nsh = dm.shape[1]
    tn = _pick(nsh, (512, 256, 128))

    def body(c_ref, dm_ref, o_ref):
        cv = c_ref[...]
        a = (cv * _sigmoid(cv)).astype(BF16)
        o_ref[...] = lax.dot_general(a, dm_ref[...].astype(BF16), TN, preferred_element_type=F32)

    return pl.pallas_call(
        body, name=name, grid=(nsh // tn,), out_shape=jax.ShapeDtypeStruct((D, nsh), F32),
        in_specs=[pl.BlockSpec((R, D), lambda j: (0, 0)), pl.BlockSpec((R, tn), lambda j: (0, j))],
        out_specs=pl.BlockSpec((D, tn), lambda j: (0, j)), compiler_params=_params("parallel"),
    )(cvecs, dm)


def _mod_dgrad(dm, w_mod, name):
    R, nsh = dm.shape
    D = w_mod.shape[0]
    tn = _pick(D, (256, 128))

    def body(dm_ref, w_ref, o_ref):
        o_ref[...] = lax.dot_general(dm_ref[...].astype(BF16), w_ref[...].astype(BF16), NT, preferred_element_type=F32)

    return pl.pallas_call(
        body, name=name, grid=(D // tn,), out_shape=jax.ShapeDtypeStruct((R, D), F32),
        in_specs=[pl.BlockSpec((R, nsh), lambda j: (0, 0)), pl.BlockSpec((tn, nsh), lambda j: (j, 0))],
        out_specs=pl.BlockSpec((R, tn), lambda j: (0, j)), compiler_params=_params("parallel"),
    )(dm, w_mod)


def _adam_rows(R, C):
    return _pick(R, [t for t in (512, 256, 128, 64, 32, 16) if t * C * 4 <= ADAM_TILE_BYTES] or [16])


def _adamw(w, m, v, grads, name):
    R, C = w.shape
    tr = _adam_rows(R, C)
    n_g = len(grads)
    blk = pl.BlockSpec((tr, C), lambda i: (i, 0))
    c1 = 1.0 - ADAM_B1 ** ADAM_STEP
    c2 = 1.0 - ADAM_B2 ** ADAM_STEP

    def body(*refs):
        w_ref, m_ref, v_ref = refs[:3]
        g_refs = refs[3:3 + n_g]
        g_ref, d_ref, m2_ref, v2_ref = refs[3 + n_g:]
        g = g_refs[0][...].astype(F32)
        for r in g_refs[1:]:
            g = g + r[...].astype(F32)
        m2 = ADAM_B1 * m_ref[...] + (1.0 - ADAM_B1) * g
        v2 = ADAM_B2 * v_ref[...] + (1.0 - ADAM_B2) * (g * g)
        g_ref[...] = g
        m2_ref[...] = m2
        v2_ref[...] = v2
        d_ref[...] = -ADAM_LR * ((m2 / c1) / (jnp.sqrt(v2 / c2) + ADAM_EPS) + ADAM_WD * w_ref[...])

    out = jax.ShapeDtypeStruct((R, C), F32)
    return pl.pallas_call(
        body, name=name, grid=(R // tr,), out_shape=(out, out, out, out),
        in_specs=[blk] * (3 + n_g), out_specs=(blk, blk, blk, blk), compiler_params=_params("parallel"),
    )(w, m, v, *grads)


def _sum_slabs(slabs, name):
    n, R, C = slabs.shape
    tr = _adam_rows(R, C)

    def body(s_ref, o_ref):
        acc = s_ref[0].astype(F32)
        for k in range(1, n):
            acc = acc + s_ref[k].astype(F32)
        o_ref[...] = acc.astype(BF16)

    return pl.pallas_call(
        body, name=name, grid=(R // tr,), out_shape=jax.ShapeDtypeStruct((R, C), BF16),
        in_specs=[pl.BlockSpec((n, tr, C), lambda i: (0, i, 0))],
        out_specs=pl.BlockSpec((tr, C), lambda i: (i, 0)), compiler_params=_params("parallel"),
    )(slabs)


def _plane_peers():
    x, y = lax.axis_index("x"), lax.axis_index("y")
    return [(1 - x, y), (x, 1 - y), (1 - x, 1 - y)]


class _Rider:
    def __init__(self, arrays, out_shapes, copies):
        n = len(arrays)
        self.arrays, self.out_shapes = list(arrays), list(out_shapes)
        self.sems = [pltpu.SemaphoreType.DMA((3 * n,)), pltpu.SemaphoreType.DMA((3 * n,)), pltpu.SemaphoreType.DMA((n,))]
        self._copies = copies

    def start(self, ins, outs, sems):
        local, sends, _ = self._copies(ins, outs, sems)
        for cp in local + sends:
            cp.start()

    def finish(self, ins, outs, sems):
        local, sends, recvs = self._copies(ins, outs, sems)
        for cp in recvs:
            cp.wait_recv()
        for cp in sends:
            cp.wait_send()
        for cp in local:
            cp.wait()


def _gather_rider(shards):
    n = len(shards)

    def copies(ins, outs, sems):
        send_sems, recv_sems, local_sems = sems
        x, y, c = lax.axis_index("x"), lax.axis_index("y"), lax.axis_index("c")
        me = 2 * x + y
        peers = _plane_peers()

        def remote(w, k, slab):
            px, py = peers[k]
            return pltpu.make_async_remote_copy(
                src_ref=ins[w], dst_ref=outs[w].at[slab], send_sem=send_sems.at[3 * w + k],
                recv_sem=recv_sems.at[3 * w + k], device_id=(px, py, c), device_id_type=MESH)

        local = [pltpu.make_async_copy(ins[w], outs[w].at[me], local_sems.at[w]) for w in range(n)]
        sends = [remote(w, k, me) for w in range(n) for k in range(3)]
        recvs = [remote(w, k, 2 * px + py) for w in range(n) for k, (px, py) in enumerate(peers)]
        return local, sends, recvs

    return _Rider(shards, [jax.ShapeDtypeStruct((N_SHARDS,) + s.shape, s.dtype) for s in shards], copies)


def _scatter_rider(fulls):
    n = len(fulls)

    def copies(ins, outs, sems):
        send_sems, recv_sems, local_sems = sems
        x, y, c = lax.axis_index("x"), lax.axis_index("y"), lax.axis_index("c")
        me = 2 * x + y
        peers = _plane_peers()

        def remote(w, k):
            px, py = peers[k]
            return pltpu.make_async_remote_copy(
                src_ref=ins[w].at[2 * px + py], dst_ref=outs[w].at[k], send_sem=send_sems.at[3 * w + k],
                recv_sem=recv_sems.at[3 * w + k], device_id=(px, py, c), device_id_type=MESH)

        local = [pltpu.make_async_copy(ins[w].at[me], outs[w].at[3], local_sems.at[w]) for w in range(n)]
        sends = [remote(w, k) for w in range(n) for k in range(3)]
        return local, sends, sends

    return _Rider(fulls, [jax.ShapeDtypeStruct(f.shape, f.dtype) for f in fulls], copies)


def _comm_call(rider, name):
    n = len(rider.arrays)

    def body(*refs):
        ins, outs, sems = refs[:n], refs[n:2 * n], refs[2 * n:]
        rider.start(ins, outs, sems)
        rider.finish(ins, outs, sems)

    return list(pl.pallas_call(
        body, name=name, out_shape=tuple(rider.out_shapes), in_specs=[HBM_SPEC] * n, out_specs=tuple([HBM_SPEC] * n),
        scratch_shapes=rider.sems,
    )(*rider.arrays))


def _sibling_exchange(blocks, name):
    n = len(blocks)

    def body(*refs):
        ins, outs = refs[:n], refs[n:2 * n]
        send_sems, recv_sems = refs[2 * n:]
        sibling = (lax.axis_index("x"), lax.axis_index("y"), 1 - lax.axis_index("c"))
        sends = [pltpu.make_async_remote_copy(src_ref=ins[w], dst_ref=outs[w], send_sem=send_sems.at[w],
                                              recv_sem=recv_sems.at[w], device_id=sibling, device_id_type=MESH)
                 for w in range(n)]
        for cp in sends:
            cp.start()
        for cp in sends:
            cp.wait_recv()
        for cp in sends:
            cp.wait_send()

    return pl.pallas_call(
        body, name=name, out_shape=tuple(jax.ShapeDtypeStruct(b.shape, b.dtype) for b in blocks),
        in_specs=[HBM_SPEC] * n, out_specs=tuple([HBM_SPEC] * n),
        scratch_shapes=[pltpu.SemaphoreType.DMA((n,)), pltpu.SemaphoreType.DMA((n,))],
    )(*blocks)


def _allgather_devices(block, name):
    m_per, n_cols = block.shape

    def body(x_ref, out_ref, send_sems, recv_sems, local_sem):
        x, y, c = lax.axis_index("x"), lax.axis_index("y"), lax.axis_index("c")
        me, sibling = (x, y, c), (x, y, 1 - c)
        chips = _plane_peers()

        def rows(px, py, pc):
            return out_ref.at[pl.ds((4 * px + 2 * py + pc) * m_per, m_per), :]

        def copy(k, blk, to, src=None):
            return pltpu.make_async_remote_copy(
                src_ref=rows(*blk) if src is None else src, dst_ref=rows(*blk), send_sem=send_sems.at[k],
                recv_sem=recv_sems.at[k], device_id=to, device_id_type=MESH)

        mine = pltpu.make_async_copy(x_ref, rows(*me), local_sem)
        mine.start()
        first = [copy(0, me, sibling, src=x_ref)]
        first += [copy(1 + j, me, (*chip, c), src=x_ref) for j, chip in enumerate(chips)]
        for cp in first:
            cp.start()
        passed = [copy(4 + j, (*chip, c), sibling) for j, chip in enumerate(chips)]
        for j, chip in enumerate(chips):
            copy(1 + j, (*chip, c), me).wait_recv()
            passed[j].start()
        copy(0, sibling, me).wait_recv()
        for j, chip in enumerate(chips):
            copy(4 + j, (*chip, 1 - c), me).wait_recv()
        for cp in first + passed:
            cp.wait_send()
        mine.wait()

    out = pl.pallas_call(
        body, name=name, out_shape=jax.ShapeDtypeStruct((N_DEV * m_per, n_cols), block.dtype),
        in_specs=[VMEM_SPEC], out_specs=VMEM_SPEC,
        scratch_shapes=[pltpu.SemaphoreType.DMA((7,)), pltpu.SemaphoreType.DMA((7,)), pltpu.SemaphoreType.DMA],
        compiler_params=pltpu.CompilerParams(vmem_limit_bytes=VMEM_LIMIT_BYTES),
    )(block)
    return out.reshape(N_DEV, m_per, n_cols)


PACK_ROWS = 16


def _pack_rows(shape, width):
    return -(-math.prod(shape) // (width * PACK_ROWS)) * PACK_ROWS


def _pack(arrays, width):
    rows = []
    for a in arrays:
        flat = a.reshape(-1).astype(F32)
        n_rows = _pack_rows(a.shape, width)
        rows.append(jnp.pad(flat, (0, n_rows * width - flat.shape[0])).reshape(n_rows, width))
    return jnp.concatenate(rows, axis=0)


def _unpack(packed, shapes, width):
    out, r = [], 0
    for shp in shapes:
        size = math.prod(shp)
        n_rows = _pack_rows(shp, width)
        out.append(packed[r:r + n_rows].reshape(-1)[:size].reshape(shp))
        r += n_rows
    return out


def _rope_tables(S, C):
    rows = S // GRID_W
    axis_dim = HEAD_DIM // 2
    row = jnp.broadcast_to(jnp.arange(rows, dtype=F32)[:, None], (rows, GRID_W)).reshape(-1)
    col = jnp.broadcast_to(jnp.arange(GRID_W, dtype=F32)[None, :], (rows, GRID_W)).reshape(-1)
    inv_freq = ROPE_THETA ** (-jnp.arange(0, axis_dim, 2, dtype=F32) / axis_dim)
    ang = jnp.concatenate([row[:, None] * inv_freq, col[:, None] * inv_freq], axis=-1)
    cos, sin = jnp.cos(ang), jnp.sin(ang)
    cos2 = jnp.repeat(cos, 2, axis=-1)
    sin2 = jnp.stack([-sin, sin], axis=-1).reshape(S, HEAD_DIM)
    cos2 = jnp.concatenate([cos2, jnp.ones((C, HEAD_DIM), F32)], axis=0)
    sin2 = jnp.concatenate([sin2, jnp.zeros((C, HEAD_DIM), F32)], axis=0)
    return cos2, sin2


def kernel(x, c, ctx, c_ctx, w_mod, b_mod, norm_w, w_ffn1_in, w_ffn1_out, w_ffn2_in, w_ffn2_out, w_in, b_gate, q_norm_w, k_norm_w, gmlp_ln_w, gmlp_ln_b, w_spatial, b_spatial, w_branch_attn, w_branch_gmlp, w_out, final_norm_w, loss_target, m_c_ctx, m_w_mod, m_b_mod, m_norm_w, m_w_ffn1_in, m_w_ffn1_out, m_w_ffn2_in, m_w_ffn2_out, m_w_in, m_b_gate, m_q_norm_w, m_k_norm_w, m_gmlp_ln_w, m_gmlp_ln_b, m_w_spatial, m_b_spatial, m_w_branch_attn, m_w_branch_gmlp, m_w_out, m_final_norm_w, v_c_ctx, v_w_mod, v_b_mod, v_norm_w, v_w_ffn1_in, v_w_ffn1_out, v_w_ffn2_in, v_w_ffn2_out, v_w_in, v_b_gate, v_q_norm_w, v_k_norm_w, v_gmlp_ln_w, v_gmlp_ln_b, v_w_spatial, v_b_spatial, v_w_branch_attn, v_w_branch_gmlp, v_w_out, v_final_norm_w):
    _, S, D = x.shape
    C = ctx.shape[1]
    NTOK = S + C
    n_xt = S // ROW_TILE
    q_w, kv_w = N_Q_HEADS * HEAD_DIM, N_KV_HEADS * HEAD_DIM
    W = gmlp_ln_w.shape[1]
    v_end = q_w + 2 * kv_w
    gv_end = v_end + 2 * W
    scale = HEAD_DIM ** -0.5
    dev = 4 * lax.axis_index("x") + 2 * lax.axis_index("y") + lax.axis_index("c")
    shard = 2 * lax.axis_index("x") + lax.axis_index("y")

    c_all = _allgather_devices(jnp.pad(c, ((0, 7), (0, 0))), "gather_c")[:, 0, :]
    cvecs = jnp.concatenate([c_all, jnp.pad(c_ctx[None, :], ((0, 7), (0, 0)))], axis=0)
    w_mod_l = w_mod[0]
    n_modsh = w_mod_l.shape[1]
    mod_part = _mod_fwd(cvecs, w_mod_l, "mod_fwd")
    mod_all = _allgather_devices(mod_part, "gather_mod")[0::2]
    mod_full = jnp.transpose(mod_all, (1, 0, 2)).reshape(16, N_SHARDS * n_modsh) + b_mod
    mx = lax.dynamic_index_in_dim(mod_full, dev, 0, keepdims=False).reshape(N_MOD, D)
    mc = mod_full[8].reshape(N_MOD, D)
    mods = jnp.concatenate([mx, mc], axis=0).reshape(2 * N_MOD, 1, D)

    shard_of = lambda w: w[0].astype(BF16)
    rows_of = lambda g: g.reshape(-1, g.shape[-1])
    (wf1i,) = _comm_call(_gather_rider([shard_of(w_ffn1_in)]), "gather_ffn1_in")
    n_vsh = norm_w.shape[-1]
    nw_all = _allgather_devices(_pack([norm_w[0], b_gate[0]], n_vsh), "gather_vecs")[0::2]
    nw = jnp.transpose(nw_all[:, 0:3, :], (1, 0, 2)).reshape(3, D)
    bg = jnp.transpose(nw_all[:, PACK_ROWS:PACK_ROWS + 2, :], (1, 0, 2)).reshape(2, 1, D)
    nw0, nw1, nw2 = nw[0:1], nw[1:2], nw[2:3]

    cos2, sin2 = _rope_tables(S, C)
    b_sb = jnp.broadcast_to(b_spatial[0][:, :, None], (GMLP_GROUPS, CHUNK, W // GMLP_GROUPS))
    w_s = w_spatial[0]
    fw = final_norm_w[None, :]

    tok0 = jnp.concatenate([x[0], ctx[0]], axis=0)
    h1 = _normmod(tok0, mods, nw0, k_shift=0, k_scale=1, n_xt=n_xt, name="ffn1_norm")
    z1, (wf1o_g,) = _matmul(h1, wf1i, form="nn", b_shards=N_SHARDS, out_shards=2, name="ffn1_in",
                            rider=_gather_rider([shard_of(w_ffn1_out)]))
    wf1o = rows_of(wf1o_g)
    g1 = _swiglu_fwd(z1, "ffn1_act")
    y1, (win,) = _matmul(g1, wf1o, form="nn", name="ffn1_out", tn=1024, rider=_gather_rider([shard_of(w_in)]))
    tok1, h2 = _normmod(tok0, mods, nw1, k_shift=3, k_scale=4, n_xt=n_xt, name="mix_norm",
                        resid=(y1, 2, MACARON_WEIGHT))
    later = [w_branch_attn, w_branch_gmlp, w_ffn2_in]
    zx, (wba_g, wbg_g, wf2i) = _matmul(h2, win, form="nn", b_shards=N_SHARDS, name="mix_in",
                                       rider=_gather_rider([shard_of(w) for w in later]))
    wba, wbg = rows_of(wba_g), rows_of(wbg_g)
    qt, kt, vt = _qk_prep(zx, cos2, sin2, q_norm_w, k_norm_w, q_w=q_w, kv_w=kv_w, scale=scale, name="qk_prep")
    attn, lse = _flash_fwd(qt, kt, vt, S=S, name="attn_fwd")
    gm = _gmlp_fwd(zx, gmlp_ln_w, gmlp_ln_b, w_s, b_sb, S=S, col0=v_end, name="gmlp_fwd")
    a_br, (wo_g,) = _matmul(attn, wba, form="nn", name="branch_attn", tm=512, rider=_gather_rider([shard_of(w_out)]))
    wo = rows_of(wo_g)
    g_br = _matmul(gm, wbg, form="nn", name="branch_gmlp", tm=512)
    t_mix = _merge_fwd(zx, a_br, g_br, bg, S=S, col0=gv_end, name="merge_fwd")
    y_mix = _matmul(t_mix, wo, form="nn", name="mix_out", tm=512)
    x2, h3 = _normmod(tok1, mods, nw2, k_shift=6, k_scale=7, n_xt=n_xt, name="ffn2_norm", resid=(y_mix, 5, None),
                      rows=S)
    z2, (wf2o_g,) = _matmul(h3, wf2i, form="nn", b_shards=N_SHARDS, out_shards=2, name="ffn2_in",
                            rider=_gather_rider([shard_of(w_ffn2_out)]))
    wf2o = rows_of(wf2o_g)
    g2 = _swiglu_fwd(z2, "ffn2_act")
    y2 = _matmul(g2, wf2o, form="nn", name="ffn2_out", tn=1024)
    x3 = _resid_only(x2, y2, mods, k_gate=8, name="ffn2_resid")

    dx3, loss_row, dfw = _final_loss(x3, loss_target[0], fw, "loss")

    dy2, dgate8 = _resid_bwd(dx3, y2, mods, k_gate=8, weight=MACARON_WEIGHT, n_xt=n_xt, name="ffn2_resid_bwd")
    dg2 = _matmul(dy2, wf2o, form="nt", name="ffn2_out_dgrad")
    gw_f2o = _matmul(g2, dy2, form="tn", out_dtype=BF16, name="ffn2_out_wgrad", tn=1024)
    dz2 = _swiglu_bwd(z2, dg2, "ffn2_act_bwd")
    dh3 = _matmul(dz2, wf2i, form="nt", a_shards=2, b_shards=N_SHARDS, name="ffn2_in_dgrad", tn=1024)
    gw_f2i = _matmul(h3, dz2, form="tn", out_dtype=BF16, b_shards=2, out_shards=N_SHARDS, name="ffn2_in_wgrad")
    dx2, dsh6, dsc7, dnw2 = _normmod_bwd(dh3, x2, dx3, mods, nw2, k_scale=7, n_xt=n_xt, name="ffn2_norm_bwd")

    dy_mix, dgate5 = _resid_bwd(dx2, y_mix, mods, k_gate=5, weight=None, n_xt=n_xt, name="mix_resid_bwd")
    dt = _matmul(dy_mix, wo, form="nt", name="mix_out_dgrad", tm=512)
    gw_wo = _matmul(t_mix, dy_mix, form="tn", out_dtype=BF16, name="mix_out_wgrad", tn=1024)
    d_abr, d_gbr, dl0, dl1, dbg0, dbg1 = _merge_bwd(dt, zx, a_br, g_br, bg, S=S, col0=gv_end, name="merge_bwd")
    d_attn = _matmul(d_abr, wba, form="nt", name="branch_attn_dgrad", tm=512)
    gw_wba = _matmul(attn, d_abr, form="tn", out_dtype=BF16, name="branch_attn_wgrad", tn=1024)
    d_gm = _matmul(d_gbr, wbg, form="nt", name="branch_gmlp_dgrad", tm=512)
    gw_wbg = _matmul(gm, d_gbr, form="tn", out_dtype=BF16, name="branch_gmlp_wgrad", tn=1024)
    dz_gm, dws, dbs_wide, dlnw, dlnb = _gmlp_bwd(zx, d_gm, gmlp_ln_w, gmlp_ln_b, w_s, b_sb, S=S, col0=v_end,
                                                 name="gmlp_bwd")
    delta = _attn_delta(d_attn, attn, G=N_KV_HEADS, name="attn_delta")
    dq, dk, dv = _flash_bwd(qt, kt, vt, d_attn, lse, delta, S=S, name="attn_bwd")
    dz_qkv, dqw, dkw = _qk_prep_bwd(zx, dq, dk, dv, cos2, sin2, q_norm_w, k_norm_w, q_w=q_w, kv_w=kv_w,
                                    scale=scale, n_xt=n_xt, name="qk_prep_bwd")
    ctx_pad = ((0, C), (0, 0))
    dzx = jnp.concatenate([dz_qkv, jnp.pad(dz_gm, ctx_pad), jnp.pad(dl0, ctx_pad), jnp.pad(dl1, ctx_pad)], axis=1)
    slabs_of = lambda g: g.reshape(N_SHARDS, g.shape[0] // N_SHARDS, g.shape[1])
    dh2, (rc_f2i,) = _matmul(dzx, win, form="nt", b_shards=N_SHARDS, name="mix_in_dgrad", tn=1024,
                             rider=_scatter_rider([gw_f2i]))
    gw_win, (rc_f2o, rc_wo, rc_wba, rc_wbg) = _matmul(
        h2, dzx, form="tn", out_dtype=BF16, out_shards=N_SHARDS, name="mix_in_wgrad",
        rider=_scatter_rider([slabs_of(g) for g in (gw_f2o, gw_wo, gw_wba, gw_wbg)]))
    dtok1, dsh3, dsc4, dnw1 = _normmod_bwd(dh2, tok1, dx2, mods, nw1, k_scale=4, n_xt=n_xt, name="mix_norm_bwd")

    dy1, dgate2 = _resid_bwd(dtok1, y1, mods, k_gate=2, weight=MACARON_WEIGHT, n_xt=n_xt, name="ffn1_resid_bwd")
    dg1 = _matmul(dy1, wf1o, form="nt", name="ffn1_out_dgrad")
    gw_f1o, (rc_win,) = _matmul(g1, dy1, form="tn", out_dtype=BF16, name="ffn1_out_wgrad", tn=1024,
                                rider=_scatter_rider([gw_win]))
    dz1 = _swiglu_bwd(z1, dg1, "ffn1_act_bwd")
    gw_f1i, (rc_f1o,) = _matmul(h1, dz1, form="tn", out_dtype=BF16, b_shards=2, out_shards=N_SHARDS, name="ffn1_in_wgrad",
                                rider=_scatter_rider([slabs_of(gw_f1o)]))
    dh1, (rc_f1i,) = _matmul(dz1, wf1i, form="nt", a_shards=2, b_shards=N_SHARDS, name="ffn1_in_dgrad", tn=1024,
                             rider=_scatter_rider([gw_f1i]))
    dtok0, dsh0, dsc1, dnw0 = _normmod_bwd(dh1, tok0, dtok1, mods, nw0, k_scale=1, n_xt=n_xt, name="ffn1_norm_bwd")
    grad_x = dtok0[:S][None]

    big_w = [w_ffn1_in, w_ffn2_in, w_in, w_ffn1_out, w_ffn2_out, w_branch_attn, w_branch_gmlp, w_out]
    big_m = [m_w_ffn1_in, m_w_ffn2_in, m_w_in, m_w_ffn1_out, m_w_ffn2_out, m_w_branch_attn, m_w_branch_gmlp, m_w_out]
    big_v = [v_w_ffn1_in, v_w_ffn2_in, v_w_in, v_w_ffn1_out, v_w_ffn2_out, v_w_branch_attn, v_w_branch_gmlp, v_w_out]
    big_names = ["w_ffn1_in", "w_ffn2_in", "w_in", "w_ffn1_out", "w_ffn2_out", "w_branch_attn", "w_branch_gmlp", "w_out"]
    received = [rc_f1i, rc_f2i, rc_win, rc_f1o, rc_f2o, rc_wba, rc_wbg, rc_wo]
    plane_sums = [_sum_slabs(r, "plane_sum_" + nm) for r, nm in zip(received, big_names)]
    sibling_sums = _sibling_exchange(plane_sums, "sibling_grads")
    big_out = {}
    for nm, w_, m_, v_, pa, pb in zip(big_names, big_w, big_m, big_v, plane_sums, sibling_sums):
        res = _adamw(w_[0], m_[0], v_[0], [pa, pb], "adamw_" + nm)
        big_out[nm] = [r[None] for r in res]

    zeros9 = jnp.zeros((1, D), F32)
    dmx = jnp.concatenate([dsh0[0], dsc1[0], dgate2[0], dsh3[0], dsc4[0], dgate5[0], dsh6[0], dsc7[0], dgate8[0]], axis=0)
    dmc = jnp.concatenate([dsh0[1], dsc1[1], dgate2[1], dsh3[1], dsc4[1], zeros9, zeros9, zeros9, zeros9], axis=0)
    dbs = jnp.sum(dbs_wide, axis=-1)
    parts = [dmx, dmc, jnp.concatenate([dnw0, dnw1, dnw2], axis=0), jnp.concatenate([dbg0, dbg1], axis=0),
             dqw, dkw, dlnw, dlnb, dws, dbs, dfw, loss_row[:, :1]]
    part_shapes = [p.shape for p in parts]
    small_all = _allgather_devices(_pack(parts, D), "gather_small")
    dmx_all = small_all[:, 0:N_MOD, :].reshape(N_DEV, N_MOD * D)
    small_sum = small_all[0]
    for d in range(1, N_DEV):
        small_sum = small_sum + small_all[d]
    (_, dmc_sum, g_nw, g_bg, g_qw, g_kw, g_lnw, g_lnb, g_ws, g_bs, g_fw, loss_sum) = _unpack(small_sum, part_shapes, D)
    loss = loss_sum[0, 0]
    dmx_sum = small_sum[0:N_MOD].reshape(1, N_MOD * D)
    g_b_mod = dmx_sum + dmc_sum.reshape(1, N_MOD * D)
    dm_rows = jnp.concatenate([dmx_all, jnp.pad(dmc_sum.reshape(1, N_MOD * D), ((0, 7), (0, 0)))], axis=0)
    dm_sh = lax.dynamic_slice_in_dim(dm_rows, shard * n_modsh, n_modsh, axis=1)
    g_w_mod = _mod_wgrad(cvecs, dm_sh, "mod_wgrad")
    dsc_part = _mod_dgrad(dm_sh, w_mod_l, "mod_dgrad")
    dsc_all = _allgather_devices(dsc_part, "gather_dsilu")[0::2, 8, :]
    dscc = ((dsc_all[0] + dsc_all[1]) + dsc_all[2]) + dsc_all[3]
    sg = _sigmoid(c_ctx)
    g_c_ctx = dscc * (sg * (1.0 + c_ctx * (1.0 - sg)))
    g_norm_w = lax.dynamic_slice_in_dim(g_nw, shard * n_vsh, n_vsh, axis=1)[None]
    g_b_gate = lax.dynamic_slice_in_dim(g_bg, shard * n_vsh, n_vsh, axis=1)[None]

    w_mod_out = [r[None] for r in _adamw(w_mod_l, m_w_mod[0], v_w_mod[0], [g_w_mod], "adamw_w_mod")]

    small_names = ["c_ctx", "b_mod", "norm_w", "b_gate", "q_norm_w", "k_norm_w", "gmlp_ln_w", "gmlp_ln_b",
                   "w_spatial", "b_spatial", "final_norm_w"]
    small_w = [c_ctx, b_mod, norm_w, b_gate, q_norm_w, k_norm_w, gmlp_ln_w, gmlp_ln_b, w_spatial, b_spatial, final_norm_w]
    small_m = [m_c_ctx, m_b_mod, m_norm_w, m_b_gate, m_q_norm_w, m_k_norm_w, m_gmlp_ln_w, m_gmlp_ln_b, m_w_spatial,
               m_b_spatial, m_final_norm_w]
    small_v = [v_c_ctx, v_b_mod, v_norm_w, v_b_gate, v_q_norm_w, v_k_norm_w, v_gmlp_ln_w, v_gmlp_ln_b, v_w_spatial,
               v_b_spatial, v_final_norm_w]
    small_g = [g_c_ctx, g_b_mod, g_norm_w, g_b_gate, g_qw, g_kw, g_lnw, g_lnb, g_ws, g_bs, g_fw]
    small_shapes = [w_.shape for w_ in small_w]
    packed = [_pack(group, D) for group in (small_w, small_m, small_v, small_g)]
    small_res = [_unpack(r, small_shapes, D) for r in _adamw(*packed[:3], [packed[3]], "adamw_small")]
    small_out = {nm: [small_res[t][i] for t in range(4)] for i, nm in enumerate(small_names)}

    order = ["c_ctx", "w_mod", "b_mod", "norm_w", "w_ffn1_in", "w_ffn1_out", "w_ffn2_in", "w_ffn2_out", "w_in", "b_gate",
             "q_norm_w", "k_norm_w", "gmlp_ln_w", "gmlp_ln_b", "w_spatial", "b_spatial", "w_branch_attn", "w_branch_gmlp",
             "w_out", "final_norm_w"]
    results = {**big_out, **small_out, "w_mod": w_mod_out}
    outs = [loss, grad_x]
    for t in range(4):
        outs += [results[nm][t] for nm in order]
    return tuple(outs)


def _resid_only(xp, y_in, mods, *, k_gate, name):
    M, D = xp.shape
    row = pl.BlockSpec((ROW_TILE, D), lambda i: (i, 0))

    def body(xp_ref, y_ref, g_ref, x_ref):
        x_ref[...] = xp_ref[...] + (MACARON_WEIGHT * g_ref[...]) * y_ref[...]

    return pl.pallas_call(
        body, name=name, grid=(M // ROW_TILE,), out_shape=jax.ShapeDtypeStruct((M, D), F32),
        in_specs=[row, row, pl.BlockSpec((None, 1, D), lambda i: (k_gate, 0, 0))], out_specs=row,
        compiler_params=_params("parallel"),
    )(xp, y_in, mods)
```

```python
import functools
import math

import jax
import jax.numpy as jnp
from jax import lax
from jax.experimental import pallas as pl
from jax.experimental.pallas import tpu as pltpu

F32 = jnp.float32
BF16 = jnp.bfloat16
MESH = pl.DeviceIdType.MESH
HBM_SPEC = pl.BlockSpec(memory_space=pltpu.HBM)
VMEM_SPEC = pl.BlockSpec(memory_space=pltpu.VMEM)

HEAD_DIM = 128
N_Q_HEADS = 16
N_KV_HEADS = 4
GMLP_GROUPS = 16
CHUNK = 128
GRID_W = 64
ROPE_THETA = 10000.0
EPS = 1e-6
MACARON_WEIGHT = 0.5
N_MOD = 9

ADAM_LR = 0.001
ADAM_B1 = 0.9
ADAM_B2 = 0.999
ADAM_EPS = 1e-08
ADAM_WD = 0.01
ADAM_STEP = 10

N_SHARDS = 4
N_DEV = 8
ROW_TILE = 256
KEY_TILES = (2816, 768, 512, 256, 128)
FWD_KEY_TILES = (8448,) + KEY_TILES
LANES = 128
VMEM_LIMIT_BYTES = 48 * 1024 * 1024
ADAM_TILE_BYTES = 1 << 20

NN = (((1,), (0,)), ((), ()))
NT = (((1,), (1,)), ((), ()))
TN = (((0,), (0,)), ((), ()))


def _pick(n, cands):
    for t in cands:
        if t <= n and n % t == 0:
            return t
    raise ValueError(f"no tile for {n} among {cands}")


def _params(*sem):
    return pltpu.CompilerParams(dimension_semantics=sem or None, vmem_limit_bytes=VMEM_LIMIT_BYTES)


def _sigmoid(x):
    return 1.0 / (1.0 + jnp.exp(-x))


def _matmul(a, b, *, form, name, out_dtype=F32, a_shards=1, b_shards=1, out_shards=1, tm=None, tn=None, tk=None,
            order="ji", rider=None):
    if form == "nn":
        M, K = a.shape
        N = b.shape[-1] * b_shards
    elif form == "nt":
        M, K = a.shape[-2], a.shape[-1] * a_shards
        N = b.shape[-2]
    else:
        K, M = a.shape
        N = b.shape[-1] * b_shards
    n_b = N // b_shards if form in ("nn", "tn") else N
    n_o = N // out_shards
    k_a = K // a_shards
    k_b = K // b_shards if form == "nt" else K
    tm = _pick(M, ((tm,) if tm else ()) + (1024, 768, 512, 256, 128, 64, 32, 16))
    tn = _pick(math.gcd(n_b, n_o), ((tn,) if tn else ()) + (1408, 1024, 512, 256, 128))
    tk = tk or _pick(math.gcd(k_a, k_b),
                     (1024, 768, 512, 256, 128) if form == "tn" else (2816, 2048, 1408, 1024, 768, 512, 256, 128))
    nk = K // tk
    nbb, nbo = n_b // tn, n_o // tn
    ka, kb = k_a // tk, k_b // tk
    dims = {"nn": NN, "nt": NT, "tn": TN}[form]

    def ij(g0, g1):
        return (g0, g1) if order == "ij" else (g1, g0)

    def a_map(g0, g1, k):
        i, _ = ij(g0, g1)
        if form == "tn":
            return (k, i)
        return (k // ka, i, k % ka) if a_shards > 1 else (i, k)

    def b_map(g0, g1, k):
        _, j = ij(g0, g1)
        if form == "nt":
            return (k // kb, j, k % kb) if b_shards > 1 else (j, k)
        return (j // nbb, k, j % nbb) if b_shards > 1 else (k, j)

    def o_map(g0, g1, k):
        i, j = ij(g0, g1)
        return (j // nbo, i, j % nbo) if out_shards > 1 else (i, j)

    a_block = (tk, tm) if form == "tn" else (tm, tk)
    if a_shards > 1:
        a_block = (None,) + a_block
    b_block = (tk, tn) if form in ("nn", "tn") else (tn, tk)
    if b_shards > 1:
        b_block = (None,) + b_block
    o_block = (None, tm, tn) if out_shards > 1 else (tm, tn)
    o_shape = (out_shards, M, n_o) if out_shards > 1 else (M, N)

    n_i, n_j = M // tm, N // tn
    grid = (n_i, n_j, nk) if order == "ij" else (n_j, n_i, nk)
    n_ride = len(rider.arrays) if rider else 0
    n_acc = 1 if nk > 1 else 0

    def body(*refs):
        a_ref, b_ref = refs[:2]
        ride_in = refs[2:2 + n_ride]
        o_ref = refs[2 + n_ride]
        ride_out = refs[3 + n_ride:3 + 2 * n_ride]
        scratch = refs[3 + 2 * n_ride:]
        ride_sems = scratch[n_acc:]
        pid = [pl.program_id(d) for d in range(3)]
        if rider:
            @pl.when((pid[0] == 0) & (pid[1] == 0) & (pid[2] == 0))
            def _():
                rider.start(ride_in, ride_out, ride_sems)

        part = lax.dot_general(a_ref[...].astype(BF16), b_ref[...].astype(BF16), dims, preferred_element_type=F32)
        if nk == 1:
            o_ref[...] = part.astype(o_ref.dtype)
        else:
            acc = scratch[0]
            k = pid[2]

            @pl.when(k == 0)
            def _():
                acc[...] = part

            @pl.when(k > 0)
            def _():
                acc[...] += part

            @pl.when(k == nk - 1)
            def _():
                o_ref[...] = acc[...].astype(o_ref.dtype)

        if rider:
            @pl.when((pid[0] == grid[0] - 1) & (pid[1] == grid[1] - 1) & (pid[2] == nk - 1))
            def _():
                rider.finish(ride_in, ride_out, ride_sems)

    main_shape = jax.ShapeDtypeStruct(o_shape, out_dtype)
    res = pl.pallas_call(
        body,
        name=name,
        out_shape=(main_shape, *rider.out_shapes) if rider else main_shape,
        grid=grid,
        in_specs=[pl.BlockSpec(a_block, a_map), pl.BlockSpec(b_block, b_map)] + [HBM_SPEC] * n_ride,
        out_specs=(pl.BlockSpec(o_block, o_map), *[HBM_SPEC] * n_ride) if rider else pl.BlockSpec(o_block, o_map),
        scratch_shapes=([pltpu.VMEM((tm, tn), F32)] if nk > 1 else []) + (rider.sems if rider else []),
        compiler_params=_params(*(("arbitrary",) * 3 if rider else ("parallel", "parallel", "arbitrary"))),
    )(a, b, *(rider.arrays if rider else ()))
    return (res[0], list(res[1:])) if rider else res


def _type_of(i, n_xt, n_tiles):
    return jnp.where(i >= n_xt, 1, 0) if n_tiles > n_xt else 0


def _mod_spec(D, k, n_xt, n_tiles):
    return pl.BlockSpec((None, 1, D), lambda i: (_type_of(i, n_xt, n_tiles) * N_MOD + k, 0, 0))


def _acc_spec(D, n_xt, n_tiles):
    return pl.BlockSpec((None, 1, D), lambda i: (_type_of(i, n_xt, n_tiles), 0, 0))


def _accumulate(ref, value, first):
    @pl.when(first)
    def _():
        ref[...] = value

    @pl.when(jnp.logical_not(first))
    def _():
        ref[...] += value


def _normmod(xp, mods, nw, *, k_shift, k_scale, n_xt, name, resid=None, rows=None):
    M, D = rows or xp.shape[0], xp.shape[1]
    n_tiles = M // ROW_TILE
    row = pl.BlockSpec((ROW_TILE, D), lambda i: (i, 0))
    vec = pl.BlockSpec((1, D), lambda i: (0, 0))
    mod = functools.partial(_mod_spec, D, n_xt=n_xt, n_tiles=n_tiles)

    if resid is not None:
        y_in, k_gate, weight = resid

        def body(xp_ref, y_ref, g_ref, nw_ref, sh_ref, sc_ref, x_ref, h_ref):
            g = g_ref[...] if weight is None else weight * g_ref[...]
            x = xp_ref[...] + g * y_ref[...]
            x_ref[...] = x
            r = lax.rsqrt(jnp.mean(x * x, axis=-1, keepdims=True) + EPS)
            y = (x * r) * nw_ref[...]
            h_ref[...] = (y * (1.0 + sc_ref[...]) + sh_ref[...]).astype(BF16)

        return pl.pallas_call(
            body, name=name, grid=(n_tiles,),
            out_shape=(jax.ShapeDtypeStruct((M, D), F32), jax.ShapeDtypeStruct((M, D), BF16)),
            in_specs=[row, row, mod(k_gate), vec, mod(k_shift), mod(k_scale)],
            out_specs=(row, row), compiler_params=_params("parallel"),
        )(xp, y_in, mods, nw, mods, mods)

    def body(xp_ref, nw_ref, sh_ref, sc_ref, h_ref):
        x = xp_ref[...]
        r = lax.rsqrt(jnp.mean(x * x, axis=-1, keepdims=True) + EPS)
        y = (x * r) * nw_ref[...]
        h_ref[...] = (y * (1.0 + sc_ref[...]) + sh_ref[...]).astype(BF16)

    return pl.pallas_call(
        body, name=name, grid=(n_tiles,), out_shape=jax.ShapeDtypeStruct((M, D), BF16),
        in_specs=[row, vec, mod(k_shift), mod(k_scale)], out_specs=row, compiler_params=_params("parallel"),
    )(xp, nw, mods, mods)


def _resid_bwd(dxo, y_in, mods, *, k_gate, weight, n_xt, name):
    M, D = dxo.shape
    n_tiles = M // ROW_TILE
    n_types = 2 if n_tiles > n_xt else 1
    row = pl.BlockSpec((ROW_TILE, D), lambda i: (i, 0))

    def body(dxo_ref, y_ref, g_ref, dy_ref, dg_ref):
        i = pl.program_id(0)
        dxo_t = dxo_ref[...]
        g = g_ref[...] if weight is None else weight * g_ref[...]
        yw = y_ref[...] if weight is None else weight * y_ref[...]
        dy_ref[...] = (dxo_t * g).astype(BF16)
        _accumulate(dg_ref, jnp.sum(dxo_t * yw, axis=0, keepdims=True), (i == 0) | (i == n_xt))

    return pl.pallas_call(
        body, name=name, grid=(n_tiles,),
        out_shape=(jax.ShapeDtypeStruct((M, D), BF16), jax.ShapeDtypeStruct((n_types, 1, D), F32)),
        in_specs=[row, row, _mod_spec(D, k_gate, n_xt, n_tiles)],
        out_specs=(row, _acc_spec(D, n_xt, n_tiles)), compiler_params=_params("arbitrary"),
    )(dxo, y_in, mods)


def _normmod_bwd(dh, x_in, dxo, mods, nw, *, k_scale, n_xt, name):
    M, D = dh.shape
    n_tiles = M // ROW_TILE
    n_types = 2 if n_tiles > n_xt else 1
    dxo_tiles = dxo.shape[0] // ROW_TILE
    row = pl.BlockSpec((ROW_TILE, D), lambda i: (i, 0))
    dxo_row = pl.BlockSpec((ROW_TILE, D), lambda i: (jnp.minimum(i, dxo_tiles - 1), 0))
    vec = pl.BlockSpec((1, D), lambda i: (0, 0))

    def body(dh_ref, x_ref, dxo_ref, nw_ref, sc_ref, dx_ref, dsh_ref, dsc_ref, dnw_ref):
        i = pl.program_id(0)
        x = x_ref[...]
        dh_t = dh_ref[...]
        w = nw_ref[...]
        r = lax.rsqrt(jnp.mean(x * x, axis=-1, keepdims=True) + EPS)
        n = x * r
        dy = dh_t * (1.0 + sc_ref[...])
        dn = dy * w
        dx = r * (dn - n * jnp.mean(dn * n, axis=-1, keepdims=True))
        if dxo_tiles < n_tiles:
            dx_ref[...] = dx + jnp.where(i < dxo_tiles, dxo_ref[...], 0.0)
        else:
            dx_ref[...] = dx + dxo_ref[...]
        first = (i == 0) | (i == n_xt)
        _accumulate(dsh_ref, jnp.sum(dh_t, axis=0, keepdims=True), first)
        _accumulate(dsc_ref, jnp.sum(dh_t * (n * w), axis=0, keepdims=True), first)
        _accumulate(dnw_ref, jnp.sum(dy * n, axis=0, keepdims=True), i == 0)

    acc = _acc_spec(D, n_xt, n_tiles)
    return pl.pallas_call(
        body, name=name, grid=(n_tiles,),
        out_shape=(jax.ShapeDtypeStruct((M, D), F32), jax.ShapeDtypeStruct((n_types, 1, D), F32),
                   jax.ShapeDtypeStruct((n_types, 1, D), F32), jax.ShapeDtypeStruct((1, D), F32)),
        in_specs=[row, row, dxo_row, vec, _mod_spec(D, k_scale, n_xt, n_tiles)],
        out_specs=(row, acc, acc, vec), compiler_params=_params("arbitrary"),
    )(dh, x_in, dxo, nw, mods)


def _swiglu_fwd(z, name):
    _, M, Fh = z.shape
    tf = _pick(Fh, (1408, 1024, 512, 256, 128))

    def body(z_ref, g_ref):
        a = z_ref[0]
        g_ref[...] = ((a * _sigmoid(a)) * z_ref[1]).astype(BF16)

    return pl.pallas_call(
        body, name=name, grid=(M // ROW_TILE, Fh // tf), out_shape=jax.ShapeDtypeStruct((M, Fh), BF16),
        in_specs=[pl.BlockSpec((2, ROW_TILE, tf), lambda i, j: (0, i, j))],
        out_specs=pl.BlockSpec((ROW_TILE, tf), lambda i, j: (i, j)), compiler_params=_params("parallel", "parallel"),
    )(z)


def _swiglu_bwd(z, dg, name):
    _, M, Fh = z.shape
    tf = _pick(Fh, (1408, 1024, 512, 256, 128))

    def body(z_ref, dg_ref, dz_ref):
        a = z_ref[0]
        sig = _sigmoid(a)
        dg_t = dg_ref[...]
        dz_ref[0] = (dg_t * z_ref[1] * (sig * (1.0 + a * (1.0 - sig)))).astype(BF16)
        dz_ref[1] = (dg_t * (a * sig)).astype(BF16)

    pair = pl.BlockSpec((2, ROW_TILE, tf), lambda i, j: (0, i, j))
    return pl.pallas_call(
        body, name=name, grid=(M // ROW_TILE, Fh // tf), out_shape=jax.ShapeDtypeStruct((2, M, Fh), BF16),
        in_specs=[pair, pl.BlockSpec((ROW_TILE, tf), lambda i, j: (i, j))],
        out_specs=pair, compiler_params=_params("parallel", "parallel"),
    )(z, dg)


def _final_loss(x3, target, fw, name):
    S, D = x3.shape
    row = pl.BlockSpec((ROW_TILE, D), lambda i: (i, 0))
    vec = pl.BlockSpec((1, D), lambda i: (0, 0))

    def body(x_ref, t_ref, fw_ref, dx_ref, loss_ref, dfw_ref):
        i = pl.program_id(0)
        x = x_ref[...]
        w = fw_ref[...]
        r = lax.rsqrt(jnp.mean(x * x, axis=-1, keepdims=True) + EPS)
        n = x * r
        err = n * w - t_ref[...]
        tile_loss = 0.5 * jnp.sum(jnp.mean(err * err, axis=-1, keepdims=True), axis=0, keepdims=True)
        dout = err / D
        dn = dout * w
        dx_ref[...] = r * (dn - n * jnp.mean(dn * n, axis=-1, keepdims=True))
        _accumulate(loss_ref, jnp.broadcast_to(tile_loss, (1, LANES)), i == 0)
        _accumulate(dfw_ref, jnp.sum(dout * n, axis=0, keepdims=True), i == 0)

    return pl.pallas_call(
        body, name=name, grid=(S // ROW_TILE,),
        out_shape=(jax.ShapeDtypeStruct((S, D), F32), jax.ShapeDtypeStruct((1, LANES), F32),
                   jax.ShapeDtypeStruct((1, D), F32)),
        in_specs=[row, row, vec],
        out_specs=(row, pl.BlockSpec((1, LANES), lambda i: (0, 0)), vec), compiler_params=_params("arbitrary"),
    )(x3, target, fw)


def _pair_swap(v):
    lane = lax.broadcasted_iota(jnp.int32, v.shape, v.ndim - 1)
    from_next = pltpu.roll(lane, 1, v.ndim - 1) == (lane ^ 1)
    return jnp.where(from_next, pltpu.roll(v, 1, v.ndim - 1), pltpu.roll(v, LANES - 1, v.ndim - 1))


def _qk_prep(zx, cos2, sin2, qw, kw, *, q_w, kv_w, scale, name):
    M = zx.shape[0]
    nqh, nkh = q_w // HEAD_DIM, kv_w // HEAD_DIM
    row = lambda w, c: pl.BlockSpec((ROW_TILE, w), lambda i: (i, c))
    vec = pl.BlockSpec((1, HEAD_DIM), lambda i: (0, 0))

    def rot(z, w, cos_t, sin_t):
        r = lax.rsqrt(jnp.mean(z * z, axis=-1, keepdims=True) + EPS)
        y = (z * r) * w
        return y * cos_t + _pair_swap(y) * sin_t

    def body(zq_ref, zk_ref, zv_ref, cos_ref, sin_ref, qw_ref, kw_ref, q_ref, k_ref, v_ref):
        cos_t, sin_t = cos_ref[...], sin_ref[...]
        for h in range(nqh):
            hs = slice(h * HEAD_DIM, (h + 1) * HEAD_DIM)
            q_ref[:, hs] = (rot(zq_ref[:, hs], qw_ref[...], cos_t, sin_t) * scale).astype(BF16)
        for h in range(nkh):
            hs = slice(h * HEAD_DIM, (h + 1) * HEAD_DIM)
            k_ref[:, hs] = rot(zk_ref[:, hs], kw_ref[...], cos_t, sin_t).astype(BF16)
        v_ref[...] = zv_ref[...].astype(BF16)

    return pl.pallas_call(
        body, name=name, grid=(M // ROW_TILE,),
        out_shape=(jax.ShapeDtypeStruct((M, q_w), BF16), jax.ShapeDtypeStruct((M, kv_w), BF16),
                   jax.ShapeDtypeStruct((M, kv_w), BF16)),
        in_specs=[row(q_w, 0), row(kv_w, q_w // kv_w), row(kv_w, q_w // kv_w + 1), row(HEAD_DIM, 0), row(HEAD_DIM, 0),
                  vec, vec],
        out_specs=(row(q_w, 0), row(kv_w, 0), row(kv_w, 0)), compiler_params=_params("parallel"),
    )(zx, zx, zx, cos2, sin2, qw, kw)


def _qk_prep_bwd(zx, dq, dk, dv, cos2, sin2, qw, kw, *, q_w, kv_w, scale, n_xt, name):
    M = zx.shape[0]
    nqh, nkh = q_w // HEAD_DIM, kv_w // HEAD_DIM
    dq_tiles = dq.shape[0] // ROW_TILE
    row = lambda w, c: pl.BlockSpec((ROW_TILE, w), lambda i: (i, c))
    vec = pl.BlockSpec((1, HEAD_DIM), lambda i: (0, 0))

    def unrot(z, d, w, cos_t, sin_t):
        r = lax.rsqrt(jnp.mean(z * z, axis=-1, keepdims=True) + EPS)
        n = z * r
        dy = d * cos_t - _pair_swap(d) * sin_t
        dn = dy * w
        dz = r * (dn - n * jnp.mean(dn * n, axis=-1, keepdims=True))
        return dz, jnp.sum(dy * n, axis=0, keepdims=True)

    def body(zq_ref, zk_ref, dq_ref, dk_ref, dv_ref, cos_ref, sin_ref, qw_ref, kw_ref, dz_ref, dqw_ref, dkw_ref):
        i = pl.program_id(0)
        cos_t, sin_t = cos_ref[...], sin_ref[...]
        is_x = i < n_xt
        dqw = jnp.zeros((1, HEAD_DIM), F32)
        dkw = jnp.zeros((1, HEAD_DIM), F32)
        for h in range(nqh):
            hs = slice(h * HEAD_DIM, (h + 1) * HEAD_DIM)
            d = jnp.where(is_x, dq_ref[:, hs], 0.0) * scale
            dz, dw = unrot(zq_ref[:, hs], d, qw_ref[...], cos_t, sin_t)
            dz_ref[:, hs] = dz.astype(BF16)
            dqw = dqw + dw
        for h in range(nkh):
            hs = slice(h * HEAD_DIM, (h + 1) * HEAD_DIM)
            dz, dw = unrot(zk_ref[:, hs], dk_ref[:, hs], kw_ref[...], cos_t, sin_t)
            dz_ref[:, q_w + h * HEAD_DIM:q_w + (h + 1) * HEAD_DIM] = dz.astype(BF16)
            dkw = dkw + dw
        dz_ref[:, q_w + kv_w:] = dv_ref[...].astype(BF16)
        _accumulate(dqw_ref, dqw, i == 0)
        _accumulate(dkw_ref, dkw, i == 0)

    return pl.pallas_call(
        body, name=name, grid=(M // ROW_TILE,),
        out_shape=(jax.ShapeDtypeStruct((M, q_w + 2 * kv_w), BF16), jax.ShapeDtypeStruct((1, HEAD_DIM), F32),
                   jax.ShapeDtypeStruct((1, HEAD_DIM), F32)),
        in_specs=[row(q_w, 0), row(kv_w, q_w // kv_w),
                  pl.BlockSpec((ROW_TILE, q_w), lambda i: (jnp.minimum(i, dq_tiles - 1), 0)),
                  row(kv_w, 0), row(kv_w, 0), row(HEAD_DIM, 0), row(HEAD_DIM, 0), vec, vec],
        out_specs=(row(q_w + 2 * kv_w, 0), vec, vec), compiler_params=_params("arbitrary"),
    )(zx, zx, dq, dk, dv, cos2, sin2, qw, kw)


def _lane_pick(tile, h):
    lane = lax.broadcasted_iota(jnp.int32, tile.shape, 1)
    return jnp.sum(jnp.where(lane == h, tile, 0.0), axis=-1, keepdims=True)


def _flash_fwd(q, k, v, *, S, name):
    NK, kv_w = k.shape
    G = kv_w // HEAD_DIM
    qpk = q.shape[1] // kv_w
    gw = qpk * HEAD_DIM
    tq = _pick(S, (512, 256, 128))
    tk = _pick(NK, FWD_KEY_TILES)
    nk = NK // tk

    def body(q_ref, k_ref, v_ref, o_ref, lse_ref, m_s, l_s, acc_s):
        ki = pl.program_id(2)

        @pl.when(ki == 0)
        def _():
            m_s[...] = jnp.full(m_s.shape, -1e30, F32)
            l_s[...] = jnp.zeros(l_s.shape, F32)
            acc_s[...] = jnp.zeros(acc_s.shape, F32)

        k_t, v_t = k_ref[...], v_ref[...]
        for h in range(qpk):
            s = lax.dot_general(q_ref[:, h * HEAD_DIM:(h + 1) * HEAD_DIM], k_t, NT, preferred_element_type=F32)
            m_prev = m_s[h]
            m_new = jnp.maximum(m_prev, jnp.max(s, axis=-1, keepdims=True))
            alpha = jnp.exp(m_prev - m_new)
            p = jnp.exp(s - m_new)
            l_s[h] = alpha * l_s[h] + jnp.sum(p, axis=-1, keepdims=True)
            acc_s[h] = alpha * acc_s[h] + lax.dot_general(p.astype(BF16), v_t, NN, preferred_element_type=F32)
            m_s[h] = m_new

        @pl.when(ki == nk - 1)
        def _():
            lane = lax.broadcasted_iota(jnp.int32, (tq, LANES), 1)
            lse = jnp.zeros((tq, LANES), F32)
            for h in range(qpk):
                l = l_s[h]
                o_ref[:, h * HEAD_DIM:(h + 1) * HEAD_DIM] = acc_s[h] / l
                lse = jnp.where(lane == h, m_s[h] + jnp.log(l), lse)
            lse_ref[...] = lse

    return pl.pallas_call(
        body, name=name, grid=(G, S // tq, nk),
        out_shape=(jax.ShapeDtypeStruct((S, G * gw), F32), jax.ShapeDtypeStruct((G, S, LANES), F32)),
        in_specs=[pl.BlockSpec((tq, gw), lambda g, i, j: (i, g)),
                  pl.BlockSpec((tk, HEAD_DIM), lambda g, i, j: (j, g)),
                  pl.BlockSpec((tk, HEAD_DIM), lambda g, i, j: (j, g))],
        out_specs=(pl.BlockSpec((tq, gw), lambda g, i, j: (i, g)),
                   pl.BlockSpec((None, tq, LANES), lambda g, i, j: (g, i, 0))),
        scratch_shapes=[pltpu.VMEM((qpk, tq, 1), F32), pltpu.VMEM((qpk, tq, 1), F32),
                        pltpu.VMEM((qpk, tq, HEAD_DIM), F32)],
        compiler_params=_params("parallel", "parallel", "arbitrary"),
    )(q, k, v)


def _attn_delta(do, o, *, G, name):
    S, q_w = o.shape
    gw = q_w // G
    qpk = gw // HEAD_DIM
    tq = _pick(S, (512, 256, 128))

    def body(do_ref, o_ref, d_ref):
        lane = lax.broadcasted_iota(jnp.int32, (tq, LANES), 1)
        out = jnp.zeros((tq, LANES), F32)
        for h in range(qpk):
            hs = slice(h * HEAD_DIM, (h + 1) * HEAD_DIM)
            out = jnp.where(lane == h, jnp.sum(do_ref[:, hs] * o_ref[:, hs], axis=-1, keepdims=True), out)
        d_ref[...] = out

    return pl.pallas_call(
        body, name=name, grid=(G, S // tq), out_shape=jax.ShapeDtypeStruct((G, S, LANES), F32),
        in_specs=[pl.BlockSpec((tq, gw), lambda g, i: (i, g)), pl.BlockSpec((tq, gw), lambda g, i: (i, g))],
        out_specs=pl.BlockSpec((None, tq, LANES), lambda g, i: (g, i, 0)),
        compiler_params=_params("parallel", "parallel"),
    )(do, o)


def _flash_bwd(q, k, v, do, lse, delta, *, S, name):
    NK, kv_w = k.shape
    G = kv_w // HEAD_DIM
    qpk = q.shape[1] // kv_w
    gw = qpk * HEAD_DIM
    tq = _pick(S, (512, 256, 128))
    tk = _pick(NK, KEY_TILES)

    def body(q_ref, k_ref, v_ref, do_ref, lse_ref, dl_ref, dq_ref, dk_ref, dv_ref, lse_s, dl_s):
        qi, ki = pl.program_id(1), pl.program_id(2)

        @pl.when((qi == 0) & (ki == 0))
        def _():
            dk_ref[...] = jnp.zeros(dk_ref.shape, F32)
            dv_ref[...] = jnp.zeros(dv_ref.shape, F32)

        @pl.when(ki == 0)
        def _():
            dq_ref[...] = jnp.zeros(dq_ref.shape, F32)
            for h in range(qpk):
                lse_s[h] = _lane_pick(lse_ref[...], h)
                dl_s[h] = _lane_pick(dl_ref[...], h)

        k_t, v_t = k_ref[...], v_ref[...]
        dk_acc = jnp.zeros((tk, HEAD_DIM), F32)
        dv_acc = jnp.zeros((tk, HEAD_DIM), F32)
        for h in range(qpk):
            hs = slice(h * HEAD_DIM, (h + 1) * HEAD_DIM)
            q_h = q_ref[:, hs]
            do_h = do_ref[:, hs].astype(BF16)
            s = lax.dot_general(q_h, k_t, NT, preferred_element_type=F32)
            p = jnp.exp(s - lse_s[h])
            dv_acc = dv_acc + lax.dot_general(p.astype(BF16), do_h, TN, preferred_element_type=F32)
            dp = lax.dot_general(do_h, v_t, NT, preferred_element_type=F32)
            ds = (p * (dp - dl_s[h])).astype(BF16)
            dq_ref[:, hs] += lax.dot_general(ds, k_t, NN, preferred_element_type=F32)
            dk_acc = dk_acc + lax.dot_general(ds, q_h, TN, preferred_element_type=F32)
        rows = pl.ds(pl.multiple_of(ki * tk, tk), tk)
        dk_ref[rows, :] += dk_acc
        dv_ref[rows, :] += dv_acc

    qspec = pl.BlockSpec((tq, gw), lambda g, i, j: (i, g))
    kspec = pl.BlockSpec((tk, HEAD_DIM), lambda g, i, j: (j, g))
    lspec = pl.BlockSpec((None, tq, LANES), lambda g, i, j: (g, i, 0))
    group = pl.BlockSpec((NK, HEAD_DIM), lambda g, i, j: (0, g))
    kv_shape = jax.ShapeDtypeStruct((NK, kv_w), F32)
    return pl.pallas_call(
        body, name=name, grid=(G, S // tq, NK // tk),
        out_shape=(jax.ShapeDtypeStruct((S, G * gw), F32), kv_shape, kv_shape),
        in_specs=[qspec, kspec, kspec, qspec, lspec, lspec], out_specs=(qspec, group, group),
        scratch_shapes=[pltpu.VMEM((qpk, tq, 1), F32), pltpu.VMEM((qpk, tq, 1), F32)],
        compiler_params=_params("arbitrary", "arbitrary", "arbitrary"),
    )(q, k, v, do, lse, delta)


def _wide_specs(col0, width, bw):
    return [pl.BlockSpec((ROW_TILE, bw), functools.partial(lambda i, c: (i, c), c=col0 // bw + p))
            for p in range(width // bw)]


def _cat(refs):
    return refs[0][...] if len(refs) == 1 else jnp.concatenate([r[...] for r in refs], axis=1)


def _gelu(x):
    return 0.5 * x * (1.0 + lax.erf(x * (1.0 / math.sqrt(2.0))))


def _gelu_grad(x):
    return 0.5 * (1.0 + lax.erf(x * (1.0 / math.sqrt(2.0)))) + x * jnp.exp(-0.5 * x * x) * (1.0 / math.sqrt(2.0 * math.pi))


def _layernorm_stats(v):
    mu = jnp.mean(v, axis=-1, keepdims=True)
    xc = v - mu
    rstd = lax.rsqrt(jnp.mean(xc * xc, axis=-1, keepdims=True) + EPS)
    return xc * rstd, rstd


def _gmlp_fwd(zx, ln_w, ln_b, w_s, b_sb, *, S, col0, name):
    G, W = w_s.shape[0], ln_w.shape[1]
    gd = W // G
    bw = math.gcd(col0, W)
    n_parts = W // bw
    vec = pl.BlockSpec((1, W), lambda i: (0, 0))
    full3 = pl.BlockSpec((G, CHUNK, CHUNK), lambda i: (0, 0, 0))
    full3b = pl.BlockSpec((G, CHUNK, gd), lambda i: (0, 0, 0))

    def body(*refs):
        u_refs, v_refs = refs[:n_parts], refs[n_parts:2 * n_parts]
        lnw_ref, lnb_ref, ws_ref, bs_ref, gm_ref = refs[2 * n_parts:]
        u = _gelu(_cat(u_refs))
        vhat, _ = _layernorm_stats(_gelu(_cat(v_refs)))
        vn = (vhat * lnw_ref[...] + lnb_ref[...]).astype(BF16)
        for c in range(ROW_TILE // CHUNK):
            rs = slice(c * CHUNK, (c + 1) * CHUNK)
            for g in range(G):
                cs = slice(g * gd, (g + 1) * gd)
                mixed = lax.dot_general(ws_ref[g].astype(BF16), vn[rs, cs], NN, preferred_element_type=F32) + bs_ref[g]
                gm_ref[rs, cs] = (u[rs, cs] * mixed).astype(BF16)

    return pl.pallas_call(
        body, name=name, grid=(S // ROW_TILE,), out_shape=jax.ShapeDtypeStruct((S, W), BF16),
        in_specs=_wide_specs(col0, W, bw) + _wide_specs(col0 + W, W, bw) + [vec, vec, full3, full3b],
        out_specs=pl.BlockSpec((ROW_TILE, W), lambda i: (i, 0)), compiler_params=_params("parallel"),
    )(*([zx] * (2 * n_parts)), ln_w, ln_b, w_s, b_sb)


def _gmlp_bwd(zx, dgm, ln_w, ln_b, w_s, b_sb, *, S, col0, name):
    G, W = w_s.shape[0], ln_w.shape[1]
    gd = W // G
    bw = math.gcd(col0, W)
    n_parts = W // bw
    vec = pl.BlockSpec((1, W), lambda i: (0, 0))
    full3 = pl.BlockSpec((G, CHUNK, CHUNK), lambda i: (0, 0, 0))
    full3b = pl.BlockSpec((G, CHUNK, gd), lambda i: (0, 0, 0))
    row = pl.BlockSpec((ROW_TILE, W), lambda i: (i, 0))

    def body(*refs):
        u_refs, v_refs = refs[:n_parts], refs[n_parts:2 * n_parts]
        dgm_ref, lnw_ref, lnb_ref, ws_ref, bs_ref, dz_ref, dws_ref, dbs_ref, dlnw_ref, dlnb_ref, du_s, dvn_s = refs[2 * n_parts:]
        i = pl.program_id(0)
        upre, vpre = _cat(u_refs), _cat(v_refs)
        u = _gelu(upre)
        vhat, rstd = _layernorm_stats(_gelu(vpre))
        lnw = lnw_ref[...]
        vn = (vhat * lnw + lnb_ref[...]).astype(BF16)
        dgm_t = dgm_ref[...]

        @pl.when(i == 0)
        def _():
            dws_ref[...] = jnp.zeros(dws_ref.shape, F32)
            dbs_ref[...] = jnp.zeros(dbs_ref.shape, F32)

        for c in range(ROW_TILE // CHUNK):
            rs = slice(c * CHUNK, (c + 1) * CHUNK)
            for g in range(G):
                cs = slice(g * gd, (g + 1) * gd)
                ws_g = ws_ref[g].astype(BF16)
                vn_cg = vn[rs, cs]
                mixed = lax.dot_general(ws_g, vn_cg, NN, preferred_element_type=F32) + bs_ref[g]
                dgm_cg = dgm_t[rs, cs]
                du_s[rs, cs] = dgm_cg * mixed
                dmixed = dgm_cg * u[rs, cs]
                dmixed_b = dmixed.astype(BF16)
                dws_ref[g] += lax.dot_general(dmixed_b, vn_cg, NT, preferred_element_type=F32)
                dbs_ref[g] += dmixed
                dvn_s[rs, cs] = lax.dot_general(ws_g, dmixed_b, TN, preferred_element_type=F32)

        dvn = dvn_s[...]
        _accumulate(dlnw_ref, jnp.sum(dvn * vhat, axis=0, keepdims=True), i == 0)
        _accumulate(dlnb_ref, jnp.sum(dvn, axis=0, keepdims=True), i == 0)
        dvhat = dvn * lnw
        dv = rstd * (dvhat - jnp.mean(dvhat, axis=-1, keepdims=True)
                     - vhat * jnp.mean(dvhat * vhat, axis=-1, keepdims=True))
        dz_ref[:, :W] = (du_s[...] * _gelu_grad(upre)).astype(BF16)
        dz_ref[:, W:] = (dv * _gelu_grad(vpre)).astype(BF16)

    return pl.pallas_call(
        body, name=name, grid=(S // ROW_TILE,),
        out_shape=(jax.ShapeDtypeStruct((S, 2 * W), BF16), jax.ShapeDtypeStruct((G, CHUNK, CHUNK), F32),
                   jax.ShapeDtypeStruct((G, CHUNK, gd), F32), jax.ShapeDtypeStruct((1, W), F32),
                   jax.ShapeDtypeStruct((1, W), F32)),
        in_specs=_wide_specs(col0, W, bw) + _wide_specs(col0 + W, W, bw) + [row, vec, vec, full3, full3b],
        out_specs=(pl.BlockSpec((ROW_TILE, 2 * W), lambda i: (i, 0)), full3, full3b, vec, vec),
        scratch_shapes=[pltpu.VMEM((ROW_TILE, W), F32), pltpu.VMEM((ROW_TILE, W), F32)],
        compiler_params=_params("arbitrary"),
    )(*([zx] * (2 * n_parts)), dgm, ln_w, ln_b, w_s, b_sb)


def _merge_fwd(zx, a_br, g_br, b_gate, *, S, col0, name):
    D = a_br.shape[1]
    cw = min(math.gcd(col0, D), 1024)
    nc = D // cw
    c0 = col0 // cw
    blk = lambda f: pl.BlockSpec((ROW_TILE, cw), f)
    bias = lambda t: pl.BlockSpec((None, 1, cw), lambda i, j: (t, 0, j))

    def body(l0_ref, l1_ref, a_ref, g_ref, b0_ref, b1_ref, t_ref):
        g0 = _sigmoid(l0_ref[...] + b0_ref[...])
        g1 = _sigmoid(l1_ref[...] + b1_ref[...])
        t_ref[...] = (g0 * a_ref[...] + g1 * g_ref[...]).astype(BF16)

    return pl.pallas_call(
        body, name=name, grid=(S // ROW_TILE, nc), out_shape=jax.ShapeDtypeStruct((S, D), BF16),
        in_specs=[blk(lambda i, j: (i, c0 + j)), blk(lambda i, j: (i, c0 + nc + j)), blk(lambda i, j: (i, j)),
                  blk(lambda i, j: (i, j)), bias(0), bias(1)],
        out_specs=blk(lambda i, j: (i, j)), compiler_params=_params("parallel", "parallel"),
    )(zx, zx, a_br, g_br, b_gate, b_gate)


def _merge_bwd(dt, zx, a_br, g_br, b_gate, *, S, col0, name):
    D = a_br.shape[1]
    cw = min(math.gcd(col0, D), 1024)
    nc = D // cw
    c0 = col0 // cw
    blk = lambda f: pl.BlockSpec((ROW_TILE, cw), f)
    bias = lambda t: pl.BlockSpec((None, 1, cw), lambda j, i: (t, 0, j))
    own = blk(lambda j, i: (i, j))
    acc = pl.BlockSpec((1, cw), lambda j, i: (0, j))

    def body(dt_ref, l0_ref, l1_ref, a_ref, g_ref, b0_ref, b1_ref, da_ref, dg_ref, dl0_ref, dl1_ref, db0_ref, db1_ref):
        i = pl.program_id(1)
        dt_t = dt_ref[...]
        g0 = _sigmoid(l0_ref[...] + b0_ref[...])
        g1 = _sigmoid(l1_ref[...] + b1_ref[...])
        da_ref[...] = (dt_t * g0).astype(BF16)
        dg_ref[...] = (dt_t * g1).astype(BF16)
        dl0 = dt_t * a_ref[...] * (g0 * (1.0 - g0))
        dl1 = dt_t * g_ref[...] * (g1 * (1.0 - g1))
        dl0_ref[...] = dl0.astype(BF16)
        dl1_ref[...] = dl1.astype(BF16)
        _accumulate(db0_ref, jnp.sum(dl0, axis=0, keepdims=True), i == 0)
        _accumulate(db1_ref, jnp.sum(dl1, axis=0, keepdims=True), i == 0)

    sd = lambda dt_: jax.ShapeDtypeStruct((S, D), dt_)
    return pl.pallas_call(
        body, name=name, grid=(nc, S // ROW_TILE),
        out_shape=(sd(BF16), sd(BF16), sd(BF16), sd(BF16), jax.ShapeDtypeStruct((1, D), F32),
                   jax.ShapeDtypeStruct((1, D), F32)),
        in_specs=[own, blk(lambda j, i: (i, c0 + j)), blk(lambda j, i: (i, c0 + nc + j)), own, own, bias(0), bias(1)],
        out_specs=(own, own, own, own, acc, acc), compiler_params=_params("parallel", "arbitrary"),
    )(dt, zx, zx, a_br, g_br, b_gate, b_gate)


def _mod_fwd(cvecs, w_mod, name):
    R, D = cvecs.shape
    nsh = w_mod.shape[1]
    tn = _pick(nsh, (512, 256, 128))

    def body(c_ref, w_ref, o_ref):
        cv = c_ref[...]
        a = (cv * _sigmoid(cv)).astype(BF16)
        o_ref[...] = lax.dot_general(a, w_ref[...].astype(BF16), NN, preferred_element_type=F32)

    return pl.pallas_call(
        body, name=name, grid=(nsh // tn,), out_shape=jax.ShapeDtypeStruct((R, nsh), F32),
        in_specs=[pl.BlockSpec((R, D), lambda j: (0, 0)), pl.BlockSpec((D, tn), lambda j: (0, j))],
        out_specs=pl.BlockSpec((R, tn), lambda j: (0, j)), compiler_params=_params("parallel"),
    )(cvecs, w_mod)


def _mod_wgrad(cvecs, dm, name):
    R, D = cvecs.shape
    nsh = dm.shape[1]
    tn = _pick(nsh, (512, 256, 128))

    def body(c_ref, dm_ref, o_ref):
        cv = c_ref[...]
        a = (cv * _sigmoid(cv)).astype(BF16)
        o_ref[...] = lax.dot_general(a, dm_ref[...].astype(BF16), TN, preferred_element_type=F32)

    return pl.pallas_call(
        body, name=name, grid=(nsh // tn,), out_shape=jax.ShapeDtypeStruct((D, nsh), F32),
        in_specs=[pl.BlockSpec((R, D), lambda j: (0, 0)), pl.BlockSpec((R, tn), lambda j: (0, j))],
        out_specs=pl.BlockSpec((D, tn), lambda j: (0, j)), compiler_params=_params("parallel"),
    )(cvecs, dm)


def _mod_dgrad(dm, w_mod, name):
    R, nsh = dm.shape
    D = w_mod.shape[0]
    tn = _pick(D, (256, 128))

    def body(dm_ref, w_ref, o_ref):
        o_ref[...] = lax.dot_general(dm_ref[...].astype(BF16), w_ref[...].astype(BF16), NT, preferred_element_type=F32)

    return pl.pallas_call(
        body, name=name, grid=(D // tn,), out_shape=jax.ShapeDtypeStruct((R, D), F32),
        in_specs=[pl.BlockSpec((R, nsh), lambda j: (0, 0)), pl.BlockSpec((tn, nsh), lambda j: (j, 0))],
        out_specs=pl.BlockSpec((R, tn), lambda j: (0, j)), compiler_params=_params("parallel"),
    )(dm, w_mod)


def _adam_rows(R, C):
    return _pick(R, [t for t in (512, 256, 128, 64, 32, 16) if t * C * 4 <= ADAM_TILE_BYTES] or [16])


def _adamw(w, m, v, grads, name):
    R, C = w.shape
    tr = _adam_rows(R, C)
    n_g = len(grads)
    blk = pl.BlockSpec((tr, C), lambda i: (i, 0))
    c1 = 1.0 - ADAM_B1 ** ADAM_STEP
    c2 = 1.0 - ADAM_B2 ** ADAM_STEP

    def body(*refs):
        w_ref, m_ref, v_ref = refs[:3]
        g_refs = refs[3:3 + n_g]
        g_ref, d_ref, m2_ref, v2_ref = refs[3 + n_g:]
        g = g_refs[0][...].astype(F32)
        for r in g_refs[1:]:
            g = g + r[...].astype(F32)
        m2 = ADAM_B1 * m_ref[...] + (1.0 - ADAM_B1) * g
        v2 = ADAM_B2 * v_ref[...] + (1.0 - ADAM_B2) * (g * g)
        g_ref[...] = g
        m2_ref[...] = m2
        v2_ref[...] = v2
        d_ref[...] = -ADAM_LR * ((m2 / c1) / (jnp.sqrt(v2 / c2) + ADAM_EPS) + ADAM_WD * w_ref[...])

    out = jax.ShapeDtypeStruct((R, C), F32)
    return pl.pallas_call(
        body, name=name, grid=(R // tr,), out_shape=(out, out, out, out),
        in_specs=[blk] * (3 + n_g), out_specs=(blk, blk, blk, blk), compiler_params=_params("parallel"),
    )(w, m, v, *grads)


def _sum_slabs(slabs, name):
    n, R, C = slabs.shape
    tr = _adam_rows(R, C)

    def body(s_ref, o_ref):
        acc = s_ref[0].astype(F32)
        for k in range(1, n):
            acc = acc + s_ref[k].astype(F32)
        o_ref[...] = acc.astype(BF16)

    return pl.pallas_call(
        body, name=name, grid=(R // tr,), out_shape=jax.ShapeDtypeStruct((R, C), BF16),
        in_specs=[pl.BlockSpec((n, tr, C), lambda i: (0, i, 0))],
        out_specs=pl.BlockSpec((tr, C), lambda i: (i, 0)), compiler_params=_params("parallel"),
    )(slabs)


def _plane_peers():
    x, y = lax.axis_index("x"), lax.axis_index("y")
    return [(1 - x, y), (x, 1 - y), (1 - x, 1 - y)]


class _Rider:
    def __init__(self, arrays, out_shapes, copies, relay=None):
        n = len(arrays)
        self.arrays, self.out_shapes = list(arrays), list(out_shapes)
        self.sems = [pltpu.SemaphoreType.DMA((3 * n,)), pltpu.SemaphoreType.DMA((3 * n,)), pltpu.SemaphoreType.DMA((n,))]
        if relay:
            self.sems += [pltpu.SemaphoreType.DMA((3 * n,)), pltpu.SemaphoreType.DMA((3 * n,))]
        self._copies, self._relay = copies, relay

    def start(self, ins, outs, sems):
        local, sends, _ = self._copies(ins, outs, sems)
        for cp in local + sends:
            cp.start()

    def finish(self, ins, outs, sems):
        local, sends, recvs = self._copies(ins, outs, sems)
        passed, landed = self._relay(ins, outs, sems) if self._relay else ([None] * len(recvs), [])
        for cp, on in zip(recvs, passed):
            cp.wait_recv()
            if on is not None:
                on.start()
        for cp in landed:
            cp.wait_recv()
        for cp in sends + [on for on in passed if on is not None]:
            cp.wait_send()
        for cp in local:
            cp.wait()


def _gather_rider(shards):
    n = len(shards)
    half_rows = [s.shape[0] // 2 for s in shards]

    def half(ref, w, hc):
        return ref.at[pl.ds(pl.multiple_of(hc * half_rows[w], 16), half_rows[w]), :]

    def copies(ins, outs, sems):
        send_sems, recv_sems, local_sems = sems[:3]
        x, y, c = lax.axis_index("x"), lax.axis_index("y"), lax.axis_index("c")
        me = 2 * x + y
        peers = _plane_peers()

        def remote(w, k, slab):
            px, py = peers[k]
            return pltpu.make_async_remote_copy(
                src_ref=half(ins[w], w, c), dst_ref=half(outs[w].at[slab], w, c), send_sem=send_sems.at[3 * w + k],
                recv_sem=recv_sems.at[3 * w + k], device_id=(px, py, c), device_id_type=MESH)

        local = [pltpu.make_async_copy(ins[w], outs[w].at[me], local_sems.at[w]) for w in range(n)]
        sends = [remote(w, k, me) for w in range(n) for k in range(3)]
        recvs = [remote(w, k, 2 * px + py) for w in range(n) for k, (px, py) in enumerate(peers)]
        return local, sends, recvs

    def relay(ins, outs, sems):
        send_sems, recv_sems = sems[3:]
        x, y, c = lax.axis_index("x"), lax.axis_index("y"), lax.axis_index("c")
        peers = _plane_peers()

        def sibling(w, k, hc):
            px, py = peers[k]
            part = half(outs[w].at[2 * px + py], w, hc)
            return pltpu.make_async_remote_copy(
                src_ref=part, dst_ref=part, send_sem=send_sems.at[3 * w + k], recv_sem=recv_sems.at[3 * w + k],
                device_id=(x, y, 1 - c), device_id_type=MESH)

        passed = [sibling(w, k, c) for w in range(n) for k in range(3)]
        landed = [sibling(w, k, 1 - c) for w in range(n) for k in range(3)]
        return passed, landed

    return _Rider(shards, [jax.ShapeDtypeStruct((N_SHARDS,) + s.shape, s.dtype) for s in shards], copies, relay)


def _scatter_rider(fulls):
    n = len(fulls)

    def copies(ins, outs, sems):
        send_sems, recv_sems, local_sems = sems
        x, y, c = lax.axis_index("x"), lax.axis_index("y"), lax.axis_index("c")
        me = 2 * x + y
        peers = _plane_peers()

        def remote(w, k):
            px, py = peers[k]
            return pltpu.make_async_remote_copy(
                src_ref=ins[w].at[2 * px + py], dst_ref=outs[w].at[k], send_sem=send_sems.at[3 * w + k],
                recv_sem=recv_sems.at[3 * w + k], device_id=(px, py, c), device_id_type=MESH)

        local = [pltpu.make_async_copy(ins[w].at[me], outs[w].at[3], local_sems.at[w]) for w in range(n)]
        sends = [remote(w, k) for w in range(n) for k in range(3)]
        return local, sends, sends

    return _Rider(fulls, [jax.ShapeDtypeStruct(f.shape, f.dtype) for f in fulls], copies)


def _comm_call(rider, name):
    n = len(rider.arrays)

    def body(*refs):
        ins, outs, sems = refs[:n], refs[n:2 * n], refs[2 * n:]
        rider.start(ins, outs, sems)
        rider.finish(ins, outs, sems)

    return list(pl.pallas_call(
        body, name=name, out_shape=tuple(rider.out_shapes), in_specs=[HBM_SPEC] * n, out_specs=tuple([HBM_SPEC] * n),
        scratch_shapes=rider.sems,
    )(*rider.arrays))


def _sibling_exchange(blocks, name):
    n = len(blocks)

    def body(*refs):
        ins, outs = refs[:n], refs[n:2 * n]
        send_sems, recv_sems = refs[2 * n:]
        sibling = (lax.axis_index("x"), lax.axis_index("y"), 1 - lax.axis_index("c"))
        sends = [pltpu.make_async_remote_copy(src_ref=ins[w], dst_ref=outs[w], send_sem=send_sems.at[w],
                                              recv_sem=recv_sems.at[w], device_id=sibling, device_id_type=MESH)
                 for w in range(n)]
        for cp in sends:
            cp.start()
        for cp in sends:
            cp.wait_recv()
        for cp in sends:
            cp.wait_send()

    return pl.pallas_call(
        body, name=name, out_shape=tuple(jax.ShapeDtypeStruct(b.shape, b.dtype) for b in blocks),
        in_specs=[HBM_SPEC] * n, out_specs=tuple([HBM_SPEC] * n),
        scratch_shapes=[pltpu.SemaphoreType.DMA((n,)), pltpu.SemaphoreType.DMA((n,))],
    )(*blocks)


def _allgather_devices(block, name):
    m_per, n_cols = block.shape

    def body(x_ref, out_ref, send_sems, recv_sems, local_sem):
        x, y, c = lax.axis_index("x"), lax.axis_index("y"), lax.axis_index("c")
        me, sibling = (x, y, c), (x, y, 1 - c)
        chips = _plane_peers()

        def rows(px, py, pc):
            return out_ref.at[pl.ds((4 * px + 2 * py + pc) * m_per, m_per), :]

        def copy(k, blk, to, src=None):
            return pltpu.make_async_remote_copy(
                src_ref=rows(*blk) if src is None else src, dst_ref=rows(*blk), send_sem=send_sems.at[k],
                recv_sem=recv_sems.at[k], device_id=to, device_id_type=MESH)

        mine = pltpu.make_async_copy(x_ref, rows(*me), local_sem)
        mine.start()
        first = [copy(0, me, sibling, src=x_ref)]
        first += [copy(1 + j, me, (*chip, c), src=x_ref) for j, chip in enumerate(chips)]
        for cp in first:
            cp.start()
        passed = [copy(4 + j, (*chip, c), sibling) for j, chip in enumerate(chips)]
        for j, chip in enumerate(chips):
            copy(1 + j, (*chip, c), me).wait_recv()
            passed[j].start()
        copy(0, sibling, me).wait_recv()
        for j, chip in enumerate(chips):
            copy(4 + j, (*chip, 1 - c), me).wait_recv()
        for cp in first + passed:
            cp.wait_send()
        mine.wait()

    out = pl.pallas_call(
        body, name=name, out_shape=jax.ShapeDtypeStruct((N_DEV * m_per, n_cols), block.dtype),
        in_specs=[VMEM_SPEC], out_specs=VMEM_SPEC,
        scratch_shapes=[pltpu.SemaphoreType.DMA((7,)), pltpu.SemaphoreType.DMA((7,)), pltpu.SemaphoreType.DMA],
        compiler_params=pltpu.CompilerParams(vmem_limit_bytes=VMEM_LIMIT_BYTES),
    )(block)
    return out.reshape(N_DEV, m_per, n_cols)


PACK_ROWS = 16


def _pack_rows(shape, width):
    return -(-math.prod(shape) // (width * PACK_ROWS)) * PACK_ROWS


def _pack(arrays, width):
    rows = []
    for a in arrays:
        flat = a.reshape(-1).astype(F32)
        n_rows = _pack_rows(a.shape, width)
        rows.append(jnp.pad(flat, (0, n_rows * width - flat.shape[0])).reshape(n_rows, width))
    return jnp.concatenate(rows, axis=0)


def _unpack(packed, shapes, width):
    out, r = [], 0
    for shp in shapes:
        size = math.prod(shp)
        n_rows = _pack_rows(shp, width)
        out.append(packed[r:r + n_rows].reshape(-1)[:size].reshape(shp))
        r += n_rows
    return out


def _rope_tables(S, C):
    rows = S // GRID_W
    axis_dim = HEAD_DIM // 2
    row = jnp.broadcast_to(jnp.arange(rows, dtype=F32)[:, None], (rows, GRID_W)).reshape(-1)
    col = jnp.broadcast_to(jnp.arange(GRID_W, dtype=F32)[None, :], (rows, GRID_W)).reshape(-1)
    inv_freq = ROPE_THETA ** (-jnp.arange(0, axis_dim, 2, dtype=F32) / axis_dim)
    ang = jnp.concatenate([row[:, None] * inv_freq, col[:, None] * inv_freq], axis=-1)
    cos, sin = jnp.cos(ang), jnp.sin(ang)
    cos2 = jnp.repeat(cos, 2, axis=-1)
    sin2 = jnp.stack([-sin, sin], axis=-1).reshape(S, HEAD_DIM)
    cos2 = jnp.concatenate([cos2, jnp.ones((C, HEAD_DIM), F32)], axis=0)
    sin2 = jnp.concatenate([sin2, jnp.zeros((C, HEAD_DIM), F32)], axis=0)
    return cos2, sin2


def kernel(x, c, ctx, c_ctx, w_mod, b_mod, norm_w, w_ffn1_in, w_ffn1_out, w_ffn2_in, w_ffn2_out, w_in, b_gate, q_norm_w, k_norm_w, gmlp_ln_w, gmlp_ln_b, w_spatial, b_spatial, w_branch_attn, w_branch_gmlp, w_out, final_norm_w, loss_target, m_c_ctx, m_w_mod, m_b_mod, m_norm_w, m_w_ffn1_in, m_w_ffn1_out, m_w_ffn2_in, m_w_ffn2_out, m_w_in, m_b_gate, m_q_norm_w, m_k_norm_w, m_gmlp_ln_w, m_gmlp_ln_b, m_w_spatial, m_b_spatial, m_w_branch_attn, m_w_branch_gmlp, m_w_out, m_final_norm_w, v_c_ctx, v_w_mod, v_b_mod, v_norm_w, v_w_ffn1_in, v_w_ffn1_out, v_w_ffn2_in, v_w_ffn2_out, v_w_in, v_b_gate, v_q_norm_w, v_k_norm_w, v_gmlp_ln_w, v_gmlp_ln_b, v_w_spatial, v_b_spatial, v_w_branch_attn, v_w_branch_gmlp, v_w_out, v_final_norm_w):
    _, S, D = x.shape
    C = ctx.shape[1]
    NTOK = S + C
    n_xt = S // ROW_TILE
    q_w, kv_w = N_Q_HEADS * HEAD_DIM, N_KV_HEADS * HEAD_DIM
    W = gmlp_ln_w.shape[1]
    v_end = q_w + 2 * kv_w
    gv_end = v_end + 2 * W
    scale = HEAD_DIM ** -0.5
    dev = 4 * lax.axis_index("x") + 2 * lax.axis_index("y") + lax.axis_index("c")
    shard = 2 * lax.axis_index("x") + lax.axis_index("y")

    c_all = _allgather_devices(jnp.pad(c, ((0, 7), (0, 0))), "gather_c")[:, 0, :]
    cvecs = jnp.concatenate([c_all, jnp.pad(c_ctx[None, :], ((0, 7), (0, 0)))], axis=0)
    w_mod_l = w_mod[0]
    n_modsh = w_mod_l.shape[1]
    mod_part = _mod_fwd(cvecs, w_mod_l, "mod_fwd")
    mod_all = _allgather_devices(mod_part, "gather_mod")[0::2]
    mod_full = jnp.transpose(mod_all, (1, 0, 2)).reshape(16, N_SHARDS * n_modsh) + b_mod
    mx = lax.dynamic_index_in_dim(mod_full, dev, 0, keepdims=False).reshape(N_MOD, D)
    mc = mod_full[8].reshape(N_MOD, D)
    mods = jnp.concatenate([mx, mc], axis=0).reshape(2 * N_MOD, 1, D)

    shard_of = lambda w: w[0].astype(BF16)
    rows_of = lambda g: g.reshape(-1, g.shape[-1])
    (wf1i,) = _comm_call(_gather_rider([shard_of(w_ffn1_in)]), "gather_ffn1_in")
    n_vsh = norm_w.shape[-1]
    nw_all = _allgather_devices(_pack([norm_w[0], b_gate[0]], n_vsh), "gather_vecs")[0::2]
    nw = jnp.transpose(nw_all[:, 0:3, :], (1, 0, 2)).reshape(3, D)
    bg = jnp.transpose(nw_all[:, PACK_ROWS:PACK_ROWS + 2, :], (1, 0, 2)).reshape(2, 1, D)
    nw0, nw1, nw2 = nw[0:1], nw[1:2], nw[2:3]

    cos2, sin2 = _rope_tables(S, C)
    b_sb = jnp.broadcast_to(b_spatial[0][:, :, None], (GMLP_GROUPS, CHUNK, W // GMLP_GROUPS))
    w_s = w_spatial[0]
    fw = final_norm_w[None, :]

    tok0 = jnp.concatenate([x[0], ctx[0]], axis=0)
    h1 = _normmod(tok0, mods, nw0, k_shift=0, k_scale=1, n_xt=n_xt, name="ffn1_norm")
    z1, (wf1o_g,) = _matmul(h1, wf1i, form="nn", b_shards=N_SHARDS, out_shards=2, name="ffn1_in",
                            rider=_gather_rider([shard_of(w_ffn1_out)]))
    wf1o = rows_of(wf1o_g)
    g1 = _swiglu_fwd(z1, "ffn1_act")
    y1, (win,) = _matmul(g1, wf1o, form="nn", name="ffn1_out", tn=1024, rider=_gather_rider([shard_of(w_in)]))
    tok1, h2 = _normmod(tok0, mods, nw1, k_shift=3, k_scale=4, n_xt=n_xt, name="mix_norm",
                        resid=(y1, 2, MACARON_WEIGHT))
    later = [w_branch_attn, w_branch_gmlp, w_ffn2_in]
    zx, (wba_g, wbg_g, wf2i) = _matmul(h2, win, form="nn", b_shards=N_SHARDS, name="mix_in",
                                       rider=_gather_rider([shard_of(w) for w in later]))
    wba, wbg = rows_of(wba_g), rows_of(wbg_g)
    qt, kt, vt = _qk_prep(zx, cos2, sin2, q_norm_w, k_norm_w, q_w=q_w, kv_w=kv_w, scale=scale, name="qk_prep")
    attn, lse = _flash_fwd(qt, kt, vt, S=S, name="attn_fwd")
    gm = _gmlp_fwd(zx, gmlp_ln_w, gmlp_ln_b, w_s, b_sb, S=S, col0=v_end, name="gmlp_fwd")
    a_br, (wo_g,) = _matmul(attn, wba, form="nn", name="branch_attn", tm=512, rider=_gather_rider([shard_of(w_out)]))
    wo = rows_of(wo_g)
    g_br = _matmul(gm, wbg, form="nn", name="branch_gmlp", tm=512)
    t_mix = _merge_fwd(zx, a_br, g_br, bg, S=S, col0=gv_end, name="merge_fwd")
    y_mix = _matmul(t_mix, wo, form="nn", name="mix_out", tm=512)
    x2, h3 = _normmod(tok1, mods, nw2, k_shift=6, k_scale=7, n_xt=n_xt, name="ffn2_norm", resid=(y_mix, 5, None),
                      rows=S)
    z2, (wf2o_g,) = _matmul(h3, wf2i, form="nn", b_shards=N_SHARDS, out_shards=2, name="ffn2_in",
                            rider=_gather_rider([shard_of(w_ffn2_out)]))
    wf2o = rows_of(wf2o_g)
    g2 = _swiglu_fwd(z2, "ffn2_act")
    y2 = _matmul(g2, wf2o, form="nn", name="ffn2_out", tn=1024)
    x3 = _resid_only(x2, y2, mods, k_gate=8, name="ffn2_resid")

    dx3, loss_row, dfw = _final_loss(x3, loss_target[0], fw, "loss")

    dy2, dgate8 = _resid_bwd(dx3, y2, mods, k_gate=8, weight=MACARON_WEIGHT, n_xt=n_xt, name="ffn2_resid_bwd")
    dg2 = _matmul(dy2, wf2o, form="nt", name="ffn2_out_dgrad")
    gw_f2o = _matmul(g2, dy2, form="tn", out_dtype=BF16, name="ffn2_out_wgrad", tn=1024)
    dz2 = _swiglu_bwd(z2, dg2, "ffn2_act_bwd")
    dh3 = _matmul(dz2, wf2i, form="nt", a_shards=2, b_shards=N_SHARDS, name="ffn2_in_dgrad", tn=1024)
    gw_f2i = _matmul(h3, dz2, form="tn", out_dtype=BF16, b_shards=2, out_shards=N_SHARDS, name="ffn2_in_wgrad")
    dx2, dsh6, dsc7, dnw2 = _normmod_bwd(dh3, x2, dx3, mods, nw2, k_scale=7, n_xt=n_xt, name="ffn2_norm_bwd")

    dy_mix, dgate5 = _resid_bwd(dx2, y_mix, mods, k_gate=5, weight=None, n_xt=n_xt, name="mix_resid_bwd")
    dt = _matmul(dy_mix, wo, form="nt", name="mix_out_dgrad", tm=512)
    gw_wo = _matmul(t_mix, dy_mix, form="tn", out_dtype=BF16, name="mix_out_wgrad", tn=1024)
    d_abr, d_gbr, dl0, dl1, dbg0, dbg1 = _merge_bwd(dt, zx, a_br, g_br, bg, S=S, col0=gv_end, name="merge_bwd")
    d_attn = _matmul(d_abr, wba, form="nt", name="branch_attn_dgrad", tm=512)
    gw_wba = _matmul(attn, d_abr, form="tn", out_dtype=BF16, name="branch_attn_wgrad", tn=1024)
    d_gm = _matmul(d_gbr, wbg, form="nt", name="branch_gmlp_dgrad", tm=512)
    gw_wbg = _matmul(gm, d_gbr, form="tn", out_dtype=BF16, name="branch_gmlp_wgrad", tn=1024)
    dz_gm, dws, dbs_wide, dlnw, dlnb = _gmlp_bwd(zx, d_gm, gmlp_ln_w, gmlp_ln_b, w_s, b_sb, S=S, col0=v_end,
                                                 name="gmlp_bwd")
    delta = _attn_delta(d_attn, attn, G=N_KV_HEADS, name="attn_delta")
    dq, dk, dv = _flash_bwd(qt, kt, vt, d_attn, lse, delta, S=S, name="attn_bwd")
    dz_qkv, dqw, dkw = _qk_prep_bwd(zx, dq, dk, dv, cos2, sin2, q_norm_w, k_norm_w, q_w=q_w, kv_w=kv_w,
                                    scale=scale, n_xt=n_xt, name="qk_prep_bwd")
    ctx_pad = ((0, C), (0, 0))
    dzx = jnp.concatenate([dz_qkv, jnp.pad(dz_gm, ctx_pad), jnp.pad(dl0, ctx_pad), jnp.pad(dl1, ctx_pad)], axis=1)
    slabs_of = lambda g: g.reshape(N_SHARDS, g.shape[0] // N_SHARDS, g.shape[1])
    dh2, (rc_f2i,) = _matmul(dzx, win, form="nt", b_shards=N_SHARDS, name="mix_in_dgrad", tn=1024,
                             rider=_scatter_rider([gw_f2i]))
    gw_win, (rc_f2o, rc_wo, rc_wba, rc_wbg) = _matmul(
        h2, dzx, form="tn", out_dtype=BF16, out_shards=N_SHARDS, name="mix_in_wgrad",
        rider=_scatter_rider([slabs_of(g) for g in (gw_f2o, gw_wo, gw_wba, gw_wbg)]))
    dtok1, dsh3, dsc4, dnw1 = _normmod_bwd(dh2, tok1, dx2, mods, nw1, k_scale=4, n_xt=n_xt, name="mix_norm_bwd")

    dy1, dgate2 = _resid_bwd(dtok1, y1, mods, k_gate=2, weight=MACARON_WEIGHT, n_xt=n_xt, name="ffn1_resid_bwd")
    dg1 = _matmul(dy1, wf1o, form="nt", name="ffn1_out_dgrad")
    gw_f1o, (rc_win,) = _matmul(g1, dy1, form="tn", out_dtype=BF16, name="ffn1_out_wgrad", tn=1024,
                                rider=_scatter_rider([gw_win]))
    dz1 = _swiglu_bwd(z1, dg1, "ffn1_act_bwd")
    gw_f1i, (rc_f1o,) = _matmul(h1, dz1, form="tn", out_dtype=BF16, b_shards=2, out_shards=N_SHARDS, name="ffn1_in_wgrad",
                                rider=_scatter_rider([slabs_of(gw_f1o)]))
    dh1, (rc_f1i,) = _matmul(dz1, wf1i, form="nt", a_shards=2, b_shards=N_SHARDS, name="ffn1_in_dgrad", tn=1024,
                             rider=_scatter_rider([gw_f1i]))
    dtok0, dsh0, dsc1, dnw0 = _normmod_bwd(dh1, tok0, dtok1, mods, nw0, k_scale=1, n_xt=n_xt, name="ffn1_norm_bwd")
    grad_x = dtok0[:S][None]

    big_w = [w_ffn1_in, w_ffn2_in, w_in, w_ffn1_out, w_ffn2_out, w_branch_attn, w_branch_gmlp, w_out]
    big_m = [m_w_ffn1_in, m_w_ffn2_in, m_w_in, m_w_ffn1_out, m_w_ffn2_out, m_w_branch_attn, m_w_branch_gmlp, m_w_out]
    big_v = [v_w_ffn1_in, v_w_ffn2_in, v_w_in, v_w_ffn1_out, v_w_ffn2_out, v_w_branch_attn, v_w_branch_gmlp, v_w_out]
    big_names = ["w_ffn1_in", "w_ffn2_in", "w_in", "w_ffn1_out", "w_ffn2_out", "w_branch_attn", "w_branch_gmlp", "w_out"]
    received = [rc_f1i, rc_f2i, rc_win, rc_f1o, rc_f2o, rc_wba, rc_wbg, rc_wo]
    plane_sums = [_sum_slabs(r, "plane_sum_" + nm) for r, nm in zip(received, big_names)]
    sibling_sums = _sibling_exchange(plane_sums, "sibling_grads")
    big_out = {}
    for nm, w_, m_, v_, pa, pb in zip(big_names, big_w, big_m, big_v, plane_sums, sibling_sums):
        res = _adamw(w_[0], m_[0], v_[0], [pa, pb], "adamw_" + nm)
        big_out[nm] = [r[None] for r in res]

    zeros9 = jnp.zeros((1, D), F32)
    dmx = jnp.concatenate([dsh0[0], dsc1[0], dgate2[0], dsh3[0], dsc4[0], dgate5[0], dsh6[0], dsc7[0], dgate8[0]], axis=0)
    dmc = jnp.concatenate([dsh0[1], dsc1[1], dgate2[1], dsh3[1], dsc4[1], zeros9, zeros9, zeros9, zeros9], axis=0)
    dbs = jnp.sum(dbs_wide, axis=-1)
    parts = [dmx, dmc, jnp.concatenate([dnw0, dnw1, dnw2], axis=0), jnp.concatenate([dbg0, dbg1], axis=0),
             dqw, dkw, dlnw, dlnb, dws, dbs, dfw, loss_row[:, :1]]
    part_shapes = [p.shape for p in parts]
    small_all = _allgather_devices(_pack(parts, D), "gather_small")
    dmx_all = small_all[:, 0:N_MOD, :].reshape(N_DEV, N_MOD * D)
    small_sum = small_all[0]
    for d in range(1, N_DEV):
        small_sum = small_sum + small_all[d]
    (_, dmc_sum, g_nw, g_bg, g_qw, g_kw, g_lnw, g_lnb, g_ws, g_bs, g_fw, loss_sum) = _unpack(small_sum, part_shapes, D)
    loss = loss_sum[0, 0]
    dmx_sum = small_sum[0:N_MOD].reshape(1, N_MOD * D)
    g_b_mod = dmx_sum + dmc_sum.reshape(1, N_MOD * D)
    dm_rows = jnp.concatenate([dmx_all, jnp.pad(dmc_sum.reshape(1, N_MOD * D), ((0, 7), (0, 0)))], axis=0)
    dm_sh = lax.dynamic_slice_in_dim(dm_rows, shard * n_modsh, n_modsh, axis=1)
    g_w_mod = _mod_wgrad(cvecs, dm_sh, "mod_wgrad")
    dsc_part = _mod_dgrad(dm_sh, w_mod_l, "mod_dgrad")
    dsc_all = _allgather_devices(dsc_part, "gather_dsilu")[0::2, 8, :]
    dscc = ((dsc_all[0] + dsc_all[1]) + dsc_all[2]) + dsc_all[3]
    sg = _sigmoid(c_ctx)
    g_c_ctx = dscc * (sg * (1.0 + c_ctx * (1.0 - sg)))
    g_norm_w = lax.dynamic_slice_in_dim(g_nw, shard * n_vsh, n_vsh, axis=1)[None]
    g_b_gate = lax.dynamic_slice_in_dim(g_bg, shard * n_vsh, n_vsh, axis=1)[None]

    w_mod_out = [r[None] for r in _adamw(w_mod_l, m_w_mod[0], v_w_mod[0], [g_w_mod], "adamw_w_mod")]

    small_names = ["c_ctx", "b_mod", "norm_w", "b_gate", "q_norm_w", "k_norm_w", "gmlp_ln_w", "gmlp_ln_b",
                   "w_spatial", "b_spatial", "final_norm_w"]
    small_w = [c_ctx, b_mod, norm_w, b_gate, q_norm_w, k_norm_w, gmlp_ln_w, gmlp_ln_b, w_spatial, b_spatial, final_norm_w]
    small_m = [m_c_ctx, m_b_mod, m_norm_w, m_b_gate, m_q_norm_w, m_k_norm_w, m_gmlp_ln_w, m_gmlp_ln_b, m_w_spatial,
               m_b_spatial, m_final_norm_w]
    small_v = [v_c_ctx, v_b_mod, v_norm_w, v_b_gate, v_q_norm_w, v_k_norm_w, v_gmlp_ln_w, v_gmlp_ln_b, v_w_spatial,
               v_b_spatial, v_final_norm_w]
    small_g = [g_c_ctx, g_b_mod, g_norm_w, g_b_gate, g_qw, g_kw, g_lnw, g_lnb, g_ws, g_bs, g_fw]
    small_shapes = [w_.shape for w_ in small_w]
    packed = [_pack(group, D) for group in (small_w, small_m, small_v, small_g)]
    small_res = [_unpack(r, small_shapes, D) for r in _adamw(*packed[:3], [packed[3]], "adamw_small")]
    small_out = {nm: [small_res[t][i] for t in range(4)] for i, nm in enumerate(small_names)}

    order = ["c_ctx", "w_mod", "b_mod", "norm_w", "w_ffn1_in", "w_ffn1_out", "w_ffn2_in", "w_ffn2_out", "w_in", "b_gate",
             "q_norm_w", "k_norm_w", "gmlp_ln_w", "gmlp_ln_b", "w_spatial", "b_spatial", "w_branch_attn", "w_branch_gmlp",
             "w_out", "final_norm_w"]
    results = {**big_out, **small_out, "w_mod": w_mod_out}
    outs = [loss, grad_x]
    for t in range(4):
        outs += [results[nm][t] for nm in order]
    return tuple(outs)


def _resid_only(xp, y_in, mods, *, k_gate, name):
    M, D = xp.shape
    row = pl.BlockSpec((ROW_TILE, D), lambda i: (i, 0))

    def body(xp_ref, y_ref, g_ref, x_ref):
        x_ref[...] = xp_ref[...] + (MACARON_WEIGHT * g_ref[...]) * y_ref[...]

    return pl.pallas_call(
        body, name=name, grid=(M // ROW_TILE,), out_shape=jax.ShapeDtypeStruct((M, D), F32),
        in_specs=[row, row, pl.BlockSpec((None, 1, D), lambda i: (k_gate, 0, 0))], out_specs=row,
        compiler_params=_params("parallel"),
    )(xp, y_in, mods)
```

```python
import functools
import math

import jax
import jax.numpy as jnp
from jax import lax
from jax.experimental import pallas as pl
from jax.experimental.pallas import tpu as pltpu

F32 = jnp.float32
BF16 = jnp.bfloat16
MESH = pl.DeviceIdType.MESH
HBM_SPEC = pl.BlockSpec(memory_space=pltpu.HBM)
VMEM_SPEC = pl.BlockSpec(memory_space=pltpu.VMEM)

HEAD_DIM = 128
N_Q_HEADS = 16
N_KV_HEADS = 4
GMLP_GROUPS = 16
CHUNK = 128
GRID_W = 64
ROPE_THETA = 10000.0
EPS = 1e-6
MACARON_WEIGHT = 0.5
N_MOD = 9

ADAM_LR = 0.001
ADAM_B1 = 0.9
ADAM_B2 = 0.999
ADAM_EPS = 1e-08
ADAM_WD = 0.01
ADAM_STEP = 10

N_SHARDS = 4
N_DEV = 8
ROW_TILE = 256
KEY_TILES = (2816, 768, 512, 256, 128)
FWD_KEY_TILES = (8448,) + KEY_TILES
LANES = 128
VMEM_LIMIT_BYTES = 48 * 1024 * 1024
ADAM_TILE_BYTES = 1 << 20

NN = (((1,), (0,)), ((), ()))
NT = (((1,), (1,)), ((), ()))
TN = (((0,), (0,)), ((), ()))


def _pick(n, cands):
    for t in cands:
        if t <= n and n % t == 0:
            return t
    raise ValueError(f"no tile for {n} among {cands}")


def _params(*sem):
    return pltpu.CompilerParams(dimension_semantics=sem or None, vmem_limit_bytes=VMEM_LIMIT_BYTES)


def _sigmoid(x):
    return 1.0 / (1.0 + jnp.exp(-x))


def _matmul(a, b, *, form, name, out_dtype=F32, a_shards=1, b_shards=1, out_shards=1, tm=None, tn=None, tk=None,
            order="ji", rider=None):
    if form == "nn":
        M, K = a.shape
        N = b.shape[-1] * b_shards
    elif form == "nt":
        M, K = a.shape[-2], a.shape[-1] * a_shards
        N = b.shape[-2]
    else:
        K, M = a.shape
        N = b.shape[-1] * b_shards
    n_b = N // b_shards if form in ("nn", "tn") else N
    n_o = N // out_shards
    k_a = K // a_shards
    k_b = K // b_shards if form == "nt" else K
    tm = _pick(M, ((tm,) if tm else ()) + (1024, 768, 512, 256, 128, 64, 32, 16))
    tn = _pick(math.gcd(n_b, n_o), ((tn,) if tn else ()) + (1408, 1024, 512, 256, 128))
    tk = tk or _pick(math.gcd(k_a, k_b),
                     (2816, 2048, 1024, 768, 512, 256, 128) if form == "tn" else
                     (2816, 2048, 1408, 1024, 768, 512, 256, 128))
    nk = K // tk
    nbb, nbo = n_b // tn, n_o // tn
    ka, kb = k_a // tk, k_b // tk
    dims = {"nn": NN, "nt": NT, "tn": TN}[form]

    def ij(g0, g1):
        return (g0, g1) if order == "ij" else (g1, g0)

    def a_map(g0, g1, k):
        i, _ = ij(g0, g1)
        if form == "tn":
            return (k, i)
        return (k // ka, i, k % ka) if a_shards > 1 else (i, k)

    def b_map(g0, g1, k):
        _, j = ij(g0, g1)
        if form == "nt":
            return (k // kb, j, k % kb) if b_shards > 1 else (j, k)
        return (j // nbb, k, j % nbb) if b_shards > 1 else (k, j)

    def o_map(g0, g1, k):
        i, j = ij(g0, g1)
        return (j // nbo, i, j % nbo) if out_shards > 1 else (i, j)

    a_block = (tk, tm) if form == "tn" else (tm, tk)
    if a_shards > 1:
        a_block = (None,) + a_block
    b_block = (tk, tn) if form in ("nn", "tn") else (tn, tk)
    if b_shards > 1:
        b_block = (None,) + b_block
    o_block = (None, tm, tn) if out_shards > 1 else (tm, tn)
    o_shape = (out_shards, M, n_o) if out_shards > 1 else (M, N)

    n_i, n_j = M // tm, N // tn
    grid = (n_i, n_j, nk) if order == "ij" else (n_j, n_i, nk)
    n_ride = len(rider.arrays) if rider else 0
    n_acc = 1 if nk > 1 else 0

    def body(*refs):
        a_ref, b_ref = refs[:2]
        ride_in = refs[2:2 + n_ride]
        o_ref = refs[2 + n_ride]
        ride_out = refs[3 + n_ride:3 + 2 * n_ride]
        scratch = refs[3 + 2 * n_ride:]
        ride_sems = scratch[n_acc:]
        pid = [pl.program_id(d) for d in range(3)]
        if rider:
            @pl.when((pid[0] == 0) & (pid[1] == 0) & (pid[2] == 0))
            def _():
                rider.start(ride_in, ride_out, ride_sems)

        part = lax.dot_general(a_ref[...].astype(BF16), b_ref[...].astype(BF16), dims, preferred_element_type=F32)
        if nk == 1:
            o_ref[...] = part.astype(o_ref.dtype)
        else:
            acc = scratch[0]
            k = pid[2]

            @pl.when(k == 0)
            def _():
                acc[...] = part

            @pl.when(k > 0)
            def _():
                acc[...] += part

            @pl.when(k == nk - 1)
            def _():
                o_ref[...] = acc[...].astype(o_ref.dtype)

        if rider:
            @pl.when((pid[0] == grid[0] - 1) & (pid[1] == grid[1] - 1) & (pid[2] == nk - 1))
            def _():
                rider.finish(ride_in, ride_out, ride_sems)

    main_shape = jax.ShapeDtypeStruct(o_shape, out_dtype)
    res = pl.pallas_call(
        body,
        name=name,
        out_shape=(main_shape, *rider.out_shapes) if rider else main_shape,
        grid=grid,
        in_specs=[pl.BlockSpec(a_block, a_map), pl.BlockSpec(b_block, b_map)] + [HBM_SPEC] * n_ride,
        out_specs=(pl.BlockSpec(o_block, o_map), *[HBM_SPEC] * n_ride) if rider else pl.BlockSpec(o_block, o_map),
        scratch_shapes=([pltpu.VMEM((tm, tn), F32)] if nk > 1 else []) + (rider.sems if rider else []),
        compiler_params=_params(*(("arbitrary",) * 3 if rider else ("parallel", "parallel", "arbitrary"))),
    )(a, b, *(rider.arrays if rider else ()))
    return (res[0], list(res[1:])) if rider else res


def _type_of(i, n_xt, n_tiles):
    return jnp.where(i >= n_xt, 1, 0) if n_tiles > n_xt else 0


def _mod_spec(D, k, n_xt, n_tiles):
    return pl.BlockSpec((None, 1, D), lambda i: (_type_of(i, n_xt, n_tiles) * N_MOD + k, 0, 0))


def _acc_spec(D, n_xt, n_tiles):
    return pl.BlockSpec((None, 1, D), lambda i: (_type_of(i, n_xt, n_tiles), 0, 0))


def _accumulate(ref, value, first):
    @pl.when(first)
    def _():
        ref[...] = value

    @pl.when(jnp.logical_not(first))
    def _():
        ref[...] += value


def _normmod(xp, mods, nw, *, k_shift, k_scale, n_xt, name, resid=None, rows=None):
    M, D = rows or xp.shape[0], xp.shape[1]
    n_tiles = M // ROW_TILE
    row = pl.BlockSpec((ROW_TILE, D), lambda i: (i, 0))
    vec = pl.BlockSpec((1, D), lambda i: (0, 0))
    mod = functools.partial(_mod_spec, D, n_xt=n_xt, n_tiles=n_tiles)

    if resid is not None:
        y_in, k_gate, weight = resid

        def body(xp_ref, y_ref, g_ref, nw_ref, sh_ref, sc_ref, x_ref, h_ref):
            g = g_ref[...] if weight is None else weight * g_ref[...]
            x = xp_ref[...] + g * y_ref[...]
            x_ref[...] = x
            r = lax.rsqrt(jnp.mean(x * x, axis=-1, keepdims=True) + EPS)
            y = (x * r) * nw_ref[...]
            h_ref[...] = (y * (1.0 + sc_ref[...]) + sh_ref[...]).astype(BF16)

        return pl.pallas_call(
            body, name=name, grid=(n_tiles,),
            out_shape=(jax.ShapeDtypeStruct((M, D), F32), jax.ShapeDtypeStruct((M, D), BF16)),
            in_specs=[row, row, mod(k_gate), vec, mod(k_shift), mod(k_scale)],
            out_specs=(row, row), compiler_params=_params("parallel"),
        )(xp, y_in, mods, nw, mods, mods)

    def body(xp_ref, nw_ref, sh_ref, sc_ref, h_ref):
        x = xp_ref[...]
        r = lax.rsqrt(jnp.mean(x * x, axis=-1, keepdims=True) + EPS)
        y = (x * r) * nw_ref[...]
        h_ref[...] = (y * (1.0 + sc_ref[...]) + sh_ref[...]).astype(BF16)

    return pl.pallas_call(
        body, name=name, grid=(n_tiles,), out_shape=jax.ShapeDtypeStruct((M, D), BF16),
        in_specs=[row, vec, mod(k_shift), mod(k_scale)], out_specs=row, compiler_params=_params("parallel"),
    )(xp, nw, mods, mods)


def _resid_bwd(dxo, y_in, mods, *, k_gate, weight, n_xt, name):
    M, D = dxo.shape
    n_tiles = M // ROW_TILE
    n_types = 2 if n_tiles > n_xt else 1
    row = pl.BlockSpec((ROW_TILE, D), lambda i: (i, 0))

    def body(dxo_ref, y_ref, g_ref, dy_ref, dg_ref):
        i = pl.program_id(0)
        dxo_t = dxo_ref[...]
        g = g_ref[...] if weight is None else weight * g_ref[...]
        yw = y_ref[...] if weight is None else weight * y_ref[...]
        dy_ref[...] = (dxo_t * g).astype(BF16)
        _accumulate(dg_ref, jnp.sum(dxo_t * yw, axis=0, keepdims=True), (i == 0) | (i == n_xt))

    return pl.pallas_call(
        body, name=name, grid=(n_tiles,),
        out_shape=(jax.ShapeDtypeStruct((M, D), BF16), jax.ShapeDtypeStruct((n_types, 1, D), F32)),
        in_specs=[row, row, _mod_spec(D, k_gate, n_xt, n_tiles)],
        out_specs=(row, _acc_spec(D, n_xt, n_tiles)), compiler_params=_params("arbitrary"),
    )(dxo, y_in, mods)


def _normmod_bwd(dh, x_in, dxo, mods, nw, *, k_scale, n_xt, name):
    M, D = dh.shape
    n_tiles = M // ROW_TILE
    n_types = 2 if n_tiles > n_xt else 1
    dxo_tiles = dxo.shape[0] // ROW_TILE
    row = pl.BlockSpec((ROW_TILE, D), lambda i: (i, 0))
    dxo_row = pl.BlockSpec((ROW_TILE, D), lambda i: (jnp.minimum(i, dxo_tiles - 1), 0))
    vec = pl.BlockSpec((1, D), lambda i: (0, 0))

    def body(dh_ref, x_ref, dxo_ref, nw_ref, sc_ref, dx_ref, dsh_ref, dsc_ref, dnw_ref):
        i = pl.program_id(0)
        x = x_ref[...]
        dh_t = dh_ref[...]
        w = nw_ref[...]
        r = lax.rsqrt(jnp.mean(x * x, axis=-1, keepdims=True) + EPS)
        n = x * r
        dy = dh_t * (1.0 + sc_ref[...])
        dn = dy * w
        dx = r * (dn - n * jnp.mean(dn * n, axis=-1, keepdims=True))
        if dxo_tiles < n_tiles:
            dx_ref[...] = dx + jnp.where(i < dxo_tiles, dxo_ref[...], 0.0)
        else:
            dx_ref[...] = dx + dxo_ref[...]
        first = (i == 0) | (i == n_xt)
        _accumulate(dsh_ref, jnp.sum(dh_t, axis=0, keepdims=True), first)
        _accumulate(dsc_ref, jnp.sum(dh_t * (n * w), axis=0, keepdims=True), first)
        _accumulate(dnw_ref, jnp.sum(dy * n, axis=0, keepdims=True), i == 0)

    acc = _acc_spec(D, n_xt, n_tiles)
    return pl.pallas_call(
        body, name=name, grid=(n_tiles,),
        out_shape=(jax.ShapeDtypeStruct((M, D), F32), jax.ShapeDtypeStruct((n_types, 1, D), F32),
                   jax.ShapeDtypeStruct((n_types, 1, D), F32), jax.ShapeDtypeStruct((1, D), F32)),
        in_specs=[row, row, dxo_row, vec, _mod_spec(D, k_scale, n_xt, n_tiles)],
        out_specs=(row, acc, acc, vec), compiler_params=_params("arbitrary"),
    )(dh, x_in, dxo, nw, mods)


def _swiglu_fwd(z, name):
    _, M, Fh = z.shape
    tf = _pick(Fh, (1408, 1024, 512, 256, 128))

    def body(z_ref, g_ref):
        a = z_ref[0]
        g_ref[...] = ((a * _sigmoid(a)) * z_ref[1]).astype(BF16)

    return pl.pallas_call(
        body, name=name, grid=(M // ROW_TILE, Fh // tf), out_shape=jax.ShapeDtypeStruct((M, Fh), BF16),
        in_specs=[pl.BlockSpec((2, ROW_TILE, tf), lambda i, j: (0, i, j))],
        out_specs=pl.BlockSpec((ROW_TILE, tf), lambda i, j: (i, j)), compiler_params=_params("parallel", "parallel"),
    )(z)


def _swiglu_bwd(z, dg, name):
    _, M, Fh = z.shape
    tf = _pick(Fh, (1408, 1024, 512, 256, 128))

    def body(z_ref, dg_ref, dz_ref):
        a = z_ref[0]
        sig = _sigmoid(a)
        dg_t = dg_ref[...]
        dz_ref[0] = (dg_t * z_ref[1] * (sig * (1.0 + a * (1.0 - sig)))).astype(BF16)
        dz_ref[1] = (dg_t * (a * sig)).astype(BF16)

    pair = pl.BlockSpec((2, ROW_TILE, tf), lambda i, j: (0, i, j))
    return pl.pallas_call(
        body, name=name, grid=(M // ROW_TILE, Fh // tf), out_shape=jax.ShapeDtypeStruct((2, M, Fh), BF16),
        in_specs=[pair, pl.BlockSpec((ROW_TILE, tf), lambda i, j: (i, j))],
        out_specs=pair, compiler_params=_params("parallel", "parallel"),
    )(z, dg)


def _final_loss(x3, target, fw, name):
    S, D = x3.shape
    row = pl.BlockSpec((ROW_TILE, D), lambda i: (i, 0))
    vec = pl.BlockSpec((1, D), lambda i: (0, 0))

    def body(x_ref, t_ref, fw_ref, dx_ref, loss_ref, dfw_ref):
        i = pl.program_id(0)
        x = x_ref[...]
        w = fw_ref[...]
        r = lax.rsqrt(jnp.mean(x * x, axis=-1, keepdims=True) + EPS)
        n = x * r
        err = n * w - t_ref[...]
        tile_loss = 0.5 * jnp.sum(jnp.mean(err * err, axis=-1, keepdims=True), axis=0, keepdims=True)
        dout = err / D
        dn = dout * w
        dx_ref[...] = r * (dn - n * jnp.mean(dn * n, axis=-1, keepdims=True))
        _accumulate(loss_ref, jnp.broadcast_to(tile_loss, (1, LANES)), i == 0)
        _accumulate(dfw_ref, jnp.sum(dout * n, axis=0, keepdims=True), i == 0)

    return pl.pallas_call(
        body, name=name, grid=(S // ROW_TILE,),
        out_shape=(jax.ShapeDtypeStruct((S, D), F32), jax.ShapeDtypeStruct((1, LANES), F32),
                   jax.ShapeDtypeStruct((1, D), F32)),
        in_specs=[row, row, vec],
        out_specs=(row, pl.BlockSpec((1, LANES), lambda i: (0, 0)), vec), compiler_params=_params("arbitrary"),
    )(x3, target, fw)


def _pair_swap(v):
    lane = lax.broadcasted_iota(jnp.int32, v.shape, v.ndim - 1)
    from_next = pltpu.roll(lane, 1, v.ndim - 1) == (lane ^ 1)
    return jnp.where(from_next, pltpu.roll(v, 1, v.ndim - 1), pltpu.roll(v, LANES - 1, v.ndim - 1))


def _qk_prep(zx, cos2, sin2, qw, kw, *, q_w, kv_w, scale, name):
    M = zx.shape[0]
    nqh, nkh = q_w // HEAD_DIM, kv_w // HEAD_DIM
    row = lambda w, c: pl.BlockSpec((ROW_TILE, w), lambda i: (i, c))
    vec = pl.BlockSpec((1, HEAD_DIM), lambda i: (0, 0))

    def rot(z, w, cos_t, sin_t):
        r = lax.rsqrt(jnp.mean(z * z, axis=-1, keepdims=True) + EPS)
        y = (z * r) * w
        return y * cos_t + _pair_swap(y) * sin_t

    def body(zq_ref, zk_ref, zv_ref, cos_ref, sin_ref, qw_ref, kw_ref, q_ref, k_ref, v_ref):
        cos_t, sin_t = cos_ref[...], sin_ref[...]
        for h in range(nqh):
            hs = slice(h * HEAD_DIM, (h + 1) * HEAD_DIM)
            q_ref[:, hs] = (rot(zq_ref[:, hs], qw_ref[...], cos_t, sin_t) * scale).astype(BF16)
        for h in range(nkh):
            hs = slice(h * HEAD_DIM, (h + 1) * HEAD_DIM)
            k_ref[:, hs] = rot(zk_ref[:, hs], kw_ref[...], cos_t, sin_t).astype(BF16)
        v_ref[...] = zv_ref[...].astype(BF16)

    return pl.pallas_call(
        body, name=name, grid=(M // ROW_TILE,),
        out_shape=(jax.ShapeDtypeStruct((M, q_w), BF16), jax.ShapeDtypeStruct((M, kv_w), BF16),
                   jax.ShapeDtypeStruct((M, kv_w), BF16)),
        in_specs=[row(q_w, 0), row(kv_w, q_w // kv_w), row(kv_w, q_w // kv_w + 1), row(HEAD_DIM, 0), row(HEAD_DIM, 0),
                  vec, vec],
        out_specs=(row(q_w, 0), row(kv_w, 0), row(kv_w, 0)), compiler_params=_params("parallel"),
    )(zx, zx, zx, cos2, sin2, qw, kw)


def _qk_prep_bwd(zx, dq, dk, dv, cos2, sin2, qw, kw, *, q_w, kv_w, scale, n_xt, name):
    M = zx.shape[0]
    nqh, nkh = q_w // HEAD_DIM, kv_w // HEAD_DIM
    dq_tiles = dq.shape[0] // ROW_TILE
    row = lambda w, c: pl.BlockSpec((ROW_TILE, w), lambda i: (i, c))
    vec = pl.BlockSpec((1, HEAD_DIM), lambda i: (0, 0))

    def unrot(z, d, w, cos_t, sin_t):
        r = lax.rsqrt(jnp.mean(z * z, axis=-1, keepdims=True) + EPS)
        n = z * r
        dy = d * cos_t - _pair_swap(d) * sin_t
        dn = dy * w
        dz = r * (dn - n * jnp.mean(dn * n, axis=-1, keepdims=True))
        return dz, jnp.sum(dy * n, axis=0, keepdims=True)

    def body(zq_ref, zk_ref, dq_ref, dk_ref, dv_ref, cos_ref, sin_ref, qw_ref, kw_ref, dz_ref, dqw_ref, dkw_ref):
        i = pl.program_id(0)
        cos_t, sin_t = cos_ref[...], sin_ref[...]
        is_x = i < n_xt
        dqw = jnp.zeros((1, HEAD_DIM), F32)
        dkw = jnp.zeros((1, HEAD_DIM), F32)
        for h in range(nqh):
            hs = slice(h * HEAD_DIM, (h + 1) * HEAD_DIM)
            d = jnp.where(is_x, dq_ref[:, hs], 0.0) * scale
            dz, dw = unrot(zq_ref[:, hs], d, qw_ref[...], cos_t, sin_t)
            dz_ref[:, hs] = dz.astype(BF16)
            dqw = dqw + dw
        for h in range(nkh):
            hs = slice(h * HEAD_DIM, (h + 1) * HEAD_DIM)
            dz, dw = unrot(zk_ref[:, hs], dk_ref[:, hs], kw_ref[...], cos_t, sin_t)
            dz_ref[:, q_w + h * HEAD_DIM:q_w + (h + 1) * HEAD_DIM] = dz.astype(BF16)
            dkw = dkw + dw
        dz_ref[:, q_w + kv_w:] = dv_ref[...].astype(BF16)
        _accumulate(dqw_ref, dqw, i == 0)
        _accumulate(dkw_ref, dkw, i == 0)

    return pl.pallas_call(
        body, name=name, grid=(M // ROW_TILE,),
        out_shape=(jax.ShapeDtypeStruct((M, q_w + 2 * kv_w), BF16), jax.ShapeDtypeStruct((1, HEAD_DIM), F32),
                   jax.ShapeDtypeStruct((1, HEAD_DIM), F32)),
        in_specs=[row(q_w, 0), row(kv_w, q_w // kv_w),
                  pl.BlockSpec((ROW_TILE, q_w), lambda i: (jnp.minimum(i, dq_tiles - 1), 0)),
                  row(kv_w, 0), row(kv_w, 0), row(HEAD_DIM, 0), row(HEAD_DIM, 0), vec, vec],
        out_specs=(row(q_w + 2 * kv_w, 0), vec, vec), compiler_params=_params("arbitrary"),
    )(zx, zx, dq, dk, dv, cos2, sin2, qw, kw)


def _lane_pick(tile, h):
    lane = lax.broadcasted_iota(jnp.int32, tile.shape, 1)
    return jnp.sum(jnp.where(lane == h, tile, 0.0), axis=-1, keepdims=True)


def _flash_fwd(q, k, v, *, S, name):
    NK, kv_w = k.shape
    G = kv_w // HEAD_DIM
    qpk = q.shape[1] // kv_w
    gw = qpk * HEAD_DIM
    tq = _pick(S, (512, 256, 128))
    tk = _pick(NK, FWD_KEY_TILES)
    nk = NK // tk

    def body(q_ref, k_ref, v_ref, o_ref, lse_ref, m_s, l_s, acc_s):
        ki = pl.program_id(2)

        @pl.when(ki == 0)
        def _():
            m_s[...] = jnp.full(m_s.shape, -1e30, F32)
            l_s[...] = jnp.zeros(l_s.shape, F32)
            acc_s[...] = jnp.zeros(acc_s.shape, F32)

        k_t, v_t = k_ref[...], v_ref[...]
        for h in range(qpk):
            s = lax.dot_general(q_ref[:, h * HEAD_DIM:(h + 1) * HEAD_DIM], k_t, NT, preferred_element_type=F32)
            m_prev = m_s[h]
            m_new = jnp.maximum(m_prev, jnp.max(s, axis=-1, keepdims=True))
            alpha = jnp.exp(m_prev - m_new)
            p = jnp.exp(s - m_new)
            l_s[h] = alpha * l_s[h] + jnp.sum(p, axis=-1, keepdims=True)
            acc_s[h] = alpha * acc_s[h] + lax.dot_general(p.astype(BF16), v_t, NN, preferred_element_type=F32)
            m_s[h] = m_new

        @pl.when(ki == nk - 1)
        def _():
            lane = lax.broadcasted_iota(jnp.int32, (tq, LANES), 1)
            lse = jnp.zeros((tq, LANES), F32)
            for h in range(qpk):
                l = l_s[h]
                o_ref[:, h * HEAD_DIM:(h + 1) * HEAD_DIM] = acc_s[h] / l
                lse = jnp.where(lane == h, m_s[h] + jnp.log(l), lse)
            lse_ref[...] = lse

    return pl.pallas_call(
        body, name=name, grid=(G, S // tq, nk),
        out_shape=(jax.ShapeDtypeStruct((S, G * gw), F32), jax.ShapeDtypeStruct((G, S, LANES), F32)),
        in_specs=[pl.BlockSpec((tq, gw), lambda g, i, j: (i, g)),
                  pl.BlockSpec((tk, HEAD_DIM), lambda g, i, j: (j, g)),
                  pl.BlockSpec((tk, HEAD_DIM), lambda g, i, j: (j, g))],
        out_specs=(pl.BlockSpec((tq, gw), lambda g, i, j: (i, g)),
                   pl.BlockSpec((None, tq, LANES), lambda g, i, j: (g, i, 0))),
        scratch_shapes=[pltpu.VMEM((qpk, tq, 1), F32), pltpu.VMEM((qpk, tq, 1), F32),
                        pltpu.VMEM((qpk, tq, HEAD_DIM), F32)],
        compiler_params=_params("parallel", "parallel", "arbitrary"),
    )(q, k, v)


def _attn_delta(do, o, *, G, name):
    S, q_w = o.shape
    gw = q_w // G
    qpk = gw // HEAD_DIM
    tq = _pick(S, (512, 256, 128))

    def body(do_ref, o_ref, d_ref):
        lane = lax.broadcasted_iota(jnp.int32, (tq, LANES), 1)
        out = jnp.zeros((tq, LANES), F32)
        for h in range(qpk):
            hs = slice(h * HEAD_DIM, (h + 1) * HEAD_DIM)
            out = jnp.where(lane == h, jnp.sum(do_ref[:, hs] * o_ref[:, hs], axis=-1, keepdims=True), out)
        d_ref[...] = out

    return pl.pallas_call(
        body, name=name, grid=(G, S // tq), out_shape=jax.ShapeDtypeStruct((G, S, LANES), F32),
        in_specs=[pl.BlockSpec((tq, gw), lambda g, i: (i, g)), pl.BlockSpec((tq, gw), lambda g, i: (i, g))],
        out_specs=pl.BlockSpec((None, tq, LANES), lambda g, i: (g, i, 0)),
        compiler_params=_params("parallel", "parallel"),
    )(do, o)


def _flash_bwd(q, k, v, do, lse, delta, *, S, name):
    NK, kv_w = k.shape
    G = kv_w // HEAD_DIM
    qpk = q.shape[1] // kv_w
    gw = qpk * HEAD_DIM
    tq = _pick(S, (512, 256, 128))
    tk = _pick(NK, KEY_TILES)

    def body(q_ref, k_ref, v_ref, do_ref, lse_ref, dl_ref, dq_ref, dk_ref, dv_ref, lse_s, dl_s):
        qi, ki = pl.program_id(1), pl.program_id(2)

        @pl.when((qi == 0) & (ki == 0))
        def _():
            dk_ref[...] = jnp.zeros(dk_ref.shape, F32)
            dv_ref[...] = jnp.zeros(dv_ref.shape, F32)

        @pl.when(ki == 0)
        def _():
            dq_ref[...] = jnp.zeros(dq_ref.shape, F32)
            for h in range(qpk):
                lse_s[h] = _lane_pick(lse_ref[...], h)
                dl_s[h] = _lane_pick(dl_ref[...], h)

        k_t, v_t = k_ref[...], v_ref[...]
        dk_acc = jnp.zeros((tk, HEAD_DIM), F32)
        dv_acc = jnp.zeros((tk, HEAD_DIM), F32)
        for h in range(qpk):
            hs = slice(h * HEAD_DIM, (h + 1) * HEAD_DIM)
            q_h = q_ref[:, hs]
            do_h = do_ref[:, hs].astype(BF16)
            s = lax.dot_general(q_h, k_t, NT, preferred_element_type=F32)
            p = jnp.exp(s - lse_s[h])
            dv_acc = dv_acc + lax.dot_general(p.astype(BF16), do_h, TN, preferred_element_type=F32)
            dp = lax.dot_general(do_h, v_t, NT, preferred_element_type=F32)
            ds = (p * (dp - dl_s[h])).astype(BF16)
            dq_ref[:, hs] += lax.dot_general(ds, k_t, NN, preferred_element_type=F32)
            dk_acc = dk_acc + lax.dot_general(ds, q_h, TN, preferred_element_type=F32)
        rows = pl.ds(pl.multiple_of(ki * tk, tk), tk)
        dk_ref[rows, :] += dk_acc
        dv_ref[rows, :] += dv_acc

    qspec = pl.BlockSpec((tq, gw), lambda g, i, j: (i, g))
    kspec = pl.BlockSpec((tk, HEAD_DIM), lambda g, i, j: (j, g))
    lspec = pl.BlockSpec((None, tq, LANES), lambda g, i, j: (g, i, 0))
    group = pl.BlockSpec((NK, HEAD_DIM), lambda g, i, j: (0, g))
    kv_shape = jax.ShapeDtypeStruct((NK, kv_w), F32)
    return pl.pallas_call(
        body, name=name, grid=(G, S // tq, NK // tk),
        out_shape=(jax.ShapeDtypeStruct((S, G * gw), F32), kv_shape, kv_shape),
        in_specs=[qspec, kspec, kspec, qspec, lspec, lspec], out_specs=(qspec, group, group),
        scratch_shapes=[pltpu.VMEM((qpk, tq, 1), F32), pltpu.VMEM((qpk, tq, 1), F32)],
        compiler_params=_params("arbitrary", "arbitrary", "arbitrary"),
    )(q, k, v, do, lse, delta)


def _wide_specs(col0, width, bw):
    return [pl.BlockSpec((ROW_TILE, bw), functools.partial(lambda i, c: (i, c), c=col0 // bw + p))
            for p in range(width // bw)]


def _cat(refs):
    return refs[0][...] if len(refs) == 1 else jnp.concatenate([r[...] for r in refs], axis=1)


def _gelu(x):
    return 0.5 * x * (1.0 + lax.erf(x * (1.0 / math.sqrt(2.0))))


def _gelu_grad(x):
    return 0.5 * (1.0 + lax.erf(x * (1.0 / math.sqrt(2.0)))) + x * jnp.exp(-0.5 * x * x) * (1.0 / math.sqrt(2.0 * math.pi))


def _layernorm_stats(v):
    mu = jnp.mean(v, axis=-1, keepdims=True)
    xc = v - mu
    rstd = lax.rsqrt(jnp.mean(xc * xc, axis=-1, keepdims=True) + EPS)
    return xc * rstd, rstd


def _gmlp_fwd(zx, ln_w, ln_b, w_s, b_sb, *, S, col0, name):
    G, W = w_s.shape[0], ln_w.shape[1]
    gd = W // G
    bw = math.gcd(col0, W)
    n_parts = W // bw
    vec = pl.BlockSpec((1, W), lambda i: (0, 0))
    full3 = pl.BlockSpec((G, CHUNK, CHUNK), lambda i: (0, 0, 0))
    full3b = pl.BlockSpec((G, CHUNK, gd), lambda i: (0, 0, 0))

    def body(*refs):
        u_refs, v_refs = refs[:n_parts], refs[n_parts:2 * n_parts]
        lnw_ref, lnb_ref, ws_ref, bs_ref, gm_ref = refs[2 * n_parts:]
        u = _gelu(_cat(u_refs))
        vhat, _ = _layernorm_stats(_gelu(_cat(v_refs)))
        vn = (vhat * lnw_ref[...] + lnb_ref[...]).astype(BF16)
        for c in range(ROW_TILE // CHUNK):
            rs = slice(c * CHUNK, (c + 1) * CHUNK)
            for g in range(G):
                cs = slice(g * gd, (g + 1) * gd)
                mixed = lax.dot_general(ws_ref[g].astype(BF16), vn[rs, cs], NN, preferred_element_type=F32) + bs_ref[g]
                gm_ref[rs, cs] = (u[rs, cs] * mixed).astype(BF16)

    return pl.pallas_call(
        body, name=name, grid=(S // ROW_TILE,), out_shape=jax.ShapeDtypeStruct((S, W), BF16),
        in_specs=_wide_specs(col0, W, bw) + _wide_specs(col0 + W, W, bw) + [vec, vec, full3, full3b],
        out_specs=pl.BlockSpec((ROW_TILE, W), lambda i: (i, 0)), compiler_params=_params("parallel"),
    )(*([zx] * (2 * n_parts)), ln_w, ln_b, w_s, b_sb)


def _gmlp_bwd(zx, dgm, ln_w, ln_b, w_s, b_sb, *, S, col0, name):
    G, W = w_s.shape[0], ln_w.shape[1]
    gd = W // G
    bw = math.gcd(col0, W)
    n_parts = W // bw
    vec = pl.BlockSpec((1, W), lambda i: (0, 0))
    full3 = pl.BlockSpec((G, CHUNK, CHUNK), lambda i: (0, 0, 0))
    full3b = pl.BlockSpec((G, CHUNK, gd), lambda i: (0, 0, 0))
    row = pl.BlockSpec((ROW_TILE, W), lambda i: (i, 0))

    def body(*refs):
        u_refs, v_refs = refs[:n_parts], refs[n_parts:2 * n_parts]
        dgm_ref, lnw_ref, lnb_ref, ws_ref, bs_ref, dz_ref, dws_ref, dbs_ref, dlnw_ref, dlnb_ref, du_s, dvn_s = refs[2 * n_parts:]
        i = pl.program_id(0)
        upre, vpre = _cat(u_refs), _cat(v_refs)
        u = _gelu(upre)
        vhat, rstd = _layernorm_stats(_gelu(vpre))
        lnw = lnw_ref[...]
        vn = (vhat * lnw + lnb_ref[...]).astype(BF16)
        dgm_t = dgm_ref[...]

        @pl.when(i == 0)
        def _():
            dws_ref[...] = jnp.zeros(dws_ref.shape, F32)
            dbs_ref[...] = jnp.zeros(dbs_ref.shape, F32)

        for c in range(ROW_TILE // CHUNK):
            rs = slice(c * CHUNK, (c + 1) * CHUNK)
            for g in range(G):
                cs = slice(g * gd, (g + 1) * gd)
                ws_g = ws_ref[g].astype(BF16)
                vn_cg = vn[rs, cs]
                mixed = lax.dot_general(ws_g, vn_cg, NN, preferred_element_type=F32) + bs_ref[g]
                dgm_cg = dgm_t[rs, cs]
                du_s[rs, cs] = dgm_cg * mixed
                dmixed = dgm_cg * u[rs, cs]
                dmixed_b = dmixed.astype(BF16)
                dws_ref[g] += lax.dot_general(dmixed_b, vn_cg, NT, preferred_element_type=F32)
                dbs_ref[g] += dmixed
                dvn_s[rs, cs] = lax.dot_general(ws_g, dmixed_b, TN, preferred_element_type=F32)

        dvn = dvn_s[...]
        _accumulate(dlnw_ref, jnp.sum(dvn * vhat, axis=0, keepdims=True), i == 0)
        _accumulate(dlnb_ref, jnp.sum(dvn, axis=0, keepdims=True), i == 0)
        dvhat = dvn * lnw
        dv = rstd * (dvhat - jnp.mean(dvhat, axis=-1, keepdims=True)
                     - vhat * jnp.mean(dvhat * vhat, axis=-1, keepdims=True))
        dz_ref[:, :W] = (du_s[...] * _gelu_grad(upre)).astype(BF16)
        dz_ref[:, W:] = (dv * _gelu_grad(vpre)).astype(BF16)

    return pl.pallas_call(
        body, name=name, grid=(S // ROW_TILE,),
        out_shape=(jax.ShapeDtypeStruct((S, 2 * W), BF16), jax.ShapeDtypeStruct((G, CHUNK, CHUNK), F32),
                   jax.ShapeDtypeStruct((G, CHUNK, gd), F32), jax.ShapeDtypeStruct((1, W), F32),
                   jax.ShapeDtypeStruct((1, W), F32)),
        in_specs=_wide_specs(col0, W, bw) + _wide_specs(col0 + W, W, bw) + [row, vec, vec, full3, full3b],
        out_specs=(pl.BlockSpec((ROW_TILE, 2 * W), lambda i: (i, 0)), full3, full3b, vec, vec),
        scratch_shapes=[pltpu.VMEM((ROW_TILE, W), F32), pltpu.VMEM((ROW_TILE, W), F32)],
        compiler_params=_params("arbitrary"),
    )(*([zx] * (2 * n_parts)), dgm, ln_w, ln_b, w_s, b_sb)


def _merge_fwd(zx, a_br, g_br, b_gate, *, S, col0, name):
    D = a_br.shape[1]
    cw = min(math.gcd(col0, D), 1024)
    nc = D // cw
    c0 = col0 // cw
    blk = lambda f: pl.BlockSpec((ROW_TILE, cw), f)
    bias = lambda t: pl.BlockSpec((None, 1, cw), lambda i, j: (t, 0, j))

    def body(l0_ref, l1_ref, a_ref, g_ref, b0_ref, b1_ref, t_ref):
        g0 = _sigmoid(l0_ref[...] + b0_ref[...])
        g1 = _sigmoid(l1_ref[...] + b1_ref[...])
        t_ref[...] = (g0 * a_ref[...] + g1 * g_ref[...]).astype(BF16)

    return pl.pallas_call(
        body, name=name, grid=(S // ROW_TILE, nc), out_shape=jax.ShapeDtypeStruct((S, D), BF16),
        in_specs=[blk(lambda i, j: (i, c0 + j)), blk(lambda i, j: (i, c0 + nc + j)), blk(lambda i, j: (i, j)),
                  blk(lambda i, j: (i, j)), bias(0), bias(1)],
        out_specs=blk(lambda i, j: (i, j)), compiler_params=_params("parallel", "parallel"),
    )(zx, zx, a_br, g_br, b_gate, b_gate)


def _merge_bwd(dt, zx, a_br, g_br, b_gate, *, S, col0, name):
    D = a_br.shape[1]
    cw = min(math.gcd(col0, D), 1024)
    nc = D // cw
    c0 = col0 // cw
    blk = lambda f: pl.BlockSpec((ROW_TILE, cw), f)
    bias = lambda t: pl.BlockSpec((None, 1, cw), lambda j, i: (t, 0, j))
    own = blk(lambda j, i: (i, j))
    acc = pl.BlockSpec((1, cw), lambda j, i: (0, j))

    def body(dt_ref, l0_ref, l1_ref, a_ref, g_ref, b0_ref, b1_ref, da_ref, dg_ref, dl0_ref, dl1_ref, db0_ref, db1_ref):
        i = pl.program_id(1)
        dt_t = dt_ref[...]
        g0 = _sigmoid(l0_ref[...] + b0_ref[...])
        g1 = _sigmoid(l1_ref[...] + b1_ref[...])
        da_ref[...] = (dt_t * g0).astype(BF16)
        dg_ref[...] = (dt_t * g1).astype(BF16)
        dl0 = dt_t * a_ref[...] * (g0 * (1.0 - g0))
        dl1 = dt_t * g_ref[...] * (g1 * (1.0 - g1))
        dl0_ref[...] = dl0.astype(BF16)
        dl1_ref[...] = dl1.astype(BF16)
        _accumulate(db0_ref, jnp.sum(dl0, axis=0, keepdims=True), i == 0)
        _accumulate(db1_ref, jnp.sum(dl1, axis=0, keepdims=True), i == 0)

    sd = lambda dt_: jax.ShapeDtypeStruct((S, D), dt_)
    return pl.pallas_call(
        body, name=name, grid=(nc, S // ROW_TILE),
        out_shape=(sd(BF16), sd(BF16), sd(BF16), sd(BF16), jax.ShapeDtypeStruct((1, D), F32),
                   jax.ShapeDtypeStruct((1, D), F32)),
        in_specs=[own, blk(lambda j, i: (i, c0 + j)), blk(lambda j, i: (i, c0 + nc + j)), own, own, bias(0), bias(1)],
        out_specs=(own, own, own, own, acc, acc), compiler_params=_params("parallel", "arbitrary"),
    )(dt, zx, zx, a_br, g_br, b_gate, b_gate)


def _mod_fwd(cvecs, w_mod, name):
    R, D = cvecs.shape
    nsh = w_mod.shape[1]
    tn = _pick(nsh, (512, 256, 128))

    def body(c_ref, w_ref, o_ref):
        cv = c_ref[...]
        a = (cv * _sigmoid(cv)).astype(BF16)
        o_ref[...] = lax.dot_general(a, w_ref[...].astype(BF16), NN, preferred_element_type=F32)

    return pl.pallas_call(
        body, name=name, grid=(nsh // tn,), out_shape=jax.ShapeDtypeStruct((R, nsh), F32),
        in_specs=[pl.BlockSpec((R, D), lambda j: (0, 0)), pl.BlockSpec((D, tn), lambda j: (0, j))],
        out_specs=pl.BlockSpec((R, tn), lambda j: (0, j)), compiler_params=_params("parallel"),
    )(cvecs, w_mod)


def _mod_wgrad(cvecs, dm, name):
    R, D = cvecs.shape
    nsh = dm.shape[1]
    tn = _pick(nsh, (512, 256, 128))

    def body(c_ref, dm_ref, o_ref):
        cv = c_ref[...]
        a = (cv * _sigmoid(cv)).astype(BF16)
        o_ref[...] = lax.dot_general(a, dm_ref[...].astype(BF16), TN, preferred_element_type=F32)

    return pl.pallas_call(
        body, name=name, grid=(nsh // tn,), out_shape=jax.ShapeDtypeStruct((D, nsh), F32),
        in_specs=[pl.BlockSpec((R, D), lambda j: (0, 0)), pl.BlockSpec((R, tn), lambda j: (0, j))],
        out_specs=pl.BlockSpec((D, tn), lambda j: (0, j)), compiler_params=_params("parallel"),
    )(cvecs, dm)


def _mod_dgrad(dm, w_mod, name):
    R, nsh = dm.shape
    D = w_mod.shape[0]
    tn = _pick(D, (256, 128))

    def body(dm_ref, w_ref, o_ref):
        o_ref[...] = lax.dot_general(dm_ref[...].astype(BF16), w_ref[...].astype(BF16), NT, preferred_element_type=F32)

    return pl.pallas_call(
        body, name=name, grid=(D // tn,), out_shape=jax.ShapeDtypeStruct((R, D), F32),
        in_specs=[pl.BlockSpec((R, nsh), lambda j: (0, 0)), pl.BlockSpec((tn, nsh), lambda j: (j, 0))],
        out_specs=pl.BlockSpec((R, tn), lambda j: (0, j)), compiler_params=_params("parallel"),
    )(dm, w_mod)


def _adam_rows(R, C):
    return _pick(R, [t for t in (512, 256, 128, 64, 32, 16) if t * C * 4 <= ADAM_TILE_BYTES] or [16])


def _adamw(w, m, v, grads, name):
    R, C = w.shape
    tr = _adam_rows(R, C)
    n_g = len(grads)
    blk = pl.BlockSpec((tr, C), lambda i: (i, 0))
    c1 = 1.0 - ADAM_B1 ** ADAM_STEP
    c2 = 1.0 - ADAM_B2 ** ADAM_STEP

    def body(*refs):
        w_ref, m_ref, v_ref = refs[:3]
        g_refs = refs[3:3 + n_g]
        g_ref, d_ref, m2_ref, v2_ref = refs[3 + n_g:]
        g = g_refs[0][...].astype(F32)
        for r in g_refs[1:]:
            g = g + r[...].astype(F32)
        m2 = ADAM_B1 * m_ref[...] + (1.0 - ADAM_B1) * g
        v2 = ADAM_B2 * v_ref[...] + (1.0 - ADAM_B2) * (g * g)
        g_ref[...] = g
        m2_ref[...] = m2
        v2_ref[...] = v2
        d_ref[...] = -ADAM_LR * ((m2 / c1) / (jnp.sqrt(v2 / c2) + ADAM_EPS) + ADAM_WD * w_ref[...])

    out = jax.ShapeDtypeStruct((R, C), F32)
    return pl.pallas_call(
        body, name=name, grid=(R // tr,), out_shape=(out, out, out, out),
        in_specs=[blk] * (3 + n_g), out_specs=(blk, blk, blk, blk), compiler_params=_params("parallel"),
    )(w, m, v, *grads)


def _sum_slabs(slabs, name):
    n, R, C = slabs.shape
    tr = _adam_rows(R, C)

    def body(s_ref, o_ref):
        acc = s_ref[0].astype(F32)
        for k in range(1, n):
            acc = acc + s_ref[k].astype(F32)
        o_ref[...] = acc.astype(BF16)

    return pl.pallas_call(
        body, name=name, grid=(R // tr,), out_shape=jax.ShapeDtypeStruct((R, C), BF16),
        in_specs=[pl.BlockSpec((n, tr, C), lambda i: (0, i, 0))],
        out_specs=pl.BlockSpec((tr, C), lambda i: (i, 0)), compiler_params=_params("parallel"),
    )(slabs)


def _plane_peers():
    x, y = lax.axis_index("x"), lax.axis_index("y")
    return [(1 - x, y), (x, 1 - y), (1 - x, 1 - y)]


class _Rider:
    def __init__(self, arrays, out_shapes, copies, relay=None):
        n = len(arrays)
        self.arrays, self.out_shapes = list(arrays), list(out_shapes)
        self.sems = [pltpu.SemaphoreType.DMA((3 * n,)), pltpu.SemaphoreType.DMA((3 * n,)), pltpu.SemaphoreType.DMA((n,))]
        if relay:
            self.sems += [pltpu.SemaphoreType.DMA((3 * n,)), pltpu.SemaphoreType.DMA((3 * n,))]
        self._copies, self._relay = copies, relay

    def start(self, ins, outs, sems):
        local, sends, _ = self._copies(ins, outs, sems)
        for cp in local + sends:
            cp.start()

    def finish(self, ins, outs, sems):
        local, sends, recvs = self._copies(ins, outs, sems)
        passed, landed = self._relay(ins, outs, sems) if self._relay else ([None] * len(recvs), [])
        for cp, on in zip(recvs, passed):
            cp.wait_recv()
            if on is not None:
                on.start()
        for cp in landed:
            cp.wait_recv()
        for cp in sends + [on for on in passed if on is not None]:
            cp.wait_send()
        for cp in local:
            cp.wait()


def _gather_rider(shards):
    n = len(shards)
    half_rows = [s.shape[0] // 2 for s in shards]

    def half(ref, w, hc):
        return ref.at[pl.ds(pl.multiple_of(hc * half_rows[w], 16), half_rows[w]), :]

    def copies(ins, outs, sems):
        send_sems, recv_sems, local_sems = sems[:3]
        x, y, c = lax.axis_index("x"), lax.axis_index("y"), lax.axis_index("c")
        me = 2 * x + y
        peers = _plane_peers()

        def remote(w, k, slab):
            px, py = peers[k]
            return pltpu.make_async_remote_copy(
                src_ref=half(ins[w], w, c), dst_ref=half(outs[w].at[slab], w, c), send_sem=send_sems.at[3 * w + k],
                recv_sem=recv_sems.at[3 * w + k], device_id=(px, py, c), device_id_type=MESH)

        local = [pltpu.make_async_copy(ins[w], outs[w].at[me], local_sems.at[w]) for w in range(n)]
        sends = [remote(w, k, me) for w in range(n) for k in range(3)]
        recvs = [remote(w, k, 2 * px + py) for w in range(n) for k, (px, py) in enumerate(peers)]
        return local, sends, recvs

    def relay(ins, outs, sems):
        send_sems, recv_sems = sems[3:]
        x, y, c = lax.axis_index("x"), lax.axis_index("y"), lax.axis_index("c")
        peers = _plane_peers()

        def sibling(w, k, hc):
            px, py = peers[k]
            part = half(outs[w].at[2 * px + py], w, hc)
            return pltpu.make_async_remote_copy(
                src_ref=part, dst_ref=part, send_sem=send_sems.at[3 * w + k], recv_sem=recv_sems.at[3 * w + k],
                device_id=(x, y, 1 - c), device_id_type=MESH)

        passed = [sibling(w, k, c) for w in range(n) for k in range(3)]
        landed = [sibling(w, k, 1 - c) for w in range(n) for k in range(3)]
        return passed, landed

    return _Rider(shards, [jax.ShapeDtypeStruct((N_SHARDS,) + s.shape, s.dtype) for s in shards], copies, relay)


def _scatter_rider(fulls):
    n = len(fulls)

    def copies(ins, outs, sems):
        send_sems, recv_sems, local_sems = sems
        x, y, c = lax.axis_index("x"), lax.axis_index("y"), lax.axis_index("c")
        me = 2 * x + y
        peers = _plane_peers()

        def remote(w, k):
            px, py = peers[k]
            return pltpu.make_async_remote_copy(
                src_ref=ins[w].at[2 * px + py], dst_ref=outs[w].at[k], send_sem=send_sems.at[3 * w + k],
                recv_sem=recv_sems.at[3 * w + k], device_id=(px, py, c), device_id_type=MESH)

        local = [pltpu.make_async_copy(ins[w].at[me], outs[w].at[3], local_sems.at[w]) for w in range(n)]
        sends = [remote(w, k) for w in range(n) for k in range(3)]
        return local, sends, sends

    return _Rider(fulls, [jax.ShapeDtypeStruct(f.shape, f.dtype) for f in fulls], copies)


def _comm_call(rider, name):
    n = len(rider.arrays)

    def body(*refs):
        ins, outs, sems = refs[:n], refs[n:2 * n], refs[2 * n:]
        rider.start(ins, outs, sems)
        rider.finish(ins, outs, sems)

    return list(pl.pallas_call(
        body, name=name, out_shape=tuple(rider.out_shapes), in_specs=[HBM_SPEC] * n, out_specs=tuple([HBM_SPEC] * n),
        scratch_shapes=rider.sems,
    )(*rider.arrays))


def _sibling_exchange(blocks, name):
    n = len(blocks)

    def body(*refs):
        ins, outs = refs[:n], refs[n:2 * n]
        send_sems, recv_sems = refs[2 * n:]
        sibling = (lax.axis_index("x"), lax.axis_index("y"), 1 - lax.axis_index("c"))
        sends = [pltpu.make_async_remote_copy(src_ref=ins[w], dst_ref=outs[w], send_sem=send_sems.at[w],
                                              recv_sem=recv_sems.at[w], device_id=sibling, device_id_type=MESH)
                 for w in range(n)]
        for cp in sends:
            cp.start()
        for cp in sends:
            cp.wait_recv()
        for cp in sends:
            cp.wait_send()

    return pl.pallas_call(
        body, name=name, out_shape=tuple(jax.ShapeDtypeStruct(b.shape, b.dtype) for b in blocks),
        in_specs=[HBM_SPEC] * n, out_specs=tuple([HBM_SPEC] * n),
        scratch_shapes=[pltpu.SemaphoreType.DMA((n,)), pltpu.SemaphoreType.DMA((n,))],
    )(*blocks)


def _allgather_devices(block, name):
    m_per, n_cols = block.shape

    def body(x_ref, out_ref, send_sems, recv_sems, local_sem):
        x, y, c = lax.axis_index("x"), lax.axis_index("y"), lax.axis_index("c")
        me, sibling = (x, y, c), (x, y, 1 - c)
        chips = _plane_peers()

        def rows(px, py, pc):
            return out_ref.at[pl.ds((4 * px + 2 * py + pc) * m_per, m_per), :]

        def copy(k, blk, to, src=None):
            return pltpu.make_async_remote_copy(
                src_ref=rows(*blk) if src is None else src, dst_ref=rows(*blk), send_sem=send_sems.at[k],
                recv_sem=recv_sems.at[k], device_id=to, device_id_type=MESH)

        mine = pltpu.make_async_copy(x_ref, rows(*me), local_sem)
        mine.start()
        first = [copy(0, me, sibling, src=x_ref)]
        first += [copy(1 + j, me, (*chip, c), src=x_ref) for j, chip in enumerate(chips)]
        for cp in first:
            cp.start()
        passed = [copy(4 + j, (*chip, c), sibling) for j, chip in enumerate(chips)]
        for j, chip in enumerate(chips):
            copy(1 + j, (*chip, c), me).wait_recv()
            passed[j].start()
        copy(0, sibling, me).wait_recv()
        for j, chip in enumerate(chips):
            copy(4 + j, (*chip, 1 - c), me).wait_recv()
        for cp in first + passed:
            cp.wait_send()
        mine.wait()

    out = pl.pallas_call(
        body, name=name, out_shape=jax.ShapeDtypeStruct((N_DEV * m_per, n_cols), block.dtype),
        in_specs=[VMEM_SPEC], out_specs=VMEM_SPEC,
        scratch_shapes=[pltpu.SemaphoreType.DMA((7,)), pltpu.SemaphoreType.DMA((7,)), pltpu.SemaphoreType.DMA],
        compiler_params=pltpu.CompilerParams(vmem_limit_bytes=VMEM_LIMIT_BYTES),
    )(block)
    return out.reshape(N_DEV, m_per, n_cols)


PACK_ROWS = 16


def _pack_rows(shape, width):
    return -(-math.prod(shape) // (width * PACK_ROWS)) * PACK_ROWS


def _pack(arrays, width):
    rows = []
    for a in arrays:
        flat = a.reshape(-1).astype(F32)
        n_rows = _pack_rows(a.shape, width)
        rows.append(jnp.pad(flat, (0, n_rows * width - flat.shape[0])).reshape(n_rows, width))
    return jnp.concatenate(rows, axis=0)


def _unpack(packed, shapes, width):
    out, r = [], 0
    for shp in shapes:
        size = math.prod(shp)
        n_rows = _pack_rows(shp, width)
        out.append(packed[r:r + n_rows].reshape(-1)[:size].reshape(shp))
        r += n_rows
    return out


def _rope_tables(S, C):
    rows = S // GRID_W
    axis_dim = HEAD_DIM // 2
    row = jnp.broadcast_to(jnp.arange(rows, dtype=F32)[:, None], (rows, GRID_W)).reshape(-1)
    col = jnp.broadcast_to(jnp.arange(GRID_W, dtype=F32)[None, :], (rows, GRID_W)).reshape(-1)
    inv_freq = ROPE_THETA ** (-jnp.arange(0, axis_dim, 2, dtype=F32) / axis_dim)
    ang = jnp.concatenate([row[:, None] * inv_freq, col[:, None] * inv_freq], axis=-1)
    cos, sin = jnp.cos(ang), jnp.sin(ang)
    cos2 = jnp.repeat(cos, 2, axis=-1)
    sin2 = jnp.stack([-sin, sin], axis=-1).reshape(S, HEAD_DIM)
    cos2 = jnp.concatenate([cos2, jnp.ones((C, HEAD_DIM), F32)], axis=0)
    sin2 = jnp.concatenate([sin2, jnp.zeros((C, HEAD_DIM), F32)], axis=0)
    return cos2, sin2


def kernel(x, c, ctx, c_ctx, w_mod, b_mod, norm_w, w_ffn1_in, w_ffn1_out, w_ffn2_in, w_ffn2_out, w_in, b_gate, q_norm_w, k_norm_w, gmlp_ln_w, gmlp_ln_b, w_spatial, b_spatial, w_branch_attn, w_branch_gmlp, w_out, final_norm_w, loss_target, m_c_ctx, m_w_mod, m_b_mod, m_norm_w, m_w_ffn1_in, m_w_ffn1_out, m_w_ffn2_in, m_w_ffn2_out, m_w_in, m_b_gate, m_q_norm_w, m_k_norm_w, m_gmlp_ln_w, m_gmlp_ln_b, m_w_spatial, m_b_spatial, m_w_branch_attn, m_w_branch_gmlp, m_w_out, m_final_norm_w, v_c_ctx, v_w_mod, v_b_mod, v_norm_w, v_w_ffn1_in, v_w_ffn1_out, v_w_ffn2_in, v_w_ffn2_out, v_w_in, v_b_gate, v_q_norm_w, v_k_norm_w, v_gmlp_ln_w, v_gmlp_ln_b, v_w_spatial, v_b_spatial, v_w_branch_attn, v_w_branch_gmlp, v_w_out, v_final_norm_w):
    _, S, D = x.shape
    C = ctx.shape[1]
    NTOK = S + C
    n_xt = S // ROW_TILE
    q_w, kv_w = N_Q_HEADS * HEAD_DIM, N_KV_HEADS * HEAD_DIM
    W = gmlp_ln_w.shape[1]
    v_end = q_w + 2 * kv_w
    gv_end = v_end + 2 * W
    scale = HEAD_DIM ** -0.5
    dev = 4 * lax.axis_index("x") + 2 * lax.axis_index("y") + lax.axis_index("c")
    shard = 2 * lax.axis_index("x") + lax.axis_index("y")

    c_all = _allgather_devices(jnp.pad(c, ((0, 7), (0, 0))), "gather_c")[:, 0, :]
    cvecs = jnp.concatenate([c_all, jnp.pad(c_ctx[None, :], ((0, 7), (0, 0)))], axis=0)
    w_mod_l = w_mod[0]
    n_modsh = w_mod_l.shape[1]
    mod_part = _mod_fwd(cvecs, w_mod_l, "mod_fwd")
    mod_all = _allgather_devices(mod_part, "gather_mod")[0::2]
    mod_full = jnp.transpose(mod_all, (1, 0, 2)).reshape(16, N_SHARDS * n_modsh) + b_mod
    mx = lax.dynamic_index_in_dim(mod_full, dev, 0, keepdims=False).reshape(N_MOD, D)
    mc = mod_full[8].reshape(N_MOD, D)
    mods = jnp.concatenate([mx, mc], axis=0).reshape(2 * N_MOD, 1, D)

    shard_of = lambda w: w[0].astype(BF16)
    rows_of = lambda g: g.reshape(-1, g.shape[-1])
    (wf1i,) = _comm_call(_gather_rider([shard_of(w_ffn1_in)]), "gather_ffn1_in")
    n_vsh = norm_w.shape[-1]
    nw_all = _allgather_devices(_pack([norm_w[0], b_gate[0]], n_vsh), "gather_vecs")[0::2]
    nw = jnp.transpose(nw_all[:, 0:3, :], (1, 0, 2)).reshape(3, D)
    bg = jnp.transpose(nw_all[:, PACK_ROWS:PACK_ROWS + 2, :], (1, 0, 2)).reshape(2, 1, D)
    nw0, nw1, nw2 = nw[0:1], nw[1:2], nw[2:3]

    cos2, sin2 = _rope_tables(S, C)
    b_sb = jnp.broadcast_to(b_spatial[0][:, :, None], (GMLP_GROUPS, CHUNK, W // GMLP_GROUPS))
    w_s = w_spatial[0]
    fw = final_norm_w[None, :]

    tok0 = jnp.concatenate([x[0], ctx[0]], axis=0)
    h1 = _normmod(tok0, mods, nw0, k_shift=0, k_scale=1, n_xt=n_xt, name="ffn1_norm")
    z1, (wf1o_g, win) = _matmul(h1, wf1i, form="nn", b_shards=N_SHARDS, out_shards=2, name="ffn1_in",
                                rider=_gather_rider([shard_of(w_ffn1_out), shard_of(w_in)]))
    wf1o = rows_of(wf1o_g)
    g1 = _swiglu_fwd(z1, "ffn1_act")
    y1 = _matmul(g1, wf1o, form="nn", name="ffn1_out", tn=1024)
    tok1, h2 = _normmod(tok0, mods, nw1, k_shift=3, k_scale=4, n_xt=n_xt, name="mix_norm",
                        resid=(y1, 2, MACARON_WEIGHT))
    later = [w_branch_attn, w_branch_gmlp, w_ffn2_in]
    zx, (wba_g, wbg_g, wf2i) = _matmul(h2, win, form="nn", b_shards=N_SHARDS, name="mix_in",
                                       rider=_gather_rider([shard_of(w) for w in later]))
    wba, wbg = rows_of(wba_g), rows_of(wbg_g)
    qt, kt, vt = _qk_prep(zx, cos2, sin2, q_norm_w, k_norm_w, q_w=q_w, kv_w=kv_w, scale=scale, name="qk_prep")
    attn, lse = _flash_fwd(qt, kt, vt, S=S, name="attn_fwd")
    gm = _gmlp_fwd(zx, gmlp_ln_w, gmlp_ln_b, w_s, b_sb, S=S, col0=v_end, name="gmlp_fwd")
    a_br, (wo_g,) = _matmul(attn, wba, form="nn", name="branch_attn", tm=512, rider=_gather_rider([shard_of(w_out)]))
    wo = rows_of(wo_g)
    g_br = _matmul(gm, wbg, form="nn", name="branch_gmlp", tm=512)
    t_mix = _merge_fwd(zx, a_br, g_br, bg, S=S, col0=gv_end, name="merge_fwd")
    y_mix = _matmul(t_mix, wo, form="nn", name="mix_out", tm=512)
    x2, h3 = _normmod(tok1, mods, nw2, k_shift=6, k_scale=7, n_xt=n_xt, name="ffn2_norm", resid=(y_mix, 5, None),
                      rows=S)
    z2, (wf2o_g,) = _matmul(h3, wf2i, form="nn", b_shards=N_SHARDS, out_shards=2, name="ffn2_in",
                            rider=_gather_rider([shard_of(w_ffn2_out)]))
    wf2o = rows_of(wf2o_g)
    g2 = _swiglu_fwd(z2, "ffn2_act")
    y2 = _matmul(g2, wf2o, form="nn", name="ffn2_out", tn=1024)
    x3 = _resid_only(x2, y2, mods, k_gate=8, name="ffn2_resid")

    dx3, loss_row, dfw = _final_loss(x3, loss_target[0], fw, "loss")

    dy2, dgate8 = _resid_bwd(dx3, y2, mods, k_gate=8, weight=MACARON_WEIGHT, n_xt=n_xt, name="ffn2_resid_bwd")
    dg2 = _matmul(dy2, wf2o, form="nt", name="ffn2_out_dgrad")
    gw_f2o = _matmul(g2, dy2, form="tn", out_dtype=BF16, name="ffn2_out_wgrad", tn=1024)
    dz2 = _swiglu_bwd(z2, dg2, "ffn2_act_bwd")
    dh3 = _matmul(dz2, wf2i, form="nt", a_shards=2, b_shards=N_SHARDS, name="ffn2_in_dgrad", tn=1024)
    gw_f2i = _matmul(h3, dz2, form="tn", out_dtype=BF16, b_shards=2, out_shards=N_SHARDS, name="ffn2_in_wgrad")
    dx2, dsh6, dsc7, dnw2 = _normmod_bwd(dh3, x2, dx3, mods, nw2, k_scale=7, n_xt=n_xt, name="ffn2_norm_bwd")

    dy_mix, dgate5 = _resid_bwd(dx2, y_mix, mods, k_gate=5, weight=None, n_xt=n_xt, name="mix_resid_bwd")
    dt = _matmul(dy_mix, wo, form="nt", name="mix_out_dgrad", tm=512)
    gw_wo = _matmul(t_mix, dy_mix, form="tn", out_dtype=BF16, name="mix_out_wgrad", tn=1024)
    d_abr, d_gbr, dl0, dl1, dbg0, dbg1 = _merge_bwd(dt, zx, a_br, g_br, bg, S=S, col0=gv_end, name="merge_bwd")
    d_attn = _matmul(d_abr, wba, form="nt", name="branch_attn_dgrad", tm=512)
    gw_wba = _matmul(attn, d_abr, form="tn", out_dtype=BF16, name="branch_attn_wgrad", tn=1024, tk=1024)
    d_gm = _matmul(d_gbr, wbg, form="nt", name="branch_gmlp_dgrad", tm=512)
    gw_wbg = _matmul(gm, d_gbr, form="tn", out_dtype=BF16, name="branch_gmlp_wgrad", tn=1024)
    dz_gm, dws, dbs_wide, dlnw, dlnb = _gmlp_bwd(zx, d_gm, gmlp_ln_w, gmlp_ln_b, w_s, b_sb, S=S, col0=v_end,
                                                 name="gmlp_bwd")
    delta = _attn_delta(d_attn, attn, G=N_KV_HEADS, name="attn_delta")
    dq, dk, dv = _flash_bwd(qt, kt, vt, d_attn, lse, delta, S=S, name="attn_bwd")
    dz_qkv, dqw, dkw = _qk_prep_bwd(zx, dq, dk, dv, cos2, sin2, q_norm_w, k_norm_w, q_w=q_w, kv_w=kv_w,
                                    scale=scale, n_xt=n_xt, name="qk_prep_bwd")
    ctx_pad = ((0, C), (0, 0))
    dzx = jnp.concatenate([dz_qkv, jnp.pad(dz_gm, ctx_pad), jnp.pad(dl0, ctx_pad), jnp.pad(dl1, ctx_pad)], axis=1)
    slabs_of = lambda g: g.reshape(N_SHARDS, g.shape[0] // N_SHARDS, g.shape[1])
    dh2, (rc_f2i,) = _matmul(dzx, win, form="nt", b_shards=N_SHARDS, name="mix_in_dgrad", tn=1024,
                             rider=_scatter_rider([gw_f2i]))
    gw_win, (rc_f2o, rc_wo, rc_wba, rc_wbg) = _matmul(
        h2, dzx, form="tn", out_dtype=BF16, out_shards=N_SHARDS, name="mix_in_wgrad",
        rider=_scatter_rider([slabs_of(g) for g in (gw_f2o, gw_wo, gw_wba, gw_wbg)]))
    dtok1, dsh3, dsc4, dnw1 = _normmod_bwd(dh2, tok1, dx2, mods, nw1, k_scale=4, n_xt=n_xt, name="mix_norm_bwd")

    dy1, dgate2 = _resid_bwd(dtok1, y1, mods, k_gate=2, weight=MACARON_WEIGHT, n_xt=n_xt, name="ffn1_resid_bwd")
    gw_f1o = _matmul(g1, dy1, form="tn", out_dtype=BF16, name="ffn1_out_wgrad", tn=1024)
    dg1, (rc_f1o,) = _matmul(dy1, wf1o, form="nt", name="ffn1_out_dgrad", rider=_scatter_rider([slabs_of(gw_f1o)]))
    dz1 = _swiglu_bwd(z1, dg1, "ffn1_act_bwd")
    gw_f1i, (rc_win,) = _matmul(h1, dz1, form="tn", out_dtype=BF16, b_shards=2, out_shards=N_SHARDS, name="ffn1_in_wgrad",
                                rider=_scatter_rider([gw_win]))
    dh1, (rc_f1i,) = _matmul(dz1, wf1i, form="nt", a_shards=2, b_shards=N_SHARDS, name="ffn1_in_dgrad", tn=1024,
                             rider=_scatter_rider([gw_f1i]))
    dtok0, dsh0, dsc1, dnw0 = _normmod_bwd(dh1, tok0, dtok1, mods, nw0, k_scale=1, n_xt=n_xt, name="ffn1_norm_bwd")
    grad_x = dtok0[:S][None]

    big_w = [w_ffn1_in, w_ffn2_in, w_in, w_ffn1_out, w_ffn2_out, w_branch_attn, w_branch_gmlp, w_out]
    big_m = [m_w_ffn1_in, m_w_ffn2_in, m_w_in, m_w_ffn1_out, m_w_ffn2_out, m_w_branch_attn, m_w_branch_gmlp, m_w_out]
    big_v = [v_w_ffn1_in, v_w_ffn2_in, v_w_in, v_w_ffn1_out, v_w_ffn2_out, v_w_branch_attn, v_w_branch_gmlp, v_w_out]
    big_names = ["w_ffn1_in", "w_ffn2_in", "w_in", "w_ffn1_out", "w_ffn2_out", "w_branch_attn", "w_branch_gmlp", "w_out"]
    received = [rc_f1i, rc_f2i, rc_win, rc_f1o, rc_f2o, rc_wba, rc_wbg, rc_wo]
    plane_sums = [_sum_slabs(r, "plane_sum_" + nm) for r, nm in zip(received, big_names)]
    sibling_sums = _sibling_exchange(plane_sums, "sibling_grads")
    big_out = {}
    for nm, w_, m_, v_, pa, pb in zip(big_names, big_w, big_m, big_v, plane_sums, sibling_sums):
        res = _adamw(w_[0], m_[0], v_[0], [pa, pb], "adamw_" + nm)
        big_out[nm] = [r[None] for r in res]

    zeros9 = jnp.zeros((1, D), F32)
    dmx = jnp.concatenate([dsh0[0], dsc1[0], dgate2[0], dsh3[0], dsc4[0], dgate5[0], dsh6[0], dsc7[0], dgate8[0]], axis=0)
    dmc = jnp.concatenate([dsh0[1], dsc1[1], dgate2[1], dsh3[1], dsc4[1], zeros9, zeros9, zeros9, zeros9], axis=0)
    dbs = jnp.sum(dbs_wide, axis=-1)
    parts = [dmx, dmc, jnp.concatenate([dnw0, dnw1, dnw2], axis=0), jnp.concatenate([dbg0, dbg1], axis=0),
             dqw, dkw, dlnw, dlnb, dws, dbs, dfw, loss_row[:, :1]]
    part_shapes = [p.shape for p in parts]
    small_all = _allgather_devices(_pack(parts, D), "gather_small")
    dmx_all = small_all[:, 0:N_MOD, :].reshape(N_DEV, N_MOD * D)
    small_sum = small_all[0]
    for d in range(1, N_DEV):
        small_sum = small_sum + small_all[d]
    (_, dmc_sum, g_nw, g_bg, g_qw, g_kw, g_lnw, g_lnb, g_ws, g_bs, g_fw, loss_sum) = _unpack(small_sum, part_shapes, D)
    loss = loss_sum[0, 0]
    dmx_sum = small_sum[0:N_MOD].reshape(1, N_MOD * D)
    g_b_mod = dmx_sum + dmc_sum.reshape(1, N_MOD * D)
    dm_rows = jnp.concatenate([dmx_all, jnp.pad(dmc_sum.reshape(1, N_MOD * D), ((0, 7), (0, 0)))], axis=0)
    dm_sh = lax.dynamic_slice_in_dim(dm_rows, shard * n_modsh, n_modsh, axis=1)
    g_w_mod = _mod_wgrad(cvecs, dm_sh, "mod_wgrad")
    dsc_part = _mod_dgrad(dm_sh, w_mod_l, "mod_dgrad")
    dsc_all = _allgather_devices(dsc_part, "gather_dsilu")[0::2, 8, :]
    dscc = ((dsc_all[0] + dsc_all[1]) + dsc_all[2]) + dsc_all[3]
    sg = _sigmoid(c_ctx)
    g_c_ctx = dscc * (sg * (1.0 + c_ctx * (1.0 - sg)))
    g_norm_w = lax.dynamic_slice_in_dim(g_nw, shard * n_vsh, n_vsh, axis=1)[None]
    g_b_gate = lax.dynamic_slice_in_dim(g_bg, shard * n_vsh, n_vsh, axis=1)[None]

    w_mod_out = [r[None] for r in _adamw(w_mod_l, m_w_mod[0], v_w_mod[0], [g_w_mod], "adamw_w_mod")]

    small_names = ["c_ctx", "b_mod", "norm_w", "b_gate", "q_norm_w", "k_norm_w", "gmlp_ln_w", "gmlp_ln_b",
                   "w_spatial", "b_spatial", "final_norm_w"]
    small_w = [c_ctx, b_mod, norm_w, b_gate, q_norm_w, k_norm_w, gmlp_ln_w, gmlp_ln_b, w_spatial, b_spatial, final_norm_w]
    small_m = [m_c_ctx, m_b_mod, m_norm_w, m_b_gate, m_q_norm_w, m_k_norm_w, m_gmlp_ln_w, m_gmlp_ln_b, m_w_spatial,
               m_b_spatial, m_final_norm_w]
    small_v = [v_c_ctx, v_b_mod, v_norm_w, v_b_gate, v_q_norm_w, v_k_norm_w, v_gmlp_ln_w, v_gmlp_ln_b, v_w_spatial,
               v_b_spatial, v_final_norm_w]
    small_g = [g_c_ctx, g_b_mod, g_norm_w, g_b_gate, g_qw, g_kw, g_lnw, g_lnb, g_ws, g_bs, g_fw]
    small_shapes = [w_.shape for w_ in small_w]
    packed = [_pack(group, D) for group in (small_w, small_m, small_v, small_g)]
    small_res = [_unpack(r, small_shapes, D) for r in _adamw(*packed[:3], [packed[3]], "adamw_small")]
    small_out = {nm: [small_res[t][i] for t in range(4)] for i, nm in enumerate(small_names)}

    order = ["c_ctx", "w_mod", "b_mod", "norm_w", "w_ffn1_in", "w_ffn1_out", "w_ffn2_in", "w_ffn2_out", "w_in", "b_gate",
             "q_norm_w", "k_norm_w", "gmlp_ln_w", "gmlp_ln_b", "w_spatial", "b_spatial", "w_branch_attn", "w_branch_gmlp",
             "w_out", "final_norm_w"]
    results = {**big_out, **small_out, "w_mod": w_mod_out}
    outs = [loss, grad_x]
    for t in range(4):
        outs += [results[nm][t] for nm in order]
    return tuple(outs)


def _resid_only(xp, y_in, mods, *, k_gate, name):
    M, D = xp.shape
    row = pl.BlockSpec((ROW_TILE, D), lambda i: (i, 0))

    def body(xp_ref, y_ref, g_ref, x_ref):
        x_ref[...] = xp_ref[...] + (MACARON_WEIGHT * g_ref[...]) * y_ref[...]

    return pl.pallas_call(
        body, name=name, grid=(M // ROW_TILE,), out_shape=jax.ShapeDtypeStruct((M, D), F32),
        in_specs=[row, row, pl.BlockSpec((None, 1, D), lambda i: (k_gate, 0, 0))], out_specs=row,
        compiler_params=_params("parallel"),
    )(xp, y_in, mods)
```

```python
import functools
import math

import jax
import jax.numpy as jnp
from jax import lax
from jax.experimental import pallas as pl
from jax.experimental.pallas import tpu as pltpu

F32 = jnp.float32
BF16 = jnp.bfloat16
MESH = pl.DeviceIdType.MESH
HBM_SPEC = pl.BlockSpec(memory_space=pltpu.HBM)
VMEM_SPEC = pl.BlockSpec(memory_space=pltpu.VMEM)

HEAD_DIM = 128
N_Q_HEADS = 16
N_KV_HEADS = 4
GMLP_GROUPS = 16
CHUNK = 128
GRID_W = 64
ROPE_THETA = 10000.0
EPS = 1e-6
MACARON_WEIGHT = 0.5
N_MOD = 9

ADAM_LR = 0.001
ADAM_B1 = 0.9
ADAM_B2 = 0.999
ADAM_EPS = 1e-08
ADAM_WD = 0.01
ADAM_STEP = 10

N_SHARDS = 4
N_DEV = 8
ROW_TILE = 256
KEY_TILES = (2816, 768, 512, 256, 128)
FWD_KEY_TILES = (8448,) + KEY_TILES
LANES = 128
VMEM_LIMIT_BYTES = 48 * 1024 * 1024
ADAM_TILE_BYTES = 1 << 20

NN = (((1,), (0,)), ((), ()))
NT = (((1,), (1,)), ((), ()))
TN = (((0,), (0,)), ((), ()))


def _pick(n, cands):
    for t in cands:
        if t <= n and n % t == 0:
            return t
    raise ValueError(f"no tile for {n} among {cands}")


def _params(*sem):
    return pltpu.CompilerParams(dimension_semantics=sem or None, vmem_limit_bytes=VMEM_LIMIT_BYTES)


def _sigmoid(x):
    return 1.0 / (1.0 + jnp.exp(-x))


def _matmul(a, b, *, form, name, out_dtype=F32, a_shards=1, b_shards=1, out_shards=1, tm=None, tn=None, tk=None,
            order="ji", rider=None, swiglu_z=None):
    if form == "nn":
        M, K = a.shape
        N = b.shape[-1] * b_shards
    elif form == "nt":
        M, K = a.shape[-2], a.shape[-1] * a_shards
        N = b.shape[-2]
    else:
        K, M = a.shape
        N = b.shape[-1] * b_shards
    n_b = N // b_shards if form in ("nn", "tn") else N
    n_o = N // out_shards
    k_a = K // a_shards
    k_b = K // b_shards if form == "nt" else K
    tm = _pick(M, ((tm,) if tm else ()) + (1024, 768, 512, 256, 128, 64, 32, 16))
    tn = _pick(math.gcd(n_b, n_o), ((tn,) if tn else ()) + (1408, 1024, 512, 256, 128))
    tk = tk or _pick(math.gcd(k_a, k_b),
                     (2816, 2048, 1024, 768, 512, 256, 128) if form == "tn" else
                     (2816, 2048, 1408, 1024, 768, 512, 256, 128))
    nk = K // tk
    nbb, nbo = n_b // tn, n_o // tn
    ka, kb = k_a // tk, k_b // tk
    dims = {"nn": NN, "nt": NT, "tn": TN}[form]

    def ij(g0, g1):
        return (g0, g1) if order == "ij" else (g1, g0)

    def a_map(g0, g1, k):
        i, _ = ij(g0, g1)
        if form == "tn":
            return (k, i)
        return (k // ka, i, k % ka) if a_shards > 1 else (i, k)

    def b_map(g0, g1, k):
        _, j = ij(g0, g1)
        if form == "nt":
            return (k // kb, j, k % kb) if b_shards > 1 else (j, k)
        return (j // nbb, k, j % nbb) if b_shards > 1 else (k, j)

    def o_map(g0, g1, k):
        i, j = ij(g0, g1)
        return (j // nbo, i, j % nbo) if out_shards > 1 else (i, j)

    a_block = (tk, tm) if form == "tn" else (tm, tk)
    if a_shards > 1:
        a_block = (None,) + a_block
    b_block = (tk, tn) if form in ("nn", "tn") else (tn, tk)
    if b_shards > 1:
        b_block = (None,) + b_block
    o_block = (None, tm, tn) if out_shards > 1 else (tm, tn)
    o_shape = (out_shards, M, n_o) if out_shards > 1 else (M, N)

    n_i, n_j = M // tm, N // tn
    grid = (n_i, n_j, nk) if order == "ij" else (n_j, n_i, nk)
    n_ride = len(rider.arrays) if rider else 0
    n_acc = 1 if nk > 1 else 0

    n_z = 0 if swiglu_z is None else 1
    if n_z:
        assert nk == 1 and out_shards == 1, "the swiglu epilogue needs the whole contraction in one step"

    def body(*refs):
        a_ref, b_ref = refs[:2]
        n_in = 2 + n_z
        ride_in = refs[n_in:n_in + n_ride]
        o_ref = refs[n_in + n_ride]
        ride_out = refs[n_in + 1 + n_ride:n_in + 1 + 2 * n_ride]
        scratch = refs[n_in + 1 + 2 * n_ride:]
        ride_sems = scratch[n_acc:]
        pid = [pl.program_id(d) for d in range(3)]
        if rider:
            @pl.when((pid[0] == 0) & (pid[1] == 0) & (pid[2] == 0))
            def _():
                rider.start(ride_in, ride_out, ride_sems)

        part = lax.dot_general(a_ref[...].astype(BF16), b_ref[...].astype(BF16), dims, preferred_element_type=F32)
        if n_z:
            z_ref = refs[2]
            za = z_ref[0]
            sig = _sigmoid(za)
            o_ref[0] = (part * z_ref[1] * (sig * (1.0 + za * (1.0 - sig)))).astype(o_ref.dtype)
            o_ref[1] = (part * (za * sig)).astype(o_ref.dtype)
        elif nk == 1:
            o_ref[...] = part.astype(o_ref.dtype)
        else:
            acc = scratch[0]
            k = pid[2]

            @pl.when(k == 0)
            def _():
                acc[...] = part

            @pl.when(k > 0)
            def _():
                acc[...] += part

            @pl.when(k == nk - 1)
            def _():
                o_ref[...] = acc[...].astype(o_ref.dtype)

        if rider:
            @pl.when((pid[0] == grid[0] - 1) & (pid[1] == grid[1] - 1) & (pid[2] == nk - 1))
            def _():
                rider.finish(ride_in, ride_out, ride_sems)

    main_shape = jax.ShapeDtypeStruct(o_shape, out_dtype)
    main_spec = pl.BlockSpec(o_block, o_map)
    z_specs = []
    if n_z:
        pair_map = lambda g0, g1, k: (0,) + ij(g0, g1)
        z_specs = [pl.BlockSpec((2, tm, tn), pair_map)]
        main_shape = jax.ShapeDtypeStruct((2, M, N), out_dtype)
        main_spec = pl.BlockSpec((2, tm, tn), pair_map)
    res = pl.pallas_call(
        body,
        name=name,
        out_shape=(main_shape, *rider.out_shapes) if rider else main_shape,
        grid=grid,
        in_specs=[pl.BlockSpec(a_block, a_map), pl.BlockSpec(b_block, b_map)] + z_specs + [HBM_SPEC] * n_ride,
        out_specs=(main_spec, *[HBM_SPEC] * n_ride) if rider else main_spec,
        scratch_shapes=([pltpu.VMEM((tm, tn), F32)] if nk > 1 else []) + (rider.sems if rider else []),
        compiler_params=_params(*(("arbitrary",) * 3 if rider else ("parallel", "parallel", "arbitrary"))),
    )(a, b, *([swiglu_z] if n_z else []), *(rider.arrays if rider else ()))
    return (res[0], list(res[1:])) if rider else res


def _type_of(i, n_xt, n_tiles):
    return jnp.where(i >= n_xt, 1, 0) if n_tiles > n_xt else 0


def _mod_spec(D, k, n_xt, n_tiles):
    return pl.BlockSpec((None, 1, D), lambda i: (_type_of(i, n_xt, n_tiles) * N_MOD + k, 0, 0))


def _acc_spec(D, n_xt, n_tiles):
    return pl.BlockSpec((None, 1, D), lambda i: (_type_of(i, n_xt, n_tiles), 0, 0))


def _accumulate(ref, value, first):
    @pl.when(first)
    def _():
        ref[...] = value

    @pl.when(jnp.logical_not(first))
    def _():
        ref[...] += value


def _normmod(xp, mods, nw, *, k_shift, k_scale, n_xt, name, resid=None, rows=None):
    M, D = rows or xp.shape[0], xp.shape[1]
    n_tiles = M // ROW_TILE
    row = pl.BlockSpec((ROW_TILE, D), lambda i: (i, 0))
    vec = pl.BlockSpec((1, D), lambda i: (0, 0))
    mod = functools.partial(_mod_spec, D, n_xt=n_xt, n_tiles=n_tiles)

    if resid is not None:
        y_in, k_gate, weight = resid

        def body(xp_ref, y_ref, g_ref, nw_ref, sh_ref, sc_ref, x_ref, h_ref):
            g = g_ref[...] if weight is None else weight * g_ref[...]
            x = xp_ref[...] + g * y_ref[...]
            x_ref[...] = x
            r = lax.rsqrt(jnp.mean(x * x, axis=-1, keepdims=True) + EPS)
            y = (x * r) * nw_ref[...]
            h_ref[...] = (y * (1.0 + sc_ref[...]) + sh_ref[...]).astype(BF16)

        return pl.pallas_call(
            body, name=name, grid=(n_tiles,),
            out_shape=(jax.ShapeDtypeStruct((M, D), F32), jax.ShapeDtypeStruct((M, D), BF16)),
            in_specs=[row, row, mod(k_gate), vec, mod(k_shift), mod(k_scale)],
            out_specs=(row, row), compiler_params=_params("parallel"),
        )(xp, y_in, mods, nw, mods, mods)

    def body(xp_ref, nw_ref, sh_ref, sc_ref, h_ref):
        x = xp_ref[...]
        r = lax.rsqrt(jnp.mean(x * x, axis=-1, keepdims=True) + EPS)
        y = (x * r) * nw_ref[...]
        h_ref[...] = (y * (1.0 + sc_ref[...]) + sh_ref[...]).astype(BF16)

    return pl.pallas_call(
        body, name=name, grid=(n_tiles,), out_shape=jax.ShapeDtypeStruct((M, D), BF16),
        in_specs=[row, vec, mod(k_shift), mod(k_scale)], out_specs=row, compiler_params=_params("parallel"),
    )(xp, nw, mods, mods)


def _resid_bwd(dxo, y_in, mods, *, k_gate, weight, n_xt, name):
    M, D = dxo.shape
    n_tiles = M // ROW_TILE
    n_types = 2 if n_tiles > n_xt else 1
    row = pl.BlockSpec((ROW_TILE, D), lambda i: (i, 0))

    def body(dxo_ref, y_ref, g_ref, dy_ref, dg_ref):
        i = pl.program_id(0)
        dxo_t = dxo_ref[...]
        g = g_ref[...] if weight is None else weight * g_ref[...]
        yw = y_ref[...] if weight is None else weight * y_ref[...]
        dy_ref[...] = (dxo_t * g).astype(BF16)
        _accumulate(dg_ref, jnp.sum(dxo_t * yw, axis=0, keepdims=True), (i == 0) | (i == n_xt))

    return pl.pallas_call(
        body, name=name, grid=(n_tiles,),
        out_shape=(jax.ShapeDtypeStruct((M, D), BF16), jax.ShapeDtypeStruct((n_types, 1, D), F32)),
        in_specs=[row, row, _mod_spec(D, k_gate, n_xt, n_tiles)],
        out_specs=(row, _acc_spec(D, n_xt, n_tiles)), compiler_params=_params("arbitrary"),
    )(dxo, y_in, mods)


def _normmod_bwd(dh, x_in, dxo, mods, nw, *, k_scale, n_xt, name):
    M, D = dh.shape
    n_tiles = M // ROW_TILE
    n_types = 2 if n_tiles > n_xt else 1
    dxo_tiles = dxo.shape[0] // ROW_TILE
    row = pl.BlockSpec((ROW_TILE, D), lambda i: (i, 0))
    dxo_row = pl.BlockSpec((ROW_TILE, D), lambda i: (jnp.minimum(i, dxo_tiles - 1), 0))
    vec = pl.BlockSpec((1, D), lambda i: (0, 0))

    def body(dh_ref, x_ref, dxo_ref, nw_ref, sc_ref, dx_ref, dsh_ref, dsc_ref, dnw_ref):
        i = pl.program_id(0)
        x = x_ref[...]
        dh_t = dh_ref[...]
        w = nw_ref[...]
        r = lax.rsqrt(jnp.mean(x * x, axis=-1, keepdims=True) + EPS)
        n = x * r
        dy = dh_t * (1.0 + sc_ref[...])
        dn = dy * w
        dx = r * (dn - n * jnp.mean(dn * n, axis=-1, keepdims=True))
        if dxo_tiles < n_tiles:
            dx_ref[...] = dx + jnp.where(i < dxo_tiles, dxo_ref[...], 0.0)
        else:
            dx_ref[...] = dx + dxo_ref[...]
        first = (i == 0) | (i == n_xt)
        _accumulate(dsh_ref, jnp.sum(dh_t, axis=0, keepdims=True), first)
        _accumulate(dsc_ref, jnp.sum(dh_t * (n * w), axis=0, keepdims=True), first)
        _accumulate(dnw_ref, jnp.sum(dy * n, axis=0, keepdims=True), i == 0)

    acc = _acc_spec(D, n_xt, n_tiles)
    return pl.pallas_call(
        body, name=name, grid=(n_tiles,),
        out_shape=(jax.ShapeDtypeStruct((M, D), F32), jax.ShapeDtypeStruct((n_types, 1, D), F32),
                   jax.ShapeDtypeStruct((n_types, 1, D), F32), jax.ShapeDtypeStruct((1, D), F32)),
        in_specs=[row, row, dxo_row, vec, _mod_spec(D, k_scale, n_xt, n_tiles)],
        out_specs=(row, acc, acc, vec), compiler_params=_params("arbitrary"),
    )(dh, x_in, dxo, nw, mods)


def _swiglu_fwd(z, name):
    _, M, Fh = z.shape
    tf = _pick(Fh, (1408, 1024, 512, 256, 128))

    def body(z_ref, g_ref):
        a = z_ref[0]
        g_ref[...] = ((a * _sigmoid(a)) * z_ref[1]).astype(BF16)

    return pl.pallas_call(
        body, name=name, grid=(M // ROW_TILE, Fh // tf), out_shape=jax.ShapeDtypeStruct((M, Fh), BF16),
        in_specs=[pl.BlockSpec((2, ROW_TILE, tf), lambda i, j: (0, i, j))],
        out_specs=pl.BlockSpec((ROW_TILE, tf), lambda i, j: (i, j)), compiler_params=_params("parallel", "parallel"),
    )(z)


def _final_loss(x3, target, fw, name):
    S, D = x3.shape
    row = pl.BlockSpec((ROW_TILE, D), lambda i: (i, 0))
    vec = pl.BlockSpec((1, D), lambda i: (0, 0))

    def body(x_ref, t_ref, fw_ref, dx_ref, loss_ref, dfw_ref):
        i = pl.program_id(0)
        x = x_ref[...]
        w = fw_ref[...]
        r = lax.rsqrt(jnp.mean(x * x, axis=-1, keepdims=True) + EPS)
        n = x * r
        err = n * w - t_ref[...]
        tile_loss = 0.5 * jnp.sum(jnp.mean(err * err, axis=-1, keepdims=True), axis=0, keepdims=True)
        dout = err / D
        dn = dout * w
        dx_ref[...] = r * (dn - n * jnp.mean(dn * n, axis=-1, keepdims=True))
        _accumulate(loss_ref, jnp.broadcast_to(tile_loss, (1, LANES)), i == 0)
        _accumulate(dfw_ref, jnp.sum(dout * n, axis=0, keepdims=True), i == 0)

    return pl.pallas_call(
        body, name=name, grid=(S // ROW_TILE,),
        out_shape=(jax.ShapeDtypeStruct((S, D), F32), jax.ShapeDtypeStruct((1, LANES), F32),
                   jax.ShapeDtypeStruct((1, D), F32)),
        in_specs=[row, row, vec],
        out_specs=(row, pl.BlockSpec((1, LANES), lambda i: (0, 0)), vec), compiler_params=_params("arbitrary"),
    )(x3, target, fw)


def _pair_swap(v):
    lane = lax.broadcasted_iota(jnp.int32, v.shape, v.ndim - 1)
    from_next = pltpu.roll(lane, 1, v.ndim - 1) == (lane ^ 1)
    return jnp.where(from_next, pltpu.roll(v, 1, v.ndim - 1), pltpu.roll(v, LANES - 1, v.ndim - 1))


def _qk_prep(zx, cos2, sin2, qw, kw, *, q_w, kv_w, scale, name):
    M = zx.shape[0]
    nqh, nkh = q_w // HEAD_DIM, kv_w // HEAD_DIM
    row = lambda w, c: pl.BlockSpec((ROW_TILE, w), lambda i: (i, c))
    vec = pl.BlockSpec((1, HEAD_DIM), lambda i: (0, 0))

    def rot(z, w, cos_t, sin_t):
        r = lax.rsqrt(jnp.mean(z * z, axis=-1, keepdims=True) + EPS)
        y = (z * r) * w
        return y * cos_t + _pair_swap(y) * sin_t

    def body(zq_ref, zk_ref, zv_ref, cos_ref, sin_ref, qw_ref, kw_ref, q_ref, k_ref, v_ref):
        cos_t, sin_t = cos_ref[...], sin_ref[...]
        for h in range(nqh):
            hs = slice(h * HEAD_DIM, (h + 1) * HEAD_DIM)
            q_ref[:, hs] = (rot(zq_ref[:, hs], qw_ref[...], cos_t, sin_t) * scale).astype(BF16)
        for h in range(nkh):
            hs = slice(h * HEAD_DIM, (h + 1) * HEAD_DIM)
            k_ref[:, hs] = rot(zk_ref[:, hs], kw_ref[...], cos_t, sin_t).astype(BF16)
        v_ref[...] = zv_ref[...].astype(BF16)

    return pl.pallas_call(
        body, name=name, grid=(M // ROW_TILE,),
        out_shape=(jax.ShapeDtypeStruct((M, q_w), BF16), jax.ShapeDtypeStruct((M, kv_w), BF16),
                   jax.ShapeDtypeStruct((M, kv_w), BF16)),
        in_specs=[row(q_w, 0), row(kv_w, q_w // kv_w), row(kv_w, q_w // kv_w + 1), row(HEAD_DIM, 0), row(HEAD_DIM, 0),
                  vec, vec],
        out_specs=(row(q_w, 0), row(kv_w, 0), row(kv_w, 0)), compiler_params=_params("parallel"),
    )(zx, zx, zx, cos2, sin2, qw, kw)


def _qk_prep_bwd(zx, dq, dk, dv, cos2, sin2, qw, kw, *, q_w, kv_w, scale, n_xt, name):
    M = zx.shape[0]
    nqh, nkh = q_w // HEAD_DIM, kv_w // HEAD_DIM
    dq_tiles = dq.shape[0] // ROW_TILE
    row = lambda w, c: pl.BlockSpec((ROW_TILE, w), lambda i: (i, c))
    vec = pl.BlockSpec((1, HEAD_DIM), lambda i: (0, 0))

    def unrot(z, d, w, cos_t, sin_t):
        r = lax.rsqrt(jnp.mean(z * z, axis=-1, keepdims=True) + EPS)
        n = z * r
        dy = d * cos_t - _pair_swap(d) * sin_t
        dn = dy * w
        dz = r * (dn - n * jnp.mean(dn * n, axis=-1, keepdims=True))
        return dz, jnp.sum(dy * n, axis=0, keepdims=True)

    def body(zq_ref, zk_ref, dq_ref, dk_ref, dv_ref, cos_ref, sin_ref, qw_ref, kw_ref, dz_ref, dqw_ref, dkw_ref):
        i = pl.program_id(0)
        cos_t, sin_t = cos_ref[...], sin_ref[...]
        is_x = i < n_xt
        dqw = jnp.zeros((1, HEAD_DIM), F32)
        dkw = jnp.zeros((1, HEAD_DIM), F32)
        for h in range(nqh):
            hs = slice(h * HEAD_DIM, (h + 1) * HEAD_DIM)
            d = jnp.where(is_x, dq_ref[:, hs], 0.0) * scale
            dz, dw = unrot(zq_ref[:, hs], d, qw_ref[...], cos_t, sin_t)
            dz_ref[:, hs] = dz.astype(BF16)
            dqw = dqw + dw
        for h in range(nkh):
            hs = slice(h * HEAD_DIM, (h + 1) * HEAD_DIM)
            dz, dw = unrot(zk_ref[:, hs], dk_ref[:, hs], kw_ref[...], cos_t, sin_t)
            dz_ref[:, q_w + h * HEAD_DIM:q_w + (h + 1) * HEAD_DIM] = dz.astype(BF16)
            dkw = dkw + dw
        dz_ref[:, q_w + kv_w:] = dv_ref[...].astype(BF16)
        _accumulate(dqw_ref, dqw, i == 0)
        _accumulate(dkw_ref, dkw, i == 0)

    return pl.pallas_call(
        body, name=name, grid=(M // ROW_TILE,),
        out_shape=(jax.ShapeDtypeStruct((M, q_w + 2 * kv_w), BF16), jax.ShapeDtypeStruct((1, HEAD_DIM), F32),
                   jax.ShapeDtypeStruct((1, HEAD_DIM), F32)),
        in_specs=[row(q_w, 0), row(kv_w, q_w // kv_w),
                  pl.BlockSpec((ROW_TILE, q_w), lambda i: (jnp.minimum(i, dq_tiles - 1), 0)),
                  row(kv_w, 0), row(kv_w, 0), row(HEAD_DIM, 0), row(HEAD_DIM, 0), vec, vec],
        out_specs=(row(q_w + 2 * kv_w, 0), vec, vec), compiler_params=_params("arbitrary"),
    )(zx, zx, dq, dk, dv, cos2, sin2, qw, kw)


def _lane_pick(tile, h):
    lane = lax.broadcasted_iota(jnp.int32, tile.shape, 1)
    return jnp.sum(jnp.where(lane == h, tile, 0.0), axis=-1, keepdims=True)


def _flash_fwd(q, k, v, *, S, name):
    NK, kv_w = k.shape
    G = kv_w // HEAD_DIM
    qpk = q.shape[1] // kv_w
    gw = qpk * HEAD_DIM
    tq = _pick(S, (512, 256, 128))
    tk = _pick(NK, FWD_KEY_TILES)
    nk = NK // tk

    def body(q_ref, k_ref, v_ref, o_ref, lse_ref, m_s, l_s, acc_s):
        ki = pl.program_id(2)

        @pl.when(ki == 0)
        def _():
            m_s[...] = jnp.full(m_s.shape, -1e30, F32)
            l_s[...] = jnp.zeros(l_s.shape, F32)
            acc_s[...] = jnp.zeros(acc_s.shape, F32)

        k_t, v_t = k_ref[...], v_ref[...]
        for h in range(qpk):
            s = lax.dot_general(q_ref[:, h * HEAD_DIM:(h + 1) * HEAD_DIM], k_t, NT, preferred_element_type=F32)
            m_prev = m_s[h]
            m_new = jnp.maximum(m_prev, jnp.max(s, axis=-1, keepdims=True))
            alpha = jnp.exp(m_prev - m_new)
            p = jnp.exp(s - m_new)
            l_s[h] = alpha * l_s[h] + jnp.sum(p, axis=-1, keepdims=True)
            acc_s[h] = alpha * acc_s[h] + lax.dot_general(p.astype(BF16), v_t, NN, preferred_element_type=F32)
            m_s[h] = m_new

        @pl.when(ki == nk - 1)
        def _():
            lane = lax.broadcasted_iota(jnp.int32, (tq, LANES), 1)
            lse = jnp.zeros((tq, LANES), F32)
            for h in range(qpk):
                l = l_s[h]
                o_ref[:, h * HEAD_DIM:(h + 1) * HEAD_DIM] = acc_s[h] / l
                lse = jnp.where(lane == h, m_s[h] + jnp.log(l), lse)
            lse_ref[...] = lse

    return pl.pallas_call(
        body, name=name, grid=(G, S // tq, nk),
        out_shape=(jax.ShapeDtypeStruct((S, G * gw), F32), jax.ShapeDtypeStruct((G, S, LANES), F32)),
        in_specs=[pl.BlockSpec((tq, gw), lambda g, i, j: (i, g)),
                  pl.BlockSpec((tk, HEAD_DIM), lambda g, i, j: (j, g)),
                  pl.BlockSpec((tk, HEAD_DIM), lambda g, i, j: (j, g))],
        out_specs=(pl.BlockSpec((tq, gw), lambda g, i, j: (i, g)),
                   pl.BlockSpec((None, tq, LANES), lambda g, i, j: (g, i, 0))),
        scratch_shapes=[pltpu.VMEM((qpk, tq, 1), F32), pltpu.VMEM((qpk, tq, 1), F32),
                        pltpu.VMEM((qpk, tq, HEAD_DIM), F32)],
        compiler_params=_params("parallel", "parallel", "arbitrary"),
    )(q, k, v)


def _attn_delta(do, o, *, G, name):
    S, q_w = o.shape
    gw = q_w // G
    qpk = gw // HEAD_DIM
    tq = _pick(S, (512, 256, 128))

    def body(do_ref, o_ref, d_ref):
        lane = lax.broadcasted_iota(jnp.int32, (tq, LANES), 1)
        out = jnp.zeros((tq, LANES), F32)
        for h in range(qpk):
            hs = slice(h * HEAD_DIM, (h + 1) * HEAD_DIM)
            out = jnp.where(lane == h, jnp.sum(do_ref[:, hs] * o_ref[:, hs], axis=-1, keepdims=True), out)
        d_ref[...] = out

    return pl.pallas_call(
        body, name=name, grid=(G, S // tq), out_shape=jax.ShapeDtypeStruct((G, S, LANES), F32),
        in_specs=[pl.BlockSpec((tq, gw), lambda g, i: (i, g)), pl.BlockSpec((tq, gw), lambda g, i: (i, g))],
        out_specs=pl.BlockSpec((None, tq, LANES), lambda g, i: (g, i, 0)),
        compiler_params=_params("parallel", "parallel"),
    )(do, o)


def _flash_bwd(q, k, v, do, lse, delta, *, S, name):
    NK, kv_w = k.shape
    G = kv_w // HEAD_DIM
    qpk = q.shape[1] // kv_w
    gw = qpk * HEAD_DIM
    tq = _pick(S, (512, 256, 128))
    tk = _pick(NK, KEY_TILES)

    def body(q_ref, k_ref, v_ref, do_ref, lse_ref, dl_ref, dq_ref, dk_ref, dv_ref, lse_s, dl_s):
        qi, ki = pl.program_id(1), pl.program_id(2)

        @pl.when((qi == 0) & (ki == 0))
        def _():
            dk_ref[...] = jnp.zeros(dk_ref.shape, F32)
            dv_ref[...] = jnp.zeros(dv_ref.shape, F32)

        @pl.when(ki == 0)
        def _():
            dq_ref[...] = jnp.zeros(dq_ref.shape, F32)
            for h in range(qpk):
                lse_s[h] = _lane_pick(lse_ref[...], h)
                dl_s[h] = _lane_pick(dl_ref[...], h)

        k_t, v_t = k_ref[...], v_ref[...]
        dk_acc = jnp.zeros((tk, HEAD_DIM), F32)
        dv_acc = jnp.zeros((tk, HEAD_DIM), F32)
        for h in range(qpk):
            hs = slice(h * HEAD_DIM, (h + 1) * HEAD_DIM)
            q_h = q_ref[:, hs]
            do_h = do_ref[:, hs].astype(BF16)
            s = lax.dot_general(q_h, k_t, NT, preferred_element_type=F32)
            p = jnp.exp(s - lse_s[h])
            dv_acc = dv_acc + lax.dot_general(p.astype(BF16), do_h, TN, preferred_element_type=F32)
            dp = lax.dot_general(do_h, v_t, NT, preferred_element_type=F32)
            ds = (p * (dp - dl_s[h])).astype(BF16)
            dq_ref[:, hs] += lax.dot_general(ds, k_t, NN, preferred_element_type=F32)
            dk_acc = dk_acc + lax.dot_general(ds, q_h, TN, preferred_element_type=F32)
        rows = pl.ds(pl.multiple_of(ki * tk, tk), tk)
        dk_ref[rows, :] += dk_acc
        dv_ref[rows, :] += dv_acc

    qspec = pl.BlockSpec((tq, gw), lambda g, i, j: (i, g))
    kspec = pl.BlockSpec((tk, HEAD_DIM), lambda g, i, j: (j, g))
    lspec = pl.BlockSpec((None, tq, LANES), lambda g, i, j: (g, i, 0))
    group = pl.BlockSpec((NK, HEAD_DIM), lambda g, i, j: (0, g))
    kv_shape = jax.ShapeDtypeStruct((NK, kv_w), F32)
    return pl.pallas_call(
        body, name=name, grid=(G, S // tq, NK // tk),
        out_shape=(jax.ShapeDtypeStruct((S, G * gw), F32), kv_shape, kv_shape),
        in_specs=[qspec, kspec, kspec, qspec, lspec, lspec], out_specs=(qspec, group, group),
        scratch_shapes=[pltpu.VMEM((qpk, tq, 1), F32), pltpu.VMEM((qpk, tq, 1), F32)],
        compiler_params=_params("arbitrary", "arbitrary", "arbitrary"),
    )(q, k, v, do, lse, delta)


def _wide_specs(col0, width, bw):
    return [pl.BlockSpec((ROW_TILE, bw), functools.partial(lambda i, c: (i, c), c=col0 // bw + p))
            for p in range(width // bw)]


def _cat(refs):
    return refs[0][...] if len(refs) == 1 else jnp.concatenate([r[...] for r in refs], axis=1)


def _gelu(x):
    return 0.5 * x * (1.0 + lax.erf(x * (1.0 / math.sqrt(2.0))))


def _gelu_grad(x):
    return 0.5 * (1.0 + lax.erf(x * (1.0 / math.sqrt(2.0)))) + x * jnp.exp(-0.5 * x * x) * (1.0 / math.sqrt(2.0 * math.pi))


def _layernorm_stats(v):
    mu = jnp.mean(v, axis=-1, keepdims=True)
    xc = v - mu
    rstd = lax.rsqrt(jnp.mean(xc * xc, axis=-1, keepdims=True) + EPS)
    return xc * rstd, rstd


def _gmlp_fwd(zx, ln_w, ln_b, w_s, b_sb, *, S, col0, name):
    G, W = w_s.shape[0], ln_w.shape[1]
    gd = W // G
    bw = math.gcd(col0, W)
    n_parts = W // bw
    vec = pl.BlockSpec((1, W), lambda i: (0, 0))
    full3 = pl.BlockSpec((G, CHUNK, CHUNK), lambda i: (0, 0, 0))
    full3b = pl.BlockSpec((G, CHUNK, gd), lambda i: (0, 0, 0))

    def body(*refs):
        u_refs, v_refs = refs[:n_parts], refs[n_parts:2 * n_parts]
        lnw_ref, lnb_ref, ws_ref, bs_ref, gm_ref = refs[2 * n_parts:]
        u = _gelu(_cat(u_refs))
        vhat, _ = _layernorm_stats(_gelu(_cat(v_refs)))
        vn = (vhat * lnw_ref[...] + lnb_ref[...]).astype(BF16)
        for c in range(ROW_TILE // CHUNK):
            rs = slice(c * CHUNK, (c + 1) * CHUNK)
            for g in range(G):
                cs = slice(g * gd, (g + 1) * gd)
                mixed = lax.dot_general(ws_ref[g].astype(BF16), vn[rs, cs], NN, preferred_element_type=F32) + bs_ref[g]
                gm_ref[rs, cs] = (u[rs, cs] * mixed).astype(BF16)

    return pl.pallas_call(
        body, name=name, grid=(S // ROW_TILE,), out_shape=jax.ShapeDtypeStruct((S, W), BF16),
        in_specs=_wide_specs(col0, W, bw) + _wide_specs(col0 + W, W, bw) + [vec, vec, full3, full3b],
        out_specs=pl.BlockSpec((ROW_TILE, W), lambda i: (i, 0)), compiler_params=_params("parallel"),
    )(*([zx] * (2 * n_parts)), ln_w, ln_b, w_s, b_sb)


def _gmlp_bwd(zx, dgm, ln_w, ln_b, w_s, b_sb, *, S, col0, name):
    G, W = w_s.shape[0], ln_w.shape[1]
    gd = W // G
    bw = math.gcd(col0, W)
    n_parts = W // bw
    vec = pl.BlockSpec((1, W), lambda i: (0, 0))
    full3 = pl.BlockSpec((G, CHUNK, CHUNK), lambda i: (0, 0, 0))
    full3b = pl.BlockSpec((G, CHUNK, gd), lambda i: (0, 0, 0))
    row = pl.BlockSpec((ROW_TILE, W), lambda i: (i, 0))

    def body(*refs):
        u_refs, v_refs = refs[:n_parts], refs[n_parts:2 * n_parts]
        dgm_ref, lnw_ref, lnb_ref, ws_ref, bs_ref, dz_ref, dws_ref, dbs_ref, dlnw_ref, dlnb_ref, du_s, dvn_s = refs[2 * n_parts:]
        i = pl.program_id(0)
        upre, vpre = _cat(u_refs), _cat(v_refs)
        u = _gelu(upre)
        vhat, rstd = _layernorm_stats(_gelu(vpre))
        lnw = lnw_ref[...]
        vn = (vhat * lnw + lnb_ref[...]).astype(BF16)
        dgm_t = dgm_ref[...]

        @pl.when(i == 0)
        def _():
            dws_ref[...] = jnp.zeros(dws_ref.shape, F32)
            dbs_ref[...] = jnp.zeros(dbs_ref.shape, F32)

        for c in range(ROW_TILE // CHUNK):
            rs = slice(c * CHUNK, (c + 1) * CHUNK)
            for g in range(G):
                cs = slice(g * gd, (g + 1) * gd)
                ws_g = ws_ref[g].astype(BF16)
                vn_cg = vn[rs, cs]
                mixed = lax.dot_general(ws_g, vn_cg, NN, preferred_element_type=F32) + bs_ref[g]
                dgm_cg = dgm_t[rs, cs]
                du_s[rs, cs] = dgm_cg * mixed
                dmixed = dgm_cg * u[rs, cs]
                dmixed_b = dmixed.astype(BF16)
                dws_ref[g] += lax.dot_general(dmixed_b, vn_cg, NT, preferred_element_type=F32)
                dbs_ref[g] += dmixed
                dvn_s[rs, cs] = lax.dot_general(ws_g, dmixed_b, TN, preferred_element_type=F32)

        dvn = dvn_s[...]
        _accumulate(dlnw_ref, jnp.sum(dvn * vhat, axis=0, keepdims=True), i == 0)
        _accumulate(dlnb_ref, jnp.sum(dvn, axis=0, keepdims=True), i == 0)
        dvhat = dvn * lnw
        dv = rstd * (dvhat - jnp.mean(dvhat, axis=-1, keepdims=True)
                     - vhat * jnp.mean(dvhat * vhat, axis=-1, keepdims=True))
        dz_ref[:, :W] = (du_s[...] * _gelu_grad(upre)).astype(BF16)
        dz_ref[:, W:] = (dv * _gelu_grad(vpre)).astype(BF16)

    return pl.pallas_call(
        body, name=name, grid=(S // ROW_TILE,),
        out_shape=(jax.ShapeDtypeStruct((S, 2 * W), BF16), jax.ShapeDtypeStruct((G, CHUNK, CHUNK), F32),
                   jax.ShapeDtypeStruct((G, CHUNK, gd), F32), jax.ShapeDtypeStruct((1, W), F32),
                   jax.ShapeDtypeStruct((1, W), F32)),
        in_specs=_wide_specs(col0, W, bw) + _wide_specs(col0 + W, W, bw) + [row, vec, vec, full3, full3b],
        out_specs=(pl.BlockSpec((ROW_TILE, 2 * W), lambda i: (i, 0)), full3, full3b, vec, vec),
        scratch_shapes=[pltpu.VMEM((ROW_TILE, W), F32), pltpu.VMEM((ROW_TILE, W), F32)],
        compiler_params=_params("arbitrary"),
    )(*([zx] * (2 * n_parts)), dgm, ln_w, ln_b, w_s, b_sb)


def _merge_fwd(zx, a_br, g_br, b_gate, *, S, col0, name):
    D = a_br.shape[1]
    cw = min(math.gcd(col0, D), 1024)
    nc = D // cw
    c0 = col0 // cw
    blk = lambda f: pl.BlockSpec((ROW_TILE, cw), f)
    bias = lambda t: pl.BlockSpec((None, 1, cw), lambda i, j: (t, 0, j))

    def body(l0_ref, l1_ref, a_ref, g_ref, b0_ref, b1_ref, t_ref):
        g0 = _sigmoid(l0_ref[...] + b0_ref[...])
        g1 = _sigmoid(l1_ref[...] + b1_ref[...])
        t_ref[...] = (g0 * a_ref[...] + g1 * g_ref[...]).astype(BF16)

    return pl.pallas_call(
        body, name=name, grid=(S // ROW_TILE, nc), out_shape=jax.ShapeDtypeStruct((S, D), BF16),
        in_specs=[blk(lambda i, j: (i, c0 + j)), blk(lambda i, j: (i, c0 + nc + j)), blk(lambda i, j: (i, j)),
                  blk(lambda i, j: (i, j)), bias(0), bias(1)],
        out_specs=blk(lambda i, j: (i, j)), compiler_params=_params("parallel", "parallel"),
    )(zx, zx, a_br, g_br, b_gate, b_gate)


def _merge_bwd(dt, zx, a_br, g_br, b_gate, *, S, col0, name):
    D = a_br.shape[1]
    cw = min(math.gcd(col0, D), 1024)
    nc = D // cw
    c0 = col0 // cw
    blk = lambda f: pl.BlockSpec((ROW_TILE, cw), f)
    bias = lambda t: pl.BlockSpec((None, 1, cw), lambda j, i: (t, 0, j))
    own = blk(lambda j, i: (i, j))
    acc = pl.BlockSpec((1, cw), lambda j, i: (0, j))

    def body(dt_ref, l0_ref, l1_ref, a_ref, g_ref, b0_ref, b1_ref, da_ref, dg_ref, dl0_ref, dl1_ref, db0_ref, db1_ref):
        i = pl.program_id(1)
        dt_t = dt_ref[...]
        g0 = _sigmoid(l0_ref[...] + b0_ref[...])
        g1 = _sigmoid(l1_ref[...] + b1_ref[...])
        da_ref[...] = (dt_t * g0).astype(BF16)
        dg_ref[...] = (dt_t * g1).astype(BF16)
        dl0 = dt_t * a_ref[...] * (g0 * (1.0 - g0))
        dl1 = dt_t * g_ref[...] * (g1 * (1.0 - g1))
        dl0_ref[...] = dl0.astype(BF16)
        dl1_ref[...] = dl1.astype(BF16)
        _accumulate(db0_ref, jnp.sum(dl0, axis=0, keepdims=True), i == 0)
        _accumulate(db1_ref, jnp.sum(dl1, axis=0, keepdims=True), i == 0)

    sd = lambda dt_: jax.ShapeDtypeStruct((S, D), dt_)
    return pl.pallas_call(
        body, name=name, grid=(nc, S // ROW_TILE),
        out_shape=(sd(BF16), sd(BF16), sd(BF16), sd(BF16), jax.ShapeDtypeStruct((1, D), F32),
                   jax.ShapeDtypeStruct((1, D), F32)),
        in_specs=[own, blk(lambda j, i: (i, c0 + j)), blk(lambda j, i: (i, c0 + nc + j)), own, own, bias(0), bias(1)],
        out_specs=(own, own, own, own, acc, acc), compiler_params=_params("parallel", "arbitrary"),
    )(dt, zx, zx, a_br, g_br, b_gate, b_gate)


def _mod_fwd(cvecs, w_mod, name):
    R, D = cvecs.shape
    nsh = w_mod.shape[1]
    tn = _pick(nsh, (512, 256, 128))

    def body(c_ref, w_ref, o_ref):
        cv = c_ref[...]
        a = (cv * _sigmoid(cv)).astype(BF16)
        o_ref[...] = lax.dot_general(a, w_ref[...].astype(BF16), NN, preferred_element_type=F32)

    return pl.pallas_call(
        body, name=name, grid=(nsh // tn,), out_shape=jax.ShapeDtypeStruct((R, nsh), F32),
        in_specs=[pl.BlockSpec((R, D), lambda j: (0, 0)), pl.BlockSpec((D, tn), lambda j: (0, j))],
        out_specs=pl.BlockSpec((R, tn), lambda j: (0, j)), compiler_params=_params("parallel"),
    )(cvecs, w_mod)


def _mod_wgrad(cvecs, dm, name):
    R, D = cvecs.shape
    nsh = dm.shape[1]
    tn = _pick(nsh, (512, 256, 128))

    def body(c_ref, dm_ref, o_ref):
        cv = c_ref[...]
        a = (cv * _sigmoid(cv)).astype(BF16)
        o_ref[...] = lax.dot_general(a, dm_ref[...].astype(BF16), TN, preferred_element_type=F32)

    return pl.pallas_call(
        body, name=name, grid=(nsh // tn,), out_shape=jax.ShapeDtypeStruct((D, nsh), F32),
        in_specs=[pl.BlockSpec((R, D), lambda j: (0, 0)), pl.BlockSpec((R, tn), lambda j: (0, j))],
        out_specs=pl.BlockSpec((D, tn), lambda j: (0, j)), compiler_params=_params("parallel"),
    )(cvecs, dm)


def _mod_dgrad(dm, w_mod, name):
    R, nsh = dm.shape
    D = w_mod.shape[0]
    tn = _pick(D, (256, 128))

    def body(dm_ref, w_ref, o_ref):
        o_ref[...] = lax.dot_general(dm_ref[...].astype(BF16), w_ref[...].astype(BF16), NT, preferred_element_type=F32)

    return pl.pallas_call(
        body, name=name, grid=(D // tn,), out_shape=jax.ShapeDtypeStruct((R, D), F32),
        in_specs=[pl.BlockSpec((R, nsh), lambda j: (0, 0)), pl.BlockSpec((tn, nsh), lambda j: (j, 0))],
        out_specs=pl.BlockSpec((R, tn), lambda j: (0, j)), compiler_params=_params("parallel"),
    )(dm, w_mod)


def _adam_rows(R, C):
    return _pick(R, [t for t in (512, 256, 128, 64, 32, 16) if t * C * 4 <= ADAM_TILE_BYTES] or [16])


def _adamw(w, m, v, grads, name):
    R, C = w.shape
    tr = _adam_rows(R, C)
    n_g = len(grads)
    blk = pl.BlockSpec((tr, C), lambda i: (i, 0))
    c1 = 1.0 - ADAM_B1 ** ADAM_STEP
    c2 = 1.0 - ADAM_B2 ** ADAM_STEP

    def body(*refs):
        w_ref, m_ref, v_ref = refs[:3]
        g_refs = refs[3:3 + n_g]
        g_ref, d_ref, m2_ref, v2_ref = refs[3 + n_g:]
        g = g_refs[0][...].astype(F32)
        for r in g_refs[1:]:
            g = g + r[...].astype(F32)
        m2 = ADAM_B1 * m_ref[...] + (1.0 - ADAM_B1) * g
        v2 = ADAM_B2 * v_ref[...] + (1.0 - ADAM_B2) * (g * g)
        g_ref[...] = g
        m2_ref[...] = m2
        v2_ref[...] = v2
        d_ref[...] = -ADAM_LR * ((m2 / c1) / (jnp.sqrt(v2 / c2) + ADAM_EPS) + ADAM_WD * w_ref[...])

    out = jax.ShapeDtypeStruct((R, C), F32)
    return pl.pallas_call(
        body, name=name, grid=(R // tr,), out_shape=(out, out, out, out),
        in_specs=[blk] * (3 + n_g), out_specs=(blk, blk, blk, blk), compiler_params=_params("parallel"),
    )(w, m, v, *grads)


def _sum_slabs(slabs, name):
    n, R, C = slabs.shape
    tr = _adam_rows(R, C)

    def body(s_ref, o_ref):
        acc = s_ref[0].astype(F32)
        for k in range(1, n):
            acc = acc + s_ref[k].astype(F32)
        o_ref[...] = acc.astype(BF16)

    return pl.pallas_call(
        body, name=name, grid=(R // tr,), out_shape=jax.ShapeDtypeStruct((R, C), BF16),
        in_specs=[pl.BlockSpec((n, tr, C), lambda i: (0, i, 0))],
        out_specs=pl.BlockSpec((tr, C), lambda i: (i, 0)), compiler_params=_params("parallel"),
    )(slabs)


def _plane_peers():
    x, y = lax.axis_index("x"), lax.axis_index("y")
    return [(1 - x, y), (x, 1 - y), (1 - x, 1 - y)]


class _Rider:
    def __init__(self, arrays, out_shapes, copies, relay=None):
        n = len(arrays)
        self.arrays, self.out_shapes = list(arrays), list(out_shapes)
        self.sems = [pltpu.SemaphoreType.DMA((3 * n,)), pltpu.SemaphoreType.DMA((3 * n,)), pltpu.SemaphoreType.DMA((n,))]
        if relay:
            self.sems += [pltpu.SemaphoreType.DMA((3 * n,)), pltpu.SemaphoreType.DMA((3 * n,))]
        self._copies, self._relay = copies, relay

    def start(self, ins, outs, sems):
        local, sends, _ = self._copies(ins, outs, sems)
        for cp in local + sends:
            cp.start()

    def finish(self, ins, outs, sems):
        local, sends, recvs = self._copies(ins, outs, sems)
        passed, landed = self._relay(ins, outs, sems) if self._relay else ([None] * len(recvs), [])
        for cp, on in zip(recvs, passed):
            cp.wait_recv()
            if on is not None:
                on.start()
        for cp in landed:
            cp.wait_recv()
        for cp in sends + [on for on in passed if on is not None]:
            cp.wait_send()
        for cp in local:
            cp.wait()


def _gather_rider(shards):
    n = len(shards)
    half_rows = [s.shape[0] // 2 for s in shards]

    def half(ref, w, hc):
        return ref.at[pl.ds(pl.multiple_of(hc * half_rows[w], 16), half_rows[w]), :]

    def copies(ins, outs, sems):
        send_sems, recv_sems, local_sems = sems[:3]
        x, y, c = lax.axis_index("x"), lax.axis_index("y"), lax.axis_index("c")
        me = 2 * x + y
        peers = _plane_peers()

        def remote(w, k, slab):
            px, py = peers[k]
            return pltpu.make_async_remote_copy(
                src_ref=half(ins[w], w, c), dst_ref=half(outs[w].at[slab], w, c), send_sem=send_sems.at[3 * w + k],
                recv_sem=recv_sems.at[3 * w + k], device_id=(px, py, c), device_id_type=MESH)

        local = [pltpu.make_async_copy(ins[w], outs[w].at[me], local_sems.at[w]) for w in range(n)]
        sends = [remote(w, k, me) for w in range(n) for k in range(3)]
        recvs = [remote(w, k, 2 * px + py) for w in range(n) for k, (px, py) in enumerate(peers)]
        return local, sends, recvs

    def relay(ins, outs, sems):
        send_sems, recv_sems = sems[3:]
        x, y, c = lax.axis_index("x"), lax.axis_index("y"), lax.axis_index("c")
        peers = _plane_peers()

        def sibling(w, k, hc):
            px, py = peers[k]
            part = half(outs[w].at[2 * px + py], w, hc)
            return pltpu.make_async_remote_copy(
                src_ref=part, dst_ref=part, send_sem=send_sems.at[3 * w + k], recv_sem=recv_sems.at[3 * w + k],
                device_id=(x, y, 1 - c), device_id_type=MESH)

        passed = [sibling(w, k, c) for w in range(n) for k in range(3)]
        landed = [sibling(w, k, 1 - c) for w in range(n) for k in range(3)]
        return passed, landed

    return _Rider(shards, [jax.ShapeDtypeStruct((N_SHARDS,) + s.shape, s.dtype) for s in shards], copies, relay)


def _scatter_rider(fulls):
    n = len(fulls)

    def copies(ins, outs, sems):
        send_sems, recv_sems, local_sems = sems
        x, y, c = lax.axis_index("x"), lax.axis_index("y"), lax.axis_index("c")
        me = 2 * x + y
        peers = _plane_peers()

        def remote(w, k):
            px, py = peers[k]
            return pltpu.make_async_remote_copy(
                src_ref=ins[w].at[2 * px + py], dst_ref=outs[w].at[k], send_sem=send_sems.at[3 * w + k],
                recv_sem=recv_sems.at[3 * w + k], device_id=(px, py, c), device_id_type=MESH)

        local = [pltpu.make_async_copy(ins[w].at[me], outs[w].at[3], local_sems.at[w]) for w in range(n)]
        sends = [remote(w, k) for w in range(n) for k in range(3)]
        return local, sends, sends

    return _Rider(fulls, [jax.ShapeDtypeStruct(f.shape, f.dtype) for f in fulls], copies)


def _comm_call(rider, name):
    n = len(rider.arrays)

    def body(*refs):
        ins, outs, sems = refs[:n], refs[n:2 * n], refs[2 * n:]
        rider.start(ins, outs, sems)
        rider.finish(ins, outs, sems)

    return list(pl.pallas_call(
        body, name=name, out_shape=tuple(rider.out_shapes), in_specs=[HBM_SPEC] * n, out_specs=tuple([HBM_SPEC] * n),
        scratch_shapes=rider.sems,
    )(*rider.arrays))


def _sibling_exchange(blocks, name):
    n = len(blocks)

    def body(*refs):
        ins, outs = refs[:n], refs[n:2 * n]
        send_sems, recv_sems = refs[2 * n:]
        sibling = (lax.axis_index("x"), lax.axis_index("y"), 1 - lax.axis_index("c"))
        sends = [pltpu.make_async_remote_copy(src_ref=ins[w], dst_ref=outs[w], send_sem=send_sems.at[w],
                                              recv_sem=recv_sems.at[w], device_id=sibling, device_id_type=MESH)
                 for w in range(n)]
        for cp in sends:
            cp.start()
        for cp in sends:
            cp.wait_recv()
        for cp in sends:
            cp.wait_send()

    return pl.pallas_call(
        body, name=name, out_shape=tuple(jax.ShapeDtypeStruct(b.shape, b.dtype) for b in blocks),
        in_specs=[HBM_SPEC] * n, out_specs=tuple([HBM_SPEC] * n),
        scratch_shapes=[pltpu.SemaphoreType.DMA((n,)), pltpu.SemaphoreType.DMA((n,))],
    )(*blocks)


def _allgather_devices(block, name):
    m_per, n_cols = block.shape

    def body(x_ref, out_ref, send_sems, recv_sems, local_sem):
        x, y, c = lax.axis_index("x"), lax.axis_index("y"), lax.axis_index("c")
        me, sibling = (x, y, c), (x, y, 1 - c)
        chips = _plane_peers()

        def rows(px, py, pc):
            return out_ref.at[pl.ds((4 * px + 2 * py + pc) * m_per, m_per), :]

        def copy(k, blk, to, src=None):
            return pltpu.make_async_remote_copy(
                src_ref=rows(*blk) if src is None else src, dst_ref=rows(*blk), send_sem=send_sems.at[k],
                recv_sem=recv_sems.at[k], device_id=to, device_id_type=MESH)

        mine = pltpu.make_async_copy(x_ref, rows(*me), local_sem)
        mine.start()
        first = [copy(0, me, sibling, src=x_ref)]
        first += [copy(1 + j, me, (*chip, c), src=x_ref) for j, chip in enumerate(chips)]
        for cp in first:
            cp.start()
        passed = [copy(4 + j, (*chip, c), sibling) for j, chip in enumerate(chips)]
        for j, chip in enumerate(chips):
            copy(1 + j, (*chip, c), me).wait_recv()
            passed[j].start()
        copy(0, sibling, me).wait_recv()
        for j, chip in enumerate(chips):
            copy(4 + j, (*chip, 1 - c), me).wait_recv()
        for cp in first + passed:
            cp.wait_send()
        mine.wait()

    out = pl.pallas_call(
        body, name=name, out_shape=jax.ShapeDtypeStruct((N_DEV * m_per, n_cols), block.dtype),
        in_specs=[VMEM_SPEC], out_specs=VMEM_SPEC,
        scratch_shapes=[pltpu.SemaphoreType.DMA((7,)), pltpu.SemaphoreType.DMA((7,)), pltpu.SemaphoreType.DMA],
        compiler_params=pltpu.CompilerParams(vmem_limit_bytes=VMEM_LIMIT_BYTES),
    )(block)
    return out.reshape(N_DEV, m_per, n_cols)


PACK_ROWS = 16


def _pack_rows(shape, width):
    return -(-math.prod(shape) // (width * PACK_ROWS)) * PACK_ROWS


def _pack(arrays, width):
    rows = []
    for a in arrays:
        flat = a.reshape(-1).astype(F32)
        n_rows = _pack_rows(a.shape, width)
        rows.append(jnp.pad(flat, (0, n_rows * width - flat.shape[0])).reshape(n_rows, width))
    return jnp.concatenate(rows, axis=0)


def _unpack(packed, shapes, width):
    out, r = [], 0
    for shp in shapes:
        size = math.prod(shp)
        n_rows = _pack_rows(shp, width)
        out.append(packed[r:r + n_rows].reshape(-1)[:size].reshape(shp))
        r += n_rows
    return out


def _rope_tables(S, C):
    rows = S // GRID_W
    axis_dim = HEAD_DIM // 2
    row = jnp.broadcast_to(jnp.arange(rows, dtype=F32)[:, None], (rows, GRID_W)).reshape(-1)
    col = jnp.broadcast_to(jnp.arange(GRID_W, dtype=F32)[None, :], (rows, GRID_W)).reshape(-1)
    inv_freq = ROPE_THETA ** (-jnp.arange(0, axis_dim, 2, dtype=F32) / axis_dim)
    ang = jnp.concatenate([row[:, None] * inv_freq, col[:, None] * inv_freq], axis=-1)
    cos, sin = jnp.cos(ang), jnp.sin(ang)
    cos2 = jnp.repeat(cos, 2, axis=-1)
    sin2 = jnp.stack([-sin, sin], axis=-1).reshape(S, HEAD_DIM)
    cos2 = jnp.concatenate([cos2, jnp.ones((C, HEAD_DIM), F32)], axis=0)
    sin2 = jnp.concatenate([sin2, jnp.zeros((C, HEAD_DIM), F32)], axis=0)
    return cos2, sin2


def kernel(x, c, ctx, c_ctx, w_mod, b_mod, norm_w, w_ffn1_in, w_ffn1_out, w_ffn2_in, w_ffn2_out, w_in, b_gate, q_norm_w, k_norm_w, gmlp_ln_w, gmlp_ln_b, w_spatial, b_spatial, w_branch_attn, w_branch_gmlp, w_out, final_norm_w, loss_target, m_c_ctx, m_w_mod, m_b_mod, m_norm_w, m_w_ffn1_in, m_w_ffn1_out, m_w_ffn2_in, m_w_ffn2_out, m_w_in, m_b_gate, m_q_norm_w, m_k_norm_w, m_gmlp_ln_w, m_gmlp_ln_b, m_w_spatial, m_b_spatial, m_w_branch_attn, m_w_branch_gmlp, m_w_out, m_final_norm_w, v_c_ctx, v_w_mod, v_b_mod, v_norm_w, v_w_ffn1_in, v_w_ffn1_out, v_w_ffn2_in, v_w_ffn2_out, v_w_in, v_b_gate, v_q_norm_w, v_k_norm_w, v_gmlp_ln_w, v_gmlp_ln_b, v_w_spatial, v_b_spatial, v_w_branch_attn, v_w_branch_gmlp, v_w_out, v_final_norm_w):
    _, S, D = x.shape
    C = ctx.shape[1]
    NTOK = S + C
    n_xt = S // ROW_TILE
    q_w, kv_w = N_Q_HEADS * HEAD_DIM, N_KV_HEADS * HEAD_DIM
    W = gmlp_ln_w.shape[1]
    v_end = q_w + 2 * kv_w
    gv_end = v_end + 2 * W
    scale = HEAD_DIM ** -0.5
    dev = 4 * lax.axis_index("x") + 2 * lax.axis_index("y") + lax.axis_index("c")
    shard = 2 * lax.axis_index("x") + lax.axis_index("y")

    c_all = _allgather_devices(jnp.pad(c, ((0, 7), (0, 0))), "gather_c")[:, 0, :]
    cvecs = jnp.concatenate([c_all, jnp.pad(c_ctx[None, :], ((0, 7), (0, 0)))], axis=0)
    w_mod_l = w_mod[0]
    n_modsh = w_mod_l.shape[1]
    mod_part = _mod_fwd(cvecs, w_mod_l, "mod_fwd")
    mod_all = _allgather_devices(mod_part, "gather_mod")[0::2]
    mod_full = jnp.transpose(mod_all, (1, 0, 2)).reshape(16, N_SHARDS * n_modsh) + b_mod
    mx = lax.dynamic_index_in_dim(mod_full, dev, 0, keepdims=False).reshape(N_MOD, D)
    mc = mod_full[8].reshape(N_MOD, D)
    mods = jnp.concatenate([mx, mc], axis=0).reshape(2 * N_MOD, 1, D)

    shard_of = lambda w: w[0].astype(BF16)
    rows_of = lambda g: g.reshape(-1, g.shape[-1])
    (wf1i,) = _comm_call(_gather_rider([shard_of(w_ffn1_in)]), "gather_ffn1_in")
    n_vsh = norm_w.shape[-1]
    nw_all = _allgather_devices(_pack([norm_w[0], b_gate[0]], n_vsh), "gather_vecs")[0::2]
    nw = jnp.transpose(nw_all[:, 0:3, :], (1, 0, 2)).reshape(3, D)
    bg = jnp.transpose(nw_all[:, PACK_ROWS:PACK_ROWS + 2, :], (1, 0, 2)).reshape(2, 1, D)
    nw0, nw1, nw2 = nw[0:1], nw[1:2], nw[2:3]

    cos2, sin2 = _rope_tables(S, C)
    b_sb = jnp.broadcast_to(b_spatial[0][:, :, None], (GMLP_GROUPS, CHUNK, W // GMLP_GROUPS))
    w_s = w_spatial[0]
    fw = final_norm_w[None, :]

    tok0 = jnp.concatenate([x[0], ctx[0]], axis=0)
    h1 = _normmod(tok0, mods, nw0, k_shift=0, k_scale=1, n_xt=n_xt, name="ffn1_norm")
    z1, (wf1o_g, win) = _matmul(h1, wf1i, form="nn", b_shards=N_SHARDS, out_shards=2, name="ffn1_in",
                                rider=_gather_rider([shard_of(w_ffn1_out), shard_of(w_in)]))
    wf1o = rows_of(wf1o_g)
    g1 = _swiglu_fwd(z1, "ffn1_act")
    y1 = _matmul(g1, wf1o, form="nn", name="ffn1_out", tn=1024)
    tok1, h2 = _normmod(tok0, mods, nw1, k_shift=3, k_scale=4, n_xt=n_xt, name="mix_norm",
                        resid=(y1, 2, MACARON_WEIGHT))
    later = [w_branch_attn, w_branch_gmlp, w_ffn2_in]
    zx, (wba_g, wbg_g, wf2i) = _matmul(h2, win, form="nn", b_shards=N_SHARDS, name="mix_in",
                                       rider=_gather_rider([shard_of(w) for w in later]))
    wba, wbg = rows_of(wba_g), rows_of(wbg_g)
    qt, kt, vt = _qk_prep(zx, cos2, sin2, q_norm_w, k_norm_w, q_w=q_w, kv_w=kv_w, scale=scale, name="qk_prep")
    attn, lse = _flash_fwd(qt, kt, vt, S=S, name="attn_fwd")
    gm = _gmlp_fwd(zx, gmlp_ln_w, gmlp_ln_b, w_s, b_sb, S=S, col0=v_end, name="gmlp_fwd")
    a_br, (wo_g,) = _matmul(attn, wba, form="nn", name="branch_attn", tm=512, rider=_gather_rider([shard_of(w_out)]))
    wo = rows_of(wo_g)
    g_br = _matmul(gm, wbg, form="nn", name="branch_gmlp", tm=512)
    t_mix = _merge_fwd(zx, a_br, g_br, bg, S=S, col0=gv_end, name="merge_fwd")
    y_mix = _matmul(t_mix, wo, form="nn", name="mix_out", tm=512)
    x2, h3 = _normmod(tok1, mods, nw2, k_shift=6, k_scale=7, n_xt=n_xt, name="ffn2_norm", resid=(y_mix, 5, None),
                      rows=S)
    z2, (wf2o_g,) = _matmul(h3, wf2i, form="nn", b_shards=N_SHARDS, out_shards=2, name="ffn2_in",
                            rider=_gather_rider([shard_of(w_ffn2_out)]))
    wf2o = rows_of(wf2o_g)
    g2 = _swiglu_fwd(z2, "ffn2_act")
    y2 = _matmul(g2, wf2o, form="nn", name="ffn2_out", tn=1024)
    x3 = _resid_only(x2, y2, mods, k_gate=8, name="ffn2_resid")

    dx3, loss_row, dfw = _final_loss(x3, loss_target[0], fw, "loss")

    dy2, dgate8 = _resid_bwd(dx3, y2, mods, k_gate=8, weight=MACARON_WEIGHT, n_xt=n_xt, name="ffn2_resid_bwd")
    dz2 = _matmul(dy2, wf2o, form="nt", out_dtype=BF16, swiglu_z=z2, tm=512, name="ffn2_out_dgrad")
    gw_f2o = _matmul(g2, dy2, form="tn", out_dtype=BF16, name="ffn2_out_wgrad", tn=1024)
    dh3 = _matmul(dz2, wf2i, form="nt", a_shards=2, b_shards=N_SHARDS, name="ffn2_in_dgrad", tn=1024)
    gw_f2i = _matmul(h3, dz2, form="tn", out_dtype=BF16, b_shards=2, out_shards=N_SHARDS, name="ffn2_in_wgrad")
    dx2, dsh6, dsc7, dnw2 = _normmod_bwd(dh3, x2, dx3, mods, nw2, k_scale=7, n_xt=n_xt, name="ffn2_norm_bwd")

    dy_mix, dgate5 = _resid_bwd(dx2, y_mix, mods, k_gate=5, weight=None, n_xt=n_xt, name="mix_resid_bwd")
    dt = _matmul(dy_mix, wo, form="nt", name="mix_out_dgrad", tm=512)
    gw_wo = _matmul(t_mix, dy_mix, form="tn", out_dtype=BF16, name="mix_out_wgrad", tn=1024)
    d_abr, d_gbr, dl0, dl1, dbg0, dbg1 = _merge_bwd(dt, zx, a_br, g_br, bg, S=S, col0=gv_end, name="merge_bwd")
    d_attn = _matmul(d_abr, wba, form="nt", name="branch_attn_dgrad", tm=512)
    gw_wba = _matmul(attn, d_abr, form="tn", out_dtype=BF16, name="branch_attn_wgrad", tn=1024, tk=1024)
    d_gm = _matmul(d_gbr, wbg, form="nt", name="branch_gmlp_dgrad", tm=512)
    gw_wbg = _matmul(gm, d_gbr, form="tn", out_dtype=BF16, name="branch_gmlp_wgrad", tn=1024)
    dz_gm, dws, dbs_wide, dlnw, dlnb = _gmlp_bwd(zx, d_gm, gmlp_ln_w, gmlp_ln_b, w_s, b_sb, S=S, col0=v_end,
                                                 name="gmlp_bwd")
    delta = _attn_delta(d_attn, attn, G=N_KV_HEADS, name="attn_delta")
    dq, dk, dv = _flash_bwd(qt, kt, vt, d_attn, lse, delta, S=S, name="attn_bwd")
    dz_qkv, dqw, dkw = _qk_prep_bwd(zx, dq, dk, dv, cos2, sin2, q_norm_w, k_norm_w, q_w=q_w, kv_w=kv_w,
                                    scale=scale, n_xt=n_xt, name="qk_prep_bwd")
    ctx_pad = ((0, C), (0, 0))
    dzx = jnp.concatenate([dz_qkv, jnp.pad(dz_gm, ctx_pad), jnp.pad(dl0, ctx_pad), jnp.pad(dl1, ctx_pad)], axis=1)
    slabs_of = lambda g: g.reshape(N_SHARDS, g.shape[0] // N_SHARDS, g.shape[1])
    dh2, (rc_f2i,) = _matmul(dzx, win, form="nt", b_shards=N_SHARDS, name="mix_in_dgrad", tn=1024,
                             rider=_scatter_rider([gw_f2i]))
    gw_win, (rc_f2o, rc_wo, rc_wba, rc_wbg) = _matmul(
        h2, dzx, form="tn", out_dtype=BF16, out_shards=N_SHARDS, name="mix_in_wgrad",
        rider=_scatter_rider([slabs_of(g) for g in (gw_f2o, gw_wo, gw_wba, gw_wbg)]))
    dtok1, dsh3, dsc4, dnw1 = _normmod_bwd(dh2, tok1, dx2, mods, nw1, k_scale=4, n_xt=n_xt, name="mix_norm_bwd")

    dy1, dgate2 = _resid_bwd(dtok1, y1, mods, k_gate=2, weight=MACARON_WEIGHT, n_xt=n_xt, name="ffn1_resid_bwd")
    gw_f1o = _matmul(g1, dy1, form="tn", out_dtype=BF16, name="ffn1_out_wgrad", tn=1024)
    dz1, (rc_f1o,) = _matmul(dy1, wf1o, form="nt", out_dtype=BF16, swiglu_z=z1, tm=384, name="ffn1_out_dgrad",
                             rider=_scatter_rider([slabs_of(gw_f1o)]))
    gw_f1i, (rc_win,) = _matmul(h1, dz1, form="tn", out_dtype=BF16, b_shards=2, out_shards=N_SHARDS, name="ffn1_in_wgrad",
                                rider=_scatter_rider([gw_win]))
    dh1, (rc_f1i,) = _matmul(dz1, wf1i, form="nt", a_shards=2, b_shards=N_SHARDS, name="ffn1_in_dgrad", tn=1024,
                             rider=_scatter_rider([gw_f1i]))
    dtok0, dsh0, dsc1, dnw0 = _normmod_bwd(dh1, tok0, dtok1, mods, nw0, k_scale=1, n_xt=n_xt, name="ffn1_norm_bwd")
    grad_x = dtok0[:S][None]

    big_w = [w_ffn1_in, w_ffn2_in, w_in, w_ffn1_out, w_ffn2_out, w_branch_attn, w_branch_gmlp, w_out]
    big_m = [m_w_ffn1_in, m_w_ffn2_in, m_w_in, m_w_ffn1_out, m_w_ffn2_out, m_w_branch_attn, m_w_branch_gmlp, m_w_out]
    big_v = [v_w_ffn1_in, v_w_ffn2_in, v_w_in, v_w_ffn1_out, v_w_ffn2_out, v_w_branch_attn, v_w_branch_gmlp, v_w_out]
    big_names = ["w_ffn1_in", "w_ffn2_in", "w_in", "w_ffn1_out", "w_ffn2_out", "w_branch_attn", "w_branch_gmlp", "w_out"]
    received = [rc_f1i, rc_f2i, rc_win, rc_f1o, rc_f2o, rc_wba, rc_wbg, rc_wo]
    plane_sums = [_sum_slabs(r, "plane_sum_" + nm) for r, nm in zip(received, big_names)]
    sibling_sums = _sibling_exchange(plane_sums, "sibling_grads")
    big_out = {}
    for nm, w_, m_, v_, pa, pb in zip(big_names, big_w, big_m, big_v, plane_sums, sibling_sums):
        res = _adamw(w_[0], m_[0], v_[0], [pa, pb], "adamw_" + nm)
        big_out[nm] = [r[None] for r in res]

    zeros9 = jnp.zeros((1, D), F32)
    dmx = jnp.concatenate([dsh0[0], dsc1[0], dgate2[0], dsh3[0], dsc4[0], dgate5[0], dsh6[0], dsc7[0], dgate8[0]], axis=0)
    dmc = jnp.concatenate([dsh0[1], dsc1[1], dgate2[1], dsh3[1], dsc4[1], zeros9, zeros9, zeros9, zeros9], axis=0)
    dbs = jnp.sum(dbs_wide, axis=-1)
    parts = [dmx, dmc, jnp.concatenate([dnw0, dnw1, dnw2], axis=0), jnp.concatenate([dbg0, dbg1], axis=0),
             dqw, dkw, dlnw, dlnb, dws, dbs, dfw, loss_row[:, :1]]
    part_shapes = [p.shape for p in parts]
    small_all = _allgather_devices(_pack(parts, D), "gather_small")
    dmx_all = small_all[:, 0:N_MOD, :].reshape(N_DEV, N_MOD * D)
    small_sum = small_all[0]
    for d in range(1, N_DEV):
        small_sum = small_sum + small_all[d]
    (_, dmc_sum, g_nw, g_bg, g_qw, g_kw, g_lnw, g_lnb, g_ws, g_bs, g_fw, loss_sum) = _unpack(small_sum, part_shapes, D)
    loss = loss_sum[0, 0]
    dmx_sum = small_sum[0:N_MOD].reshape(1, N_MOD * D)
    g_b_mod = dmx_sum + dmc_sum.reshape(1, N_MOD * D)
    dm_rows = jnp.concatenate([dmx_all, jnp.pad(dmc_sum.reshape(1, N_MOD * D), ((0, 7), (0, 0)))], axis=0)
    dm_sh = lax.dynamic_slice_in_dim(dm_rows, shard * n_modsh, n_modsh, axis=1)
    g_w_mod = _mod_wgrad(cvecs, dm_sh, "mod_wgrad")
    dsc_part = _mod_dgrad(dm_sh, w_mod_l, "mod_dgrad")
    dsc_all = _allgather_devices(dsc_part, "gather_dsilu")[0::2, 8, :]
    dscc = ((dsc_all[0] + dsc_all[1]) + dsc_all[2]) + dsc_all[3]
    sg = _sigmoid(c_ctx)
    g_c_ctx = dscc * (sg * (1.0 + c_ctx * (1.0 - sg)))
    g_norm_w = lax.dynamic_slice_in_dim(g_nw, shard * n_vsh, n_vsh, axis=1)[None]
    g_b_gate = lax.dynamic_slice_in_dim(g_bg, shard * n_vsh, n_vsh, axis=1)[None]

    w_mod_out = [r[None] for r in _adamw(w_mod_l, m_w_mod[0], v_w_mod[0], [g_w_mod], "adamw_w_mod")]

    small_names = ["c_ctx", "b_mod", "norm_w", "b_gate", "q_norm_w", "k_norm_w", "gmlp_ln_w", "gmlp_ln_b",
                   "w_spatial", "b_spatial", "final_norm_w"]
    small_w = [c_ctx, b_mod, norm_w, b_gate, q_norm_w, k_norm_w, gmlp_ln_w, gmlp_ln_b, w_spatial, b_spatial, final_norm_w]
    small_m = [m_c_ctx, m_b_mod, m_norm_w, m_b_gate, m_q_norm_w, m_k_norm_w, m_gmlp_ln_w, m_gmlp_ln_b, m_w_spatial,
               m_b_spatial, m_final_norm_w]
    small_v = [v_c_ctx, v_b_mod, v_norm_w, v_b_gate, v_q_norm_w, v_k_norm_w, v_gmlp_ln_w, v_gmlp_ln_b, v_w_spatial,
               v_b_spatial, v_final_norm_w]
    small_g = [g_c_ctx, g_b_mod, g_norm_w, g_b_gate, g_qw, g_kw, g_lnw, g_lnb, g_ws, g_bs, g_fw]
    small_shapes = [w_.shape for w_ in small_w]
    packed = [_pack(group, D) for group in (small_w, small_m, small_v, small_g)]
    small_res = [_unpack(r, small_shapes, D) for r in _adamw(*packed[:3], [packed[3]], "adamw_small")]
    small_out = {nm: [small_res[t][i] for t in range(4)] for i, nm in enumerate(small_names)}

    order = ["c_ctx", "w_mod", "b_mod", "norm_w", "w_ffn1_in", "w_ffn1_out", "w_ffn2_in", "w_ffn2_out", "w_in", "b_gate",
             "q_norm_w", "k_norm_w", "gmlp_ln_w", "gmlp_ln_b", "w_spatial", "b_spatial", "w_branch_attn", "w_branch_gmlp",
             "w_out", "final_norm_w"]
    results = {**big_out, **small_out, "w_mod": w_mod_out}
    outs = [loss, grad_x]
    for t in range(4):
        outs += [results[nm][t] for nm in order]
    return tuple(outs)


def _resid_only(xp, y_in, mods, *, k_gate, name):
    M, D = xp.shape
    row = pl.BlockSpec((ROW_TILE, D), lambda i: (i, 0))

    def body(xp_ref, y_ref, g_ref, x_ref):
        x_ref[...] = xp_ref[...] + (MACARON_WEIGHT * g_ref[...]) * y_ref[...]

    return pl.pallas_call(
        body, name=name, grid=(M // ROW_TILE,), out_shape=jax.ShapeDtypeStruct((M, D), F32),
        in_specs=[row, row, pl.BlockSpec((None, 1, D), lambda i: (k_gate, 0, 0))], out_specs=row,
        compiler_params=_params("parallel"),
    )(xp, y_in, mods)
```

```python
import functools
import math

import jax
import jax.numpy as jnp
from jax import lax
from jax.experimental import pallas as pl
from jax.experimental.pallas import tpu as pltpu

F32 = jnp.float32
BF16 = jnp.bfloat16
MESH = pl.DeviceIdType.MESH
HBM_SPEC = pl.BlockSpec(memory_space=pltpu.HBM)
VMEM_SPEC = pl.BlockSpec(memory_space=pltpu.VMEM)

HEAD_DIM = 128
N_Q_HEADS = 16
N_KV_HEADS = 4
GMLP_GROUPS = 16
CHUNK = 128
GRID_W = 64
ROPE_THETA = 10000.0
EPS = 1e-6
MACARON_WEIGHT = 0.5
N_MOD = 9

ADAM_LR = 0.001
ADAM_B1 = 0.9
ADAM_B2 = 0.999
ADAM_EPS = 1e-08
ADAM_WD = 0.01
ADAM_STEP = 10

N_SHARDS = 4
N_DEV = 8
ROW_TILE = 256
KEY_TILES = (2816, 768, 512, 256, 128)
FWD_KEY_TILES = (8448,) + KEY_TILES
LANES = 128
VMEM_LIMIT_BYTES = 48 * 1024 * 1024
ADAM_TILE_BYTES = 1 << 20

NN = (((1,), (0,)), ((), ()))
NT = (((1,), (1,)), ((), ()))
TN = (((0,), (0,)), ((), ()))


def _pick(n, cands):
    for t in cands:
        if t <= n and n % t == 0:
            return t
    raise ValueError(f"no tile for {n} among {cands}")


def _params(*sem):
    return pltpu.CompilerParams(dimension_semantics=sem or None, vmem_limit_bytes=VMEM_LIMIT_BYTES)


def _sigmoid(x):
    return 1.0 / (1.0 + jnp.exp(-x))


def _matmul(a, b, *, form, name, out_dtype=F32, a_shards=1, b_shards=1, out_shards=1, tm=None, tn=None, tk=None,
            order="ji", rider=None, swiglu_z=None):
    if form == "nn":
        M, K = a.shape
        N = b.shape[-1] * b_shards
    elif form == "nt":
        M, K = a.shape[-2], a.shape[-1] * a_shards
        N = b.shape[-2]
    else:
        K, M = a.shape
        N = b.shape[-1] * b_shards
    n_b = N // b_shards if form in ("nn", "tn") else N
    n_o = N // out_shards
    k_a = K // a_shards
    k_b = K // b_shards if form == "nt" else K
    tm = _pick(M, ((tm,) if tm else ()) + (1024, 768, 512, 256, 128, 64, 32, 16))
    tn = _pick(math.gcd(n_b, n_o), ((tn,) if tn else ()) + (1408, 1024, 512, 256, 128))
    tk = tk or _pick(math.gcd(k_a, k_b),
                     (2816, 2048, 1024, 768, 512, 256, 128) if form == "tn" else
                     (2816, 2048, 1408, 1024, 768, 512, 256, 128))
    nk = K // tk
    nbb, nbo = n_b // tn, n_o // tn
    ka, kb = k_a // tk, k_b // tk
    dims = {"nn": NN, "nt": NT, "tn": TN}[form]

    def ij(g0, g1):
        return (g0, g1) if order == "ij" else (g1, g0)

    def a_map(g0, g1, k):
        i, _ = ij(g0, g1)
        if form == "tn":
            return (k, i)
        return (k // ka, i, k % ka) if a_shards > 1 else (i, k)

    def b_map(g0, g1, k):
        _, j = ij(g0, g1)
        if form == "nt":
            return (k // kb, j, k % kb) if b_shards > 1 else (j, k)
        return (j // nbb, k, j % nbb) if b_shards > 1 else (k, j)

    def o_map(g0, g1, k):
        i, j = ij(g0, g1)
        return (j // nbo, i, j % nbo) if out_shards > 1 else (i, j)

    a_block = (tk, tm) if form == "tn" else (tm, tk)
    if a_shards > 1:
        a_block = (None,) + a_block
    b_block = (tk, tn) if form in ("nn", "tn") else (tn, tk)
    if b_shards > 1:
        b_block = (None,) + b_block
    o_block = (None, tm, tn) if out_shards > 1 else (tm, tn)
    o_shape = (out_shards, M, n_o) if out_shards > 1 else (M, N)

    n_i, n_j = M // tm, N // tn
    grid = (n_i, n_j, nk) if order == "ij" else (n_j, n_i, nk)
    n_ride = len(rider.arrays) if rider else 0
    n_acc = 1 if nk > 1 else 0

    n_z = 0 if swiglu_z is None else 1
    if n_z:
        assert nk == 1 and out_shards == 1, "the swiglu epilogue needs the whole contraction in one step"

    def body(*refs):
        a_ref, b_ref = refs[:2]
        n_in = 2 + n_z
        ride_in = refs[n_in:n_in + n_ride]
        o_ref = refs[n_in + n_ride]
        ride_out = refs[n_in + 1 + n_ride:n_in + 1 + 2 * n_ride]
        scratch = refs[n_in + 1 + 2 * n_ride:]
        ride_sems = scratch[n_acc:]
        pid = [pl.program_id(d) for d in range(3)]
        if rider:
            @pl.when((pid[0] == 0) & (pid[1] == 0) & (pid[2] == 0))
            def _():
                rider.start(ride_in, ride_out, ride_sems)

        part = lax.dot_general(a_ref[...].astype(BF16), b_ref[...].astype(BF16), dims, preferred_element_type=F32)
        if n_z:
            z_ref = refs[2]
            za = z_ref[0]
            sig = _sigmoid(za)
            o_ref[0] = (part * z_ref[1] * (sig * (1.0 + za * (1.0 - sig)))).astype(o_ref.dtype)
            o_ref[1] = (part * (za * sig)).astype(o_ref.dtype)
        elif nk == 1:
            o_ref[...] = part.astype(o_ref.dtype)
        else:
            acc = scratch[0]
            k = pid[2]

            @pl.when(k == 0)
            def _():
                acc[...] = part

            @pl.when(k > 0)
            def _():
                acc[...] += part

            @pl.when(k == nk - 1)
            def _():
                o_ref[...] = acc[...].astype(o_ref.dtype)

        if rider:
            @pl.when((pid[0] == grid[0] - 1) & (pid[1] == grid[1] - 1) & (pid[2] == nk - 1))
            def _():
                rider.finish(ride_in, ride_out, ride_sems)

    main_shape = jax.ShapeDtypeStruct(o_shape, out_dtype)
    main_spec = pl.BlockSpec(o_block, o_map)
    z_specs = []
    if n_z:
        pair_map = lambda g0, g1, k: (0,) + ij(g0, g1)
        z_specs = [pl.BlockSpec((2, tm, tn), pair_map)]
        main_shape = jax.ShapeDtypeStruct((2, M, N), out_dtype)
        main_spec = pl.BlockSpec((2, tm, tn), pair_map)
    res = pl.pallas_call(
        body,
        name=name,
        out_shape=(main_shape, *rider.out_shapes) if rider else main_shape,
        grid=grid,
        in_specs=[pl.BlockSpec(a_block, a_map), pl.BlockSpec(b_block, b_map)] + z_specs + [HBM_SPEC] * n_ride,
        out_specs=(main_spec, *[HBM_SPEC] * n_ride) if rider else main_spec,
        scratch_shapes=([pltpu.VMEM((tm, tn), F32)] if nk > 1 else []) + (rider.sems if rider else []),
        compiler_params=_params(*(("arbitrary",) * 3 if rider else ("parallel", "parallel", "arbitrary"))),
    )(a, b, *([swiglu_z] if n_z else []), *(rider.arrays if rider else ()))
    return (res[0], list(res[1:])) if rider else res


def _type_of(i, n_xt, n_tiles):
    return jnp.where(i >= n_xt, 1, 0) if n_tiles > n_xt else 0


def _mod_spec(D, k, n_xt, n_tiles):
    return pl.BlockSpec((None, 1, D), lambda i: (_type_of(i, n_xt, n_tiles) * N_MOD + k, 0, 0))


def _acc_spec(D, n_xt, n_tiles):
    return pl.BlockSpec((None, 1, D), lambda i: (_type_of(i, n_xt, n_tiles), 0, 0))


def _accumulate(ref, value, first):
    @pl.when(first)
    def _():
        ref[...] = value

    @pl.when(jnp.logical_not(first))
    def _():
        ref[...] += value


def _normmod(xp, mods, nw, *, k_shift, k_scale, n_xt, name, resid=None, rows=None):
    M, D = rows or xp.shape[0], xp.shape[1]
    n_tiles = M // ROW_TILE
    row = pl.BlockSpec((ROW_TILE, D), lambda i: (i, 0))
    vec = pl.BlockSpec((1, D), lambda i: (0, 0))
    mod = functools.partial(_mod_spec, D, n_xt=n_xt, n_tiles=n_tiles)

    if resid is not None:
        y_in, k_gate, weight = resid

        def body(xp_ref, y_ref, g_ref, nw_ref, sh_ref, sc_ref, x_ref, h_ref):
            g = g_ref[...] if weight is None else weight * g_ref[...]
            x = xp_ref[...] + g * y_ref[...]
            x_ref[...] = x
            r = lax.rsqrt(jnp.mean(x * x, axis=-1, keepdims=True) + EPS)
            y = (x * r) * nw_ref[...]
            h_ref[...] = (y * (1.0 + sc_ref[...]) + sh_ref[...]).astype(BF16)

        return pl.pallas_call(
            body, name=name, grid=(n_tiles,),
            out_shape=(jax.ShapeDtypeStruct((M, D), F32), jax.ShapeDtypeStruct((M, D), BF16)),
            in_specs=[row, row, mod(k_gate), vec, mod(k_shift), mod(k_scale)],
            out_specs=(row, row), compiler_params=_params("parallel"),
        )(xp, y_in, mods, nw, mods, mods)

    def body(xp_ref, nw_ref, sh_ref, sc_ref, h_ref):
        x = xp_ref[...]
        r = lax.rsqrt(jnp.mean(x * x, axis=-1, keepdims=True) + EPS)
        y = (x * r) * nw_ref[...]
        h_ref[...] = (y * (1.0 + sc_ref[...]) + sh_ref[...]).astype(BF16)

    return pl.pallas_call(
        body, name=name, grid=(n_tiles,), out_shape=jax.ShapeDtypeStruct((M, D), BF16),
        in_specs=[row, vec, mod(k_shift), mod(k_scale)], out_specs=row, compiler_params=_params("parallel"),
    )(xp, nw, mods, mods)


def _resid_bwd(dxo, y_in, mods, *, k_gate, weight, n_xt, name):
    M, D = dxo.shape
    n_tiles = M // ROW_TILE
    n_types = 2 if n_tiles > n_xt else 1
    row = pl.BlockSpec((ROW_TILE, D), lambda i: (i, 0))

    def body(dxo_ref, y_ref, g_ref, dy_ref, dg_ref):
        i = pl.program_id(0)
        dxo_t = dxo_ref[...]
        g = g_ref[...] if weight is None else weight * g_ref[...]
        yw = y_ref[...] if weight is None else weight * y_ref[...]
        dy_ref[...] = (dxo_t * g).astype(BF16)
        _accumulate(dg_ref, jnp.sum(dxo_t * yw, axis=0, keepdims=True), (i == 0) | (i == n_xt))

    return pl.pallas_call(
        body, name=name, grid=(n_tiles,),
        out_shape=(jax.ShapeDtypeStruct((M, D), BF16), jax.ShapeDtypeStruct((n_types, 1, D), F32)),
        in_specs=[row, row, _mod_spec(D, k_gate, n_xt, n_tiles)],
        out_specs=(row, _acc_spec(D, n_xt, n_tiles)), compiler_params=_params("arbitrary"),
    )(dxo, y_in, mods)


def _normmod_bwd(dh, x_in, dxo, mods, nw, *, k_scale, n_xt, name):
    M, D = dh.shape
    n_tiles = M // ROW_TILE
    n_types = 2 if n_tiles > n_xt else 1
    dxo_tiles = dxo.shape[0] // ROW_TILE
    row = pl.BlockSpec((ROW_TILE, D), lambda i: (i, 0))
    dxo_row = pl.BlockSpec((ROW_TILE, D), lambda i: (jnp.minimum(i, dxo_tiles - 1), 0))
    vec = pl.BlockSpec((1, D), lambda i: (0, 0))

    def body(dh_ref, x_ref, dxo_ref, nw_ref, sc_ref, dx_ref, dsh_ref, dsc_ref, dnw_ref):
        i = pl.program_id(0)
        x = x_ref[...]
        dh_t = dh_ref[...]
        w = nw_ref[...]
        r = lax.rsqrt(jnp.mean(x * x, axis=-1, keepdims=True) + EPS)
        n = x * r
        dy = dh_t * (1.0 + sc_ref[...])
        dn = dy * w
        dx = r * (dn - n * jnp.mean(dn * n, axis=-1, keepdims=True))
        if dxo_tiles < n_tiles:
            dx_ref[...] = dx + jnp.where(i < dxo_tiles, dxo_ref[...], 0.0)
        else:
            dx_ref[...] = dx + dxo_ref[...]
        first = (i == 0) | (i == n_xt)
        _accumulate(dsh_ref, jnp.sum(dh_t, axis=0, keepdims=True), first)
        _accumulate(dsc_ref, jnp.sum(dh_t * (n * w), axis=0, keepdims=True), first)
        _accumulate(dnw_ref, jnp.sum(dy * n, axis=0, keepdims=True), i == 0)

    acc = _acc_spec(D, n_xt, n_tiles)
    return pl.pallas_call(
        body, name=name, grid=(n_tiles,),
        out_shape=(jax.ShapeDtypeStruct((M, D), F32), jax.ShapeDtypeStruct((n_types, 1, D), F32),
                   jax.ShapeDtypeStruct((n_types, 1, D), F32), jax.ShapeDtypeStruct((1, D), F32)),
        in_specs=[row, row, dxo_row, vec, _mod_spec(D, k_scale, n_xt, n_tiles)],
        out_specs=(row, acc, acc, vec), compiler_params=_params("arbitrary"),
    )(dh, x_in, dxo, nw, mods)


def _ffn_in(h, w, *, name, rider, tm):
    M, K = h.shape
    ns, _, n_sh = w.shape
    Fh = ns * n_sh // 2
    tm = _pick(M, (tm, 256, 128))
    tn = _pick(n_sh, (1408, 1024, 512, 256, 128))
    nb, nf = n_sh // tn, Fh // tn
    grid = (nf, M // tm)
    n_ride = len(rider.arrays)

    def body(*refs):
        h_ref, wa_ref, wb_ref = refs[:3]
        ride_in = refs[3:3 + n_ride]
        z_ref, g_ref = refs[3 + n_ride:5 + n_ride]
        ride_out = refs[5 + n_ride:5 + 2 * n_ride]
        ride_sems = refs[5 + 2 * n_ride:]
        j, i = pl.program_id(0), pl.program_id(1)

        @pl.when((j == 0) & (i == 0))
        def _():
            rider.start(ride_in, ride_out, ride_sems)

        h_t = h_ref[...]
        a = lax.dot_general(h_t, wa_ref[...], NN, preferred_element_type=F32)
        b = lax.dot_general(h_t, wb_ref[...], NN, preferred_element_type=F32)
        z_ref[0] = a
        z_ref[1] = b
        g_ref[...] = ((a * _sigmoid(a)) * b).astype(BF16)

        @pl.when((j == grid[0] - 1) & (i == grid[1] - 1))
        def _():
            rider.finish(ride_in, ride_out, ride_sems)

    res = pl.pallas_call(
        body, name=name, grid=grid,
        out_shape=(jax.ShapeDtypeStruct((2, M, Fh), F32), jax.ShapeDtypeStruct((M, Fh), BF16), *rider.out_shapes),
        in_specs=[pl.BlockSpec((tm, K), lambda j, i: (i, 0)),
                  pl.BlockSpec((None, K, tn), lambda j, i: (j // nb, 0, j % nb)),
                  pl.BlockSpec((None, K, tn), lambda j, i: ((j + nf) // nb, 0, (j + nf) % nb))] + [HBM_SPEC] * n_ride,
        out_specs=(pl.BlockSpec((2, tm, tn), lambda j, i: (0, i, j)), pl.BlockSpec((tm, tn), lambda j, i: (i, j)),
                   *[HBM_SPEC] * n_ride),
        scratch_shapes=rider.sems, compiler_params=_params("arbitrary", "arbitrary"),
    )(h, w, w, *rider.arrays)
    return res[0], res[1], list(res[2:])


def _final_loss(x3, target, fw, name):
    S, D = x3.shape
    row = pl.BlockSpec((ROW_TILE, D), lambda i: (i, 0))
    vec = pl.BlockSpec((1, D), lambda i: (0, 0))

    def body(x_ref, t_ref, fw_ref, dx_ref, loss_ref, dfw_ref):
        i = pl.program_id(0)
        x = x_ref[...]
        w = fw_ref[...]
        r = lax.rsqrt(jnp.mean(x * x, axis=-1, keepdims=True) + EPS)
        n = x * r
        err = n * w - t_ref[...]
        tile_loss = 0.5 * jnp.sum(jnp.mean(err * err, axis=-1, keepdims=True), axis=0, keepdims=True)
        dout = err / D
        dn = dout * w
        dx_ref[...] = r * (dn - n * jnp.mean(dn * n, axis=-1, keepdims=True))
        _accumulate(loss_ref, jnp.broadcast_to(tile_loss, (1, LANES)), i == 0)
        _accumulate(dfw_ref, jnp.sum(dout * n, axis=0, keepdims=True), i == 0)

    return pl.pallas_call(
        body, name=name, grid=(S // ROW_TILE,),
        out_shape=(jax.ShapeDtypeStruct((S, D), F32), jax.ShapeDtypeStruct((1, LANES), F32),
                   jax.ShapeDtypeStruct((1, D), F32)),
        in_specs=[row, row, vec],
        out_specs=(row, pl.BlockSpec((1, LANES), lambda i: (0, 0)), vec), compiler_params=_params("arbitrary"),
    )(x3, target, fw)


def _pair_swap(v):
    lane = lax.broadcasted_iota(jnp.int32, v.shape, v.ndim - 1)
    from_next = pltpu.roll(lane, 1, v.ndim - 1) == (lane ^ 1)
    return jnp.where(from_next, pltpu.roll(v, 1, v.ndim - 1), pltpu.roll(v, LANES - 1, v.ndim - 1))


def _qk_prep(zx, cos2, sin2, qw, kw, *, q_w, kv_w, scale, name):
    M = zx.shape[0]
    nqh, nkh = q_w // HEAD_DIM, kv_w // HEAD_DIM
    row = lambda w, c: pl.BlockSpec((ROW_TILE, w), lambda i: (i, c))
    vec = pl.BlockSpec((1, HEAD_DIM), lambda i: (0, 0))

    def rot(z, w, cos_t, sin_t):
        r = lax.rsqrt(jnp.mean(z * z, axis=-1, keepdims=True) + EPS)
        y = (z * r) * w
        return y * cos_t + _pair_swap(y) * sin_t

    def body(zq_ref, zk_ref, zv_ref, cos_ref, sin_ref, qw_ref, kw_ref, q_ref, k_ref, v_ref):
        cos_t, sin_t = cos_ref[...], sin_ref[...]
        for h in range(nqh):
            hs = slice(h * HEAD_DIM, (h + 1) * HEAD_DIM)
            q_ref[:, hs] = (rot(zq_ref[:, hs], qw_ref[...], cos_t, sin_t) * scale).astype(BF16)
        for h in range(nkh):
            hs = slice(h * HEAD_DIM, (h + 1) * HEAD_DIM)
            k_ref[:, hs] = rot(zk_ref[:, hs], kw_ref[...], cos_t, sin_t).astype(BF16)
        v_ref[...] = zv_ref[...].astype(BF16)

    return pl.pallas_call(
        body, name=name, grid=(M // ROW_TILE,),
        out_shape=(jax.ShapeDtypeStruct((M, q_w), BF16), jax.ShapeDtypeStruct((M, kv_w), BF16),
                   jax.ShapeDtypeStruct((M, kv_w), BF16)),
        in_specs=[row(q_w, 0), row(kv_w, q_w // kv_w), row(kv_w, q_w // kv_w + 1), row(HEAD_DIM, 0), row(HEAD_DIM, 0),
                  vec, vec],
        out_specs=(row(q_w, 0), row(kv_w, 0), row(kv_w, 0)), compiler_params=_params("parallel"),
    )(zx, zx, zx, cos2, sin2, qw, kw)


def _qk_prep_bwd(zx, dq, dk, dv, cos2, sin2, qw, kw, *, q_w, kv_w, scale, n_xt, name):
    M = zx.shape[0]
    nqh, nkh = q_w // HEAD_DIM, kv_w // HEAD_DIM
    dq_tiles = dq.shape[0] // ROW_TILE
    row = lambda w, c: pl.BlockSpec((ROW_TILE, w), lambda i: (i, c))
    vec = pl.BlockSpec((1, HEAD_DIM), lambda i: (0, 0))

    def unrot(z, d, w, cos_t, sin_t):
        r = lax.rsqrt(jnp.mean(z * z, axis=-1, keepdims=True) + EPS)
        n = z * r
        dy = d * cos_t - _pair_swap(d) * sin_t
        dn = dy * w
        dz = r * (dn - n * jnp.mean(dn * n, axis=-1, keepdims=True))
        return dz, jnp.sum(dy * n, axis=0, keepdims=True)

    def body(zq_ref, zk_ref, dq_ref, dk_ref, dv_ref, cos_ref, sin_ref, qw_ref, kw_ref, dz_ref, dqw_ref, dkw_ref):
        i = pl.program_id(0)
        cos_t, sin_t = cos_ref[...], sin_ref[...]
        is_x = i < n_xt
        dqw = jnp.zeros((1, HEAD_DIM), F32)
        dkw = jnp.zeros((1, HEAD_DIM), F32)
        for h in range(nqh):
            hs = slice(h * HEAD_DIM, (h + 1) * HEAD_DIM)
            d = jnp.where(is_x, dq_ref[:, hs], 0.0) * scale
            dz, dw = unrot(zq_ref[:, hs], d, qw_ref[...], cos_t, sin_t)
            dz_ref[:, hs] = dz.astype(BF16)
            dqw = dqw + dw
        for h in range(nkh):
            hs = slice(h * HEAD_DIM, (h + 1) * HEAD_DIM)
            dz, dw = unrot(zk_ref[:, hs], dk_ref[:, hs], kw_ref[...], cos_t, sin_t)
            dz_ref[:, q_w + h * HEAD_DIM:q_w + (h + 1) * HEAD_DIM] = dz.astype(BF16)
            dkw = dkw + dw
        dz_ref[:, q_w + kv_w:] = dv_ref[...].astype(BF16)
        _accumulate(dqw_ref, dqw, i == 0)
        _accumulate(dkw_ref, dkw, i == 0)

    return pl.pallas_call(
        body, name=name, grid=(M // ROW_TILE,),
        out_shape=(jax.ShapeDtypeStruct((M, q_w + 2 * kv_w), BF16), jax.ShapeDtypeStruct((1, HEAD_DIM), F32),
                   jax.ShapeDtypeStruct((1, HEAD_DIM), F32)),
        in_specs=[row(q_w, 0), row(kv_w, q_w // kv_w),
                  pl.BlockSpec((ROW_TILE, q_w), lambda i: (jnp.minimum(i, dq_tiles - 1), 0)),
                  row(kv_w, 0), row(kv_w, 0), row(HEAD_DIM, 0), row(HEAD_DIM, 0), vec, vec],
        out_specs=(row(q_w + 2 * kv_w, 0), vec, vec), compiler_params=_params("arbitrary"),
    )(zx, zx, dq, dk, dv, cos2, sin2, qw, kw)


def _lane_pick(tile, h):
    lane = lax.broadcasted_iota(jnp.int32, tile.shape, 1)
    return jnp.sum(jnp.where(lane == h, tile, 0.0), axis=-1, keepdims=True)


def _flash_fwd(q, k, v, *, S, name):
    NK, kv_w = k.shape
    G = kv_w // HEAD_DIM
    qpk = q.shape[1] // kv_w
    gw = qpk * HEAD_DIM
    tq = _pick(S, (512, 256, 128))
    tk = _pick(NK, FWD_KEY_TILES)
    nk = NK // tk

    def body(q_ref, k_ref, v_ref, o_ref, lse_ref, m_s, l_s, acc_s):
        ki = pl.program_id(2)

        @pl.when(ki == 0)
        def _():
            m_s[...] = jnp.full(m_s.shape, -1e30, F32)
            l_s[...] = jnp.zeros(l_s.shape, F32)
            acc_s[...] = jnp.zeros(acc_s.shape, F32)

        k_t, v_t = k_ref[...], v_ref[...]
        for h in range(qpk):
            s = lax.dot_general(q_ref[:, h * HEAD_DIM:(h + 1) * HEAD_DIM], k_t, NT, preferred_element_type=F32)
            m_prev = m_s[h]
            m_new = jnp.maximum(m_prev, jnp.max(s, axis=-1, keepdims=True))
            alpha = jnp.exp(m_prev - m_new)
            p = jnp.exp(s - m_new)
            l_s[h] = alpha * l_s[h] + jnp.sum(p, axis=-1, keepdims=True)
            acc_s[h] = alpha * acc_s[h] + lax.dot_general(p.astype(BF16), v_t, NN, preferred_element_type=F32)
            m_s[h] = m_new

        @pl.when(ki == nk - 1)
        def _():
            lane = lax.broadcasted_iota(jnp.int32, (tq, LANES), 1)
            lse = jnp.zeros((tq, LANES), F32)
            for h in range(qpk):
                l = l_s[h]
                o_ref[:, h * HEAD_DIM:(h + 1) * HEAD_DIM] = acc_s[h] / l
                lse = jnp.where(lane == h, m_s[h] + jnp.log(l), lse)
            lse_ref[...] = lse

    return pl.pallas_call(
        body, name=name, grid=(G, S // tq, nk),
        out_shape=(jax.ShapeDtypeStruct((S, G * gw), F32), jax.ShapeDtypeStruct((G, S, LANES), F32)),
        in_specs=[pl.BlockSpec((tq, gw), lambda g, i, j: (i, g)),
                  pl.BlockSpec((tk, HEAD_DIM), lambda g, i, j: (j, g)),
                  pl.BlockSpec((tk, HEAD_DIM), lambda g, i, j: (j, g))],
        out_specs=(pl.BlockSpec((tq, gw), lambda g, i, j: (i, g)),
                   pl.BlockSpec((None, tq, LANES), lambda g, i, j: (g, i, 0))),
        scratch_shapes=[pltpu.VMEM((qpk, tq, 1), F32), pltpu.VMEM((qpk, tq, 1), F32),
                        pltpu.VMEM((qpk, tq, HEAD_DIM), F32)],
        compiler_params=_params("parallel", "parallel", "arbitrary"),
    )(q, k, v)


def _attn_delta(do, o, *, G, name):
    S, q_w = o.shape
    gw = q_w // G
    qpk = gw // HEAD_DIM
    tq = _pick(S, (512, 256, 128))

    def body(do_ref, o_ref, d_ref):
        lane = lax.broadcasted_iota(jnp.int32, (tq, LANES), 1)
        out = jnp.zeros((tq, LANES), F32)
        for h in range(qpk):
            hs = slice(h * HEAD_DIM, (h + 1) * HEAD_DIM)
            out = jnp.where(lane == h, jnp.sum(do_ref[:, hs] * o_ref[:, hs], axis=-1, keepdims=True), out)
        d_ref[...] = out

    return pl.pallas_call(
        body, name=name, grid=(G, S // tq), out_shape=jax.ShapeDtypeStruct((G, S, LANES), F32),
        in_specs=[pl.BlockSpec((tq, gw), lambda g, i: (i, g)), pl.BlockSpec((tq, gw), lambda g, i: (i, g))],
        out_specs=pl.BlockSpec((None, tq, LANES), lambda g, i: (g, i, 0)),
        compiler_params=_params("parallel", "parallel"),
    )(do, o)


def _flash_bwd(q, k, v, do, lse, delta, *, S, name):
    NK, kv_w = k.shape
    G = kv_w // HEAD_DIM
    qpk = q.shape[1] // kv_w
    gw = qpk * HEAD_DIM
    tq = _pick(S, (512, 256, 128))
    tk = _pick(NK, KEY_TILES)

    def body(q_ref, k_ref, v_ref, do_ref, lse_ref, dl_ref, dq_ref, dk_ref, dv_ref, lse_s, dl_s):
        qi, ki = pl.program_id(1), pl.program_id(2)

        @pl.when((qi == 0) & (ki == 0))
        def _():
            dk_ref[...] = jnp.zeros(dk_ref.shape, F32)
            dv_ref[...] = jnp.zeros(dv_ref.shape, F32)

        @pl.when(ki == 0)
        def _():
            dq_ref[...] = jnp.zeros(dq_ref.shape, F32)
            for h in range(qpk):
                lse_s[h] = _lane_pick(lse_ref[...], h)
                dl_s[h] = _lane_pick(dl_ref[...], h)

        k_t, v_t = k_ref[...], v_ref[...]
        dk_acc = jnp.zeros((tk, HEAD_DIM), F32)
        dv_acc = jnp.zeros((tk, HEAD_DIM), F32)
        for h in range(qpk):
            hs = slice(h * HEAD_DIM, (h + 1) * HEAD_DIM)
            q_h = q_ref[:, hs]
            do_h = do_ref[:, hs].astype(BF16)
            s = lax.dot_general(q_h, k_t, NT, preferred_element_type=F32)
            p = jnp.exp(s - lse_s[h])
            dv_acc = dv_acc + lax.dot_general(p.astype(BF16), do_h, TN, preferred_element_type=F32)
            dp = lax.dot_general(do_h, v_t, NT, preferred_element_type=F32)
            ds = (p * (dp - dl_s[h])).astype(BF16)
            dq_ref[:, hs] += lax.dot_general(ds, k_t, NN, preferred_element_type=F32)
            dk_acc = dk_acc + lax.dot_general(ds, q_h, TN, preferred_element_type=F32)
        rows = pl.ds(pl.multiple_of(ki * tk, tk), tk)
        dk_ref[rows, :] += dk_acc
        dv_ref[rows, :] += dv_acc

    qspec = pl.BlockSpec((tq, gw), lambda g, i, j: (i, g))
    kspec = pl.BlockSpec((tk, HEAD_DIM), lambda g, i, j: (j, g))
    lspec = pl.BlockSpec((None, tq, LANES), lambda g, i, j: (g, i, 0))
    group = pl.BlockSpec((NK, HEAD_DIM), lambda g, i, j: (0, g))
    kv_shape = jax.ShapeDtypeStruct((NK, kv_w), F32)
    return pl.pallas_call(
        body, name=name, grid=(G, S // tq, NK // tk),
        out_shape=(jax.ShapeDtypeStruct((S, G * gw), F32), kv_shape, kv_shape),
        in_specs=[qspec, kspec, kspec, qspec, lspec, lspec], out_specs=(qspec, group, group),
        scratch_shapes=[pltpu.VMEM((qpk, tq, 1), F32), pltpu.VMEM((qpk, tq, 1), F32)],
        compiler_params=_params("arbitrary", "arbitrary", "arbitrary"),
    )(q, k, v, do, lse, delta)


def _wide_specs(col0, width, bw):
    return [pl.BlockSpec((ROW_TILE, bw), functools.partial(lambda i, c: (i, c), c=col0 // bw + p))
            for p in range(width // bw)]


def _cat(refs):
    return refs[0][...] if len(refs) == 1 else jnp.concatenate([r[...] for r in refs], axis=1)


def _gelu(x):
    return 0.5 * x * (1.0 + lax.erf(x * (1.0 / math.sqrt(2.0))))


def _gelu_grad(x):
    return 0.5 * (1.0 + lax.erf(x * (1.0 / math.sqrt(2.0)))) + x * jnp.exp(-0.5 * x * x) * (1.0 / math.sqrt(2.0 * math.pi))


def _layernorm_stats(v):
    mu = jnp.mean(v, axis=-1, keepdims=True)
    xc = v - mu
    rstd = lax.rsqrt(jnp.mean(xc * xc, axis=-1, keepdims=True) + EPS)
    return xc * rstd, rstd


def _gmlp_fwd(zx, ln_w, ln_b, w_s, b_sb, *, S, col0, name):
    G, W = w_s.shape[0], ln_w.shape[1]
    gd = W // G
    bw = math.gcd(col0, W)
    n_parts = W // bw
    vec = pl.BlockSpec((1, W), lambda i: (0, 0))
    full3 = pl.BlockSpec((G, CHUNK, CHUNK), lambda i: (0, 0, 0))
    full3b = pl.BlockSpec((G, CHUNK, gd), lambda i: (0, 0, 0))

    def body(*refs):
        u_refs, v_refs = refs[:n_parts], refs[n_parts:2 * n_parts]
        lnw_ref, lnb_ref, ws_ref, bs_ref, gm_ref = refs[2 * n_parts:]
        u = _gelu(_cat(u_refs))
        vhat, _ = _layernorm_stats(_gelu(_cat(v_refs)))
        vn = (vhat * lnw_ref[...] + lnb_ref[...]).astype(BF16)
        for c in range(ROW_TILE // CHUNK):
            rs = slice(c * CHUNK, (c + 1) * CHUNK)
            for g in range(G):
                cs = slice(g * gd, (g + 1) * gd)
                mixed = lax.dot_general(ws_ref[g].astype(BF16), vn[rs, cs], NN, preferred_element_type=F32) + bs_ref[g]
                gm_ref[rs, cs] = (u[rs, cs] * mixed).astype(BF16)

    return pl.pallas_call(
        body, name=name, grid=(S // ROW_TILE,), out_shape=jax.ShapeDtypeStruct((S, W), BF16),
        in_specs=_wide_specs(col0, W, bw) + _wide_specs(col0 + W, W, bw) + [vec, vec, full3, full3b],
        out_specs=pl.BlockSpec((ROW_TILE, W), lambda i: (i, 0)), compiler_params=_params("parallel"),
    )(*([zx] * (2 * n_parts)), ln_w, ln_b, w_s, b_sb)


def _gmlp_bwd(zx, dgm, ln_w, ln_b, w_s, b_sb, *, S, col0, name):
    G, W = w_s.shape[0], ln_w.shape[1]
    gd = W // G
    bw = math.gcd(col0, W)
    n_parts = W // bw
    vec = pl.BlockSpec((1, W), lambda i: (0, 0))
    full3 = pl.BlockSpec((G, CHUNK, CHUNK), lambda i: (0, 0, 0))
    full3b = pl.BlockSpec((G, CHUNK, gd), lambda i: (0, 0, 0))
    row = pl.BlockSpec((ROW_TILE, W), lambda i: (i, 0))

    def body(*refs):
        u_refs, v_refs = refs[:n_parts], refs[n_parts:2 * n_parts]
        dgm_ref, lnw_ref, lnb_ref, ws_ref, bs_ref, dz_ref, dws_ref, dbs_ref, dlnw_ref, dlnb_ref, du_s, dvn_s = refs[2 * n_parts:]
        i = pl.program_id(0)
        upre, vpre = _cat(u_refs), _cat(v_refs)
        u = _gelu(upre)
        vhat, rstd = _layernorm_stats(_gelu(vpre))
        lnw = lnw_ref[...]
        vn = (vhat * lnw + lnb_ref[...]).astype(BF16)
        dgm_t = dgm_ref[...]

        @pl.when(i == 0)
        def _():
            dws_ref[...] = jnp.zeros(dws_ref.shape, F32)
            dbs_ref[...] = jnp.zeros(dbs_ref.shape, F32)

        for c in range(ROW_TILE // CHUNK):
            rs = slice(c * CHUNK, (c + 1) * CHUNK)
            for g in range(G):
                cs = slice(g * gd, (g + 1) * gd)
                ws_g = ws_ref[g].astype(BF16)
                vn_cg = vn[rs, cs]
                mixed = lax.dot_general(ws_g, vn_cg, NN, preferred_element_type=F32) + bs_ref[g]
                dgm_cg = dgm_t[rs, cs]
                du_s[rs, cs] = dgm_cg * mixed
                dmixed = dgm_cg * u[rs, cs]
                dmixed_b = dmixed.astype(BF16)
                dws_ref[g] += lax.dot_general(dmixed_b, vn_cg, NT, preferred_element_type=F32)
                dbs_ref[g] += dmixed
                dvn_s[rs, cs] = lax.dot_general(ws_g, dmixed_b, TN, preferred_element_type=F32)

        dvn = dvn_s[...]
        _accumulate(dlnw_ref, jnp.sum(dvn * vhat, axis=0, keepdims=True), i == 0)
        _accumulate(dlnb_ref, jnp.sum(dvn, axis=0, keepdims=True), i == 0)
        dvhat = dvn * lnw
        dv = rstd * (dvhat - jnp.mean(dvhat, axis=-1, keepdims=True)
                     - vhat * jnp.mean(dvhat * vhat, axis=-1, keepdims=True))
        dz_ref[:, :W] = (du_s[...] * _gelu_grad(upre)).astype(BF16)
        dz_ref[:, W:] = (dv * _gelu_grad(vpre)).astype(BF16)

    return pl.pallas_call(
        body, name=name, grid=(S // ROW_TILE,),
        out_shape=(jax.ShapeDtypeStruct((S, 2 * W), BF16), jax.ShapeDtypeStruct((G, CHUNK, CHUNK), F32),
                   jax.ShapeDtypeStruct((G, CHUNK, gd), F32), jax.ShapeDtypeStruct((1, W), F32),
                   jax.ShapeDtypeStruct((1, W), F32)),
        in_specs=_wide_specs(col0, W, bw) + _wide_specs(col0 + W, W, bw) + [row, vec, vec, full3, full3b],
        out_specs=(pl.BlockSpec((ROW_TILE, 2 * W), lambda i: (i, 0)), full3, full3b, vec, vec),
        scratch_shapes=[pltpu.VMEM((ROW_TILE, W), F32), pltpu.VMEM((ROW_TILE, W), F32)],
        compiler_params=_params("arbitrary"),
    )(*([zx] * (2 * n_parts)), dgm, ln_w, ln_b, w_s, b_sb)


def _merge_fwd(zx, a_br, g_br, b_gate, *, S, col0, name):
    D = a_br.shape[1]
    cw = min(math.gcd(col0, D), 1024)
    nc = D // cw
    c0 = col0 // cw
    blk = lambda f: pl.BlockSpec((ROW_TILE, cw), f)
    bias = lambda t: pl.BlockSpec((None, 1, cw), lambda i, j: (t, 0, j))

    def body(l0_ref, l1_ref, a_ref, g_ref, b0_ref, b1_ref, t_ref):
        g0 = _sigmoid(l0_ref[...] + b0_ref[...])
        g1 = _sigmoid(l1_ref[...] + b1_ref[...])
        t_ref[...] = (g0 * a_ref[...] + g1 * g_ref[...]).astype(BF16)

    return pl.pallas_call(
        body, name=name, grid=(S // ROW_TILE, nc), out_shape=jax.ShapeDtypeStruct((S, D), BF16),
        in_specs=[blk(lambda i, j: (i, c0 + j)), blk(lambda i, j: (i, c0 + nc + j)), blk(lambda i, j: (i, j)),
                  blk(lambda i, j: (i, j)), bias(0), bias(1)],
        out_specs=blk(lambda i, j: (i, j)), compiler_params=_params("parallel", "parallel"),
    )(zx, zx, a_br, g_br, b_gate, b_gate)


def _merge_bwd(dt, zx, a_br, g_br, b_gate, *, S, col0, name):
    D = a_br.shape[1]
    cw = min(math.gcd(col0, D), 1024)
    nc = D // cw
    c0 = col0 // cw
    blk = lambda f: pl.BlockSpec((ROW_TILE, cw), f)
    bias = lambda t: pl.BlockSpec((None, 1, cw), lambda j, i: (t, 0, j))
    own = blk(lambda j, i: (i, j))
    acc = pl.BlockSpec((1, cw), lambda j, i: (0, j))

    def body(dt_ref, l0_ref, l1_ref, a_ref, g_ref, b0_ref, b1_ref, da_ref, dg_ref, dl0_ref, dl1_ref, db0_ref, db1_ref):
        i = pl.program_id(1)
        dt_t = dt_ref[...]
        g0 = _sigmoid(l0_ref[...] + b0_ref[...])
        g1 = _sigmoid(l1_ref[...] + b1_ref[...])
        da_ref[...] = (dt_t * g0).astype(BF16)
        dg_ref[...] = (dt_t * g1).astype(BF16)
        dl0 = dt_t * a_ref[...] * (g0 * (1.0 - g0))
        dl1 = dt_t * g_ref[...] * (g1 * (1.0 - g1))
        dl0_ref[...] = dl0.astype(BF16)
        dl1_ref[...] = dl1.astype(BF16)
        _accumulate(db0_ref, jnp.sum(dl0, axis=0, keepdims=True), i == 0)
        _accumulate(db1_ref, jnp.sum(dl1, axis=0, keepdims=True), i == 0)

    sd = lambda dt_: jax.ShapeDtypeStruct((S, D), dt_)
    return pl.pallas_call(
        body, name=name, grid=(nc, S // ROW_TILE),
        out_shape=(sd(BF16), sd(BF16), sd(BF16), sd(BF16), jax.ShapeDtypeStruct((1, D), F32),
                   jax.ShapeDtypeStruct((1, D), F32)),
        in_specs=[own, blk(lambda j, i: (i, c0 + j)), blk(lambda j, i: (i, c0 + nc + j)), own, own, bias(0), bias(1)],
        out_specs=(own, own, own, own, acc, acc), compiler_params=_params("parallel", "arbitrary"),
    )(dt, zx, zx, a_br, g_br, b_gate, b_gate)


def _mod_fwd(cvecs, w_mod, name):
    R, D = cvecs.shape
    nsh = w_mod.shape[1]
    tn = _pick(nsh, (512, 256, 128))

    def body(c_ref, w_ref, o_ref):
        cv = c_ref[...]
        a = (cv * _sigmoid(cv)).astype(BF16)
        o_ref[...] = lax.dot_general(a, w_ref[...].astype(BF16), NN, preferred_element_type=F32)

    return pl.pallas_call(
        body, name=name, grid=(nsh // tn,), out_shape=jax.ShapeDtypeStruct((R, nsh), F32),
        in_specs=[pl.BlockSpec((R, D), lambda j: (0, 0)), pl.BlockSpec((D, tn), lambda j: (0, j))],
        out_specs=pl.BlockSpec((R, tn), lambda j: (0, j)), compiler_params=_params("parallel"),
    )(cvecs, w_mod)


def _mod_wgrad(cvecs, dm, name):
    R, D = cvecs.shape
    nsh = dm.shape[1]
    tn = _pick(nsh, (512, 256, 128))

    def body(c_ref, dm_ref, o_ref):
        cv = c_ref[...]
        a = (cv * _sigmoid(cv)).astype(BF16)
        o_ref[...] = lax.dot_general(a, dm_ref[...].astype(BF16), TN, preferred_element_type=F32)

    return pl.pallas_call(
        body, name=name, grid=(nsh // tn,), out_shape=jax.ShapeDtypeStruct((D, nsh), F32),
        in_specs=[pl.BlockSpec((R, D), lambda j: (0, 0)), pl.BlockSpec((R, tn), lambda j: (0, j))],
        out_specs=pl.BlockSpec((D, tn), lambda j: (0, j)), compiler_params=_params("parallel"),
    )(cvecs, dm)


def _mod_dgrad(dm, w_mod, name):
    R, nsh = dm.shape
    D = w_mod.shape[0]
    tn = _pick(D, (256, 128))

    def body(dm_ref, w_ref, o_ref):
        o_ref[...] = lax.dot_general(dm_ref[...].astype(BF16), w_ref[...].astype(BF16), NT, preferred_element_type=F32)

    return pl.pallas_call(
        body, name=name, grid=(D // tn,), out_shape=jax.ShapeDtypeStruct((R, D), F32),
        in_specs=[pl.BlockSpec((R, nsh), lambda j: (0, 0)), pl.BlockSpec((tn, nsh), lambda j: (j, 0))],
        out_specs=pl.BlockSpec((R, tn), lambda j: (0, j)), compiler_params=_params("parallel"),
    )(dm, w_mod)


def _adam_rows(R, C):
    return _pick(R, [t for t in (512, 256, 128, 64, 32, 16) if t * C * 4 <= ADAM_TILE_BYTES] or [16])


def _adamw(w, m, v, grads, name):
    R, C = w.shape
    tr = _adam_rows(R, C)
    n_g = len(grads)
    blk = pl.BlockSpec((tr, C), lambda i: (i, 0))
    c1 = 1.0 - ADAM_B1 ** ADAM_STEP
    c2 = 1.0 - ADAM_B2 ** ADAM_STEP

    def body(*refs):
        w_ref, m_ref, v_ref = refs[:3]
        g_refs = refs[3:3 + n_g]
        g_ref, d_ref, m2_ref, v2_ref = refs[3 + n_g:]
        g = g_refs[0][...].astype(F32)
        for r in g_refs[1:]:
            g = g + r[...].astype(F32)
        m2 = ADAM_B1 * m_ref[...] + (1.0 - ADAM_B1) * g
        v2 = ADAM_B2 * v_ref[...] + (1.0 - ADAM_B2) * (g * g)
        g_ref[...] = g
        m2_ref[...] = m2
        v2_ref[...] = v2
        d_ref[...] = -ADAM_LR * ((m2 / c1) / (jnp.sqrt(v2 / c2) + ADAM_EPS) + ADAM_WD * w_ref[...])

    out = jax.ShapeDtypeStruct((R, C), F32)
    return pl.pallas_call(
        body, name=name, grid=(R // tr,), out_shape=(out, out, out, out),
        in_specs=[blk] * (3 + n_g), out_specs=(blk, blk, blk, blk), compiler_params=_params("parallel"),
    )(w, m, v, *grads)


def _sum_slabs(slabs, name):
    n, R, C = slabs.shape
    tr = _adam_rows(R, C)

    def body(s_ref, o_ref):
        acc = s_ref[0].astype(F32)
        for k in range(1, n):
            acc = acc + s_ref[k].astype(F32)
        o_ref[...] = acc.astype(BF16)

    return pl.pallas_call(
        body, name=name, grid=(R // tr,), out_shape=jax.ShapeDtypeStruct((R, C), BF16),
        in_specs=[pl.BlockSpec((n, tr, C), lambda i: (0, i, 0))],
        out_specs=pl.BlockSpec((tr, C), lambda i: (i, 0)), compiler_params=_params("parallel"),
    )(slabs)


def _plane_peers():
    x, y = lax.axis_index("x"), lax.axis_index("y")
    return [(1 - x, y), (x, 1 - y), (1 - x, 1 - y)]


class _Rider:
    def __init__(self, arrays, out_shapes, copies, relay=None):
        n = len(arrays)
        self.arrays, self.out_shapes = list(arrays), list(out_shapes)
        self.sems = [pltpu.SemaphoreType.DMA((3 * n,)), pltpu.SemaphoreType.DMA((3 * n,)), pltpu.SemaphoreType.DMA((n,))]
        if relay:
            self.sems += [pltpu.SemaphoreType.DMA((3 * n,)), pltpu.SemaphoreType.DMA((3 * n,))]
        self._copies, self._relay = copies, relay

    def start(self, ins, outs, sems):
        local, sends, _ = self._copies(ins, outs, sems)
        for cp in local + sends:
            cp.start()

    def finish(self, ins, outs, sems):
        local, sends, recvs = self._copies(ins, outs, sems)
        passed, landed = self._relay(ins, outs, sems) if self._relay else ([None] * len(recvs), [])
        for cp, on in zip(recvs, passed):
            cp.wait_recv()
            if on is not None:
                on.start()
        for cp in landed:
            cp.wait_recv()
        for cp in sends + [on for on in passed if on is not None]:
            cp.wait_send()
        for cp in local:
            cp.wait()


def _gather_rider(shards):
    n = len(shards)
    half_rows = [s.shape[0] // 2 for s in shards]

    def half(ref, w, hc):
        return ref.at[pl.ds(pl.multiple_of(hc * half_rows[w], 16), half_rows[w]), :]

    def copies(ins, outs, sems):
        send_sems, recv_sems, local_sems = sems[:3]
        x, y, c = lax.axis_index("x"), lax.axis_index("y"), lax.axis_index("c")
        me = 2 * x + y
        peers = _plane_peers()

        def remote(w, k, slab):
            px, py = peers[k]
            return pltpu.make_async_remote_copy(
                src_ref=half(ins[w], w, c), dst_ref=half(outs[w].at[slab], w, c), send_sem=send_sems.at[3 * w + k],
                recv_sem=recv_sems.at[3 * w + k], device_id=(px, py, c), device_id_type=MESH)

        local = [pltpu.make_async_copy(ins[w], outs[w].at[me], local_sems.at[w]) for w in range(n)]
        sends = [remote(w, k, me) for w in range(n) for k in range(3)]
        recvs = [remote(w, k, 2 * px + py) for w in range(n) for k, (px, py) in enumerate(peers)]
        return local, sends, recvs

    def relay(ins, outs, sems):
        send_sems, recv_sems = sems[3:]
        x, y, c = lax.axis_index("x"), lax.axis_index("y"), lax.axis_index("c")
        peers = _plane_peers()

        def sibling(w, k, hc):
            px, py = peers[k]
            part = half(outs[w].at[2 * px + py], w, hc)
            return pltpu.make_async_remote_copy(
                src_ref=part, dst_ref=part, send_sem=send_sems.at[3 * w + k], recv_sem=recv_sems.at[3 * w + k],
                device_id=(x, y, 1 - c), device_id_type=MESH)

        passed = [sibling(w, k, c) for w in range(n) for k in range(3)]
        landed = [sibling(w, k, 1 - c) for w in range(n) for k in range(3)]
        return passed, landed

    return _Rider(shards, [jax.ShapeDtypeStruct((N_SHARDS,) + s.shape, s.dtype) for s in shards], copies, relay)


def _scatter_rider(fulls):
    n = len(fulls)

    def copies(ins, outs, sems):
        send_sems, recv_sems, local_sems = sems
        x, y, c = lax.axis_index("x"), lax.axis_index("y"), lax.axis_index("c")
        me = 2 * x + y
        peers = _plane_peers()

        def remote(w, k):
            px, py = peers[k]
            return pltpu.make_async_remote_copy(
                src_ref=ins[w].at[2 * px + py], dst_ref=outs[w].at[k], send_sem=send_sems.at[3 * w + k],
                recv_sem=recv_sems.at[3 * w + k], device_id=(px, py, c), device_id_type=MESH)

        local = [pltpu.make_async_copy(ins[w].at[me], outs[w].at[3], local_sems.at[w]) for w in range(n)]
        sends = [remote(w, k) for w in range(n) for k in range(3)]
        return local, sends, sends

    return _Rider(fulls, [jax.ShapeDtypeStruct(f.shape, f.dtype) for f in fulls], copies)


def _comm_call(rider, name):
    n = len(rider.arrays)

    def body(*refs):
        ins, outs, sems = refs[:n], refs[n:2 * n], refs[2 * n:]
        rider.start(ins, outs, sems)
        rider.finish(ins, outs, sems)

    return list(pl.pallas_call(
        body, name=name, out_shape=tuple(rider.out_shapes), in_specs=[HBM_SPEC] * n, out_specs=tuple([HBM_SPEC] * n),
        scratch_shapes=rider.sems,
    )(*rider.arrays))


def _sibling_exchange(blocks, name):
    n = len(blocks)

    def body(*refs):
        ins, outs = refs[:n], refs[n:2 * n]
        send_sems, recv_sems = refs[2 * n:]
        sibling = (lax.axis_index("x"), lax.axis_index("y"), 1 - lax.axis_index("c"))
        sends = [pltpu.make_async_remote_copy(src_ref=ins[w], dst_ref=outs[w], send_sem=send_sems.at[w],
                                              recv_sem=recv_sems.at[w], device_id=sibling, device_id_type=MESH)
                 for w in range(n)]
        for cp in sends:
            cp.start()
        for cp in sends:
            cp.wait_recv()
        for cp in sends:
            cp.wait_send()

    return pl.pallas_call(
        body, name=name, out_shape=tuple(jax.ShapeDtypeStruct(b.shape, b.dtype) for b in blocks),
        in_specs=[HBM_SPEC] * n, out_specs=tuple([HBM_SPEC] * n),
        scratch_shapes=[pltpu.SemaphoreType.DMA((n,)), pltpu.SemaphoreType.DMA((n,))],
    )(*blocks)


def _allgather_devices(block, name):
    m_per, n_cols = block.shape

    def body(x_ref, out_ref, send_sems, recv_sems, local_sem):
        x, y, c = lax.axis_index("x"), lax.axis_index("y"), lax.axis_index("c")
        me, sibling = (x, y, c), (x, y, 1 - c)
        chips = _plane_peers()

        def rows(px, py, pc):
            return out_ref.at[pl.ds((4 * px + 2 * py + pc) * m_per, m_per), :]

        def copy(k, blk, to, src=None):
            return pltpu.make_async_remote_copy(
                src_ref=rows(*blk) if src is None else src, dst_ref=rows(*blk), send_sem=send_sems.at[k],
                recv_sem=recv_sems.at[k], device_id=to, device_id_type=MESH)

        mine = pltpu.make_async_copy(x_ref, rows(*me), local_sem)
        mine.start()
        first = [copy(0, me, sibling, src=x_ref)]
        first += [copy(1 + j, me, (*chip, c), src=x_ref) for j, chip in enumerate(chips)]
        for cp in first:
            cp.start()
        passed = [copy(4 + j, (*chip, c), sibling) for j, chip in enumerate(chips)]
        for j, chip in enumerate(chips):
            copy(1 + j, (*chip, c), me).wait_recv()
            passed[j].start()
        copy(0, sibling, me).wait_recv()
        for j, chip in enumerate(chips):
            copy(4 + j, (*chip, 1 - c), me).wait_recv()
        for cp in first + passed:
            cp.wait_send()
        mine.wait()

    out = pl.pallas_call(
        body, name=name, out_shape=jax.ShapeDtypeStruct((N_DEV * m_per, n_cols), block.dtype),
        in_specs=[VMEM_SPEC], out_specs=VMEM_SPEC,
        scratch_shapes=[pltpu.SemaphoreType.DMA((7,)), pltpu.SemaphoreType.DMA((7,)), pltpu.SemaphoreType.DMA],
        compiler_params=pltpu.CompilerParams(vmem_limit_bytes=VMEM_LIMIT_BYTES),
    )(block)
    return out.reshape(N_DEV, m_per, n_cols)


PACK_ROWS = 16


def _pack_rows(shape, width):
    return -(-math.prod(shape) // (width * PACK_ROWS)) * PACK_ROWS


def _pack(arrays, width):
    rows = []
    for a in arrays:
        flat = a.reshape(-1).astype(F32)
        n_rows = _pack_rows(a.shape, width)
        rows.append(jnp.pad(flat, (0, n_rows * width - flat.shape[0])).reshape(n_rows, width))
    return jnp.concatenate(rows, axis=0)


def _unpack(packed, shapes, width):
    out, r = [], 0
    for shp in shapes:
        size = math.prod(shp)
        n_rows = _pack_rows(shp, width)
        out.append(packed[r:r + n_rows].reshape(-1)[:size].reshape(shp))
        r += n_rows
    return out


def _rope_tables(S, C):
    rows = S // GRID_W
    axis_dim = HEAD_DIM // 2
    row = jnp.broadcast_to(jnp.arange(rows, dtype=F32)[:, None], (rows, GRID_W)).reshape(-1)
    col = jnp.broadcast_to(jnp.arange(GRID_W, dtype=F32)[None, :], (rows, GRID_W)).reshape(-1)
    inv_freq = ROPE_THETA ** (-jnp.arange(0, axis_dim, 2, dtype=F32) / axis_dim)
    ang = jnp.concatenate([row[:, None] * inv_freq, col[:, None] * inv_freq], axis=-1)
    cos, sin = jnp.cos(ang), jnp.sin(ang)
    cos2 = jnp.repeat(cos, 2, axis=-1)
    sin2 = jnp.stack([-sin, sin], axis=-1).reshape(S, HEAD_DIM)
    cos2 = jnp.concatenate([cos2, jnp.ones((C, HEAD_DIM), F32)], axis=0)
    sin2 = jnp.concatenate([sin2, jnp.zeros((C, HEAD_DIM), F32)], axis=0)
    return cos2, sin2


def kernel(x, c, ctx, c_ctx, w_mod, b_mod, norm_w, w_ffn1_in, w_ffn1_out, w_ffn2_in, w_ffn2_out, w_in, b_gate, q_norm_w, k_norm_w, gmlp_ln_w, gmlp_ln_b, w_spatial, b_spatial, w_branch_attn, w_branch_gmlp, w_out, final_norm_w, loss_target, m_c_ctx, m_w_mod, m_b_mod, m_norm_w, m_w_ffn1_in, m_w_ffn1_out, m_w_ffn2_in, m_w_ffn2_out, m_w_in, m_b_gate, m_q_norm_w, m_k_norm_w, m_gmlp_ln_w, m_gmlp_ln_b, m_w_spatial, m_b_spatial, m_w_branch_attn, m_w_branch_gmlp, m_w_out, m_final_norm_w, v_c_ctx, v_w_mod, v_b_mod, v_norm_w, v_w_ffn1_in, v_w_ffn1_out, v_w_ffn2_in, v_w_ffn2_out, v_w_in, v_b_gate, v_q_norm_w, v_k_norm_w, v_gmlp_ln_w, v_gmlp_ln_b, v_w_spatial, v_b_spatial, v_w_branch_attn, v_w_branch_gmlp, v_w_out, v_final_norm_w):
    _, S, D = x.shape
    C = ctx.shape[1]
    NTOK = S + C
    n_xt = S // ROW_TILE
    q_w, kv_w = N_Q_HEADS * HEAD_DIM, N_KV_HEADS * HEAD_DIM
    W = gmlp_ln_w.shape[1]
    v_end = q_w + 2 * kv_w
    gv_end = v_end + 2 * W
    scale = HEAD_DIM ** -0.5
    dev = 4 * lax.axis_index("x") + 2 * lax.axis_index("y") + lax.axis_index("c")
    shard = 2 * lax.axis_index("x") + lax.axis_index("y")

    c_all = _allgather_devices(jnp.pad(c, ((0, 7), (0, 0))), "gather_c")[:, 0, :]
    cvecs = jnp.concatenate([c_all, jnp.pad(c_ctx[None, :], ((0, 7), (0, 0)))], axis=0)
    w_mod_l = w_mod[0]
    n_modsh = w_mod_l.shape[1]
    mod_part = _mod_fwd(cvecs, w_mod_l, "mod_fwd")
    mod_all = _allgather_devices(mod_part, "gather_mod")[0::2]
    mod_full = jnp.transpose(mod_all, (1, 0, 2)).reshape(16, N_SHARDS * n_modsh) + b_mod
    mx = lax.dynamic_index_in_dim(mod_full, dev, 0, keepdims=False).reshape(N_MOD, D)
    mc = mod_full[8].reshape(N_MOD, D)
    mods = jnp.concatenate([mx, mc], axis=0).reshape(2 * N_MOD, 1, D)

    shard_of = lambda w: w[0].astype(BF16)
    rows_of = lambda g: g.reshape(-1, g.shape[-1])
    (wf1i,) = _comm_call(_gather_rider([shard_of(w_ffn1_in)]), "gather_ffn1_in")
    n_vsh = norm_w.shape[-1]
    nw_all = _allgather_devices(_pack([norm_w[0], b_gate[0]], n_vsh), "gather_vecs")[0::2]
    nw = jnp.transpose(nw_all[:, 0:3, :], (1, 0, 2)).reshape(3, D)
    bg = jnp.transpose(nw_all[:, PACK_ROWS:PACK_ROWS + 2, :], (1, 0, 2)).reshape(2, 1, D)
    nw0, nw1, nw2 = nw[0:1], nw[1:2], nw[2:3]

    cos2, sin2 = _rope_tables(S, C)
    b_sb = jnp.broadcast_to(b_spatial[0][:, :, None], (GMLP_GROUPS, CHUNK, W // GMLP_GROUPS))
    w_s = w_spatial[0]
    fw = final_norm_w[None, :]

    tok0 = jnp.concatenate([x[0], ctx[0]], axis=0)
    h1 = _normmod(tok0, mods, nw0, k_shift=0, k_scale=1, n_xt=n_xt, name="ffn1_norm")
    z1, g1, (wf1o_g, win) = _ffn_in(h1, wf1i, name="ffn1_in", tm=384,
                                    rider=_gather_rider([shard_of(w_ffn1_out), shard_of(w_in)]))
    wf1o = rows_of(wf1o_g)
    y1 = _matmul(g1, wf1o, form="nn", name="ffn1_out", tn=1024)
    tok1, h2 = _normmod(tok0, mods, nw1, k_shift=3, k_scale=4, n_xt=n_xt, name="mix_norm",
                        resid=(y1, 2, MACARON_WEIGHT))
    later = [w_branch_attn, w_branch_gmlp, w_ffn2_in]
    zx, (wba_g, wbg_g, wf2i) = _matmul(h2, win, form="nn", b_shards=N_SHARDS, name="mix_in",
                                       rider=_gather_rider([shard_of(w) for w in later]))
    wba, wbg = rows_of(wba_g), rows_of(wbg_g)
    qt, kt, vt = _qk_prep(zx, cos2, sin2, q_norm_w, k_norm_w, q_w=q_w, kv_w=kv_w, scale=scale, name="qk_prep")
    attn, lse = _flash_fwd(qt, kt, vt, S=S, name="attn_fwd")
    gm = _gmlp_fwd(zx, gmlp_ln_w, gmlp_ln_b, w_s, b_sb, S=S, col0=v_end, name="gmlp_fwd")
    a_br, (wo_g,) = _matmul(attn, wba, form="nn", name="branch_attn", tm=512, rider=_gather_rider([shard_of(w_out)]))
    wo = rows_of(wo_g)
    g_br = _matmul(gm, wbg, form="nn", name="branch_gmlp", tm=512)
    t_mix = _merge_fwd(zx, a_br, g_br, bg, S=S, col0=gv_end, name="merge_fwd")
    y_mix = _matmul(t_mix, wo, form="nn", name="mix_out", tm=512)
    x2, h3 = _normmod(tok1, mods, nw2, k_shift=6, k_scale=7, n_xt=n_xt, name="ffn2_norm", resid=(y_mix, 5, None),
                      rows=S)
    z2, g2, (wf2o_g,) = _ffn_in(h3, wf2i, name="ffn2_in", tm=512, rider=_gather_rider([shard_of(w_ffn2_out)]))
    wf2o = rows_of(wf2o_g)
    y2 = _matmul(g2, wf2o, form="nn", name="ffn2_out", tn=1024)
    x3 = _resid_only(x2, y2, mods, k_gate=8, name="ffn2_resid")

    dx3, loss_row, dfw = _final_loss(x3, loss_target[0], fw, "loss")

    dy2, dgate8 = _resid_bwd(dx3, y2, mods, k_gate=8, weight=MACARON_WEIGHT, n_xt=n_xt, name="ffn2_resid_bwd")
    dz2 = _matmul(dy2, wf2o, form="nt", out_dtype=BF16, swiglu_z=z2, tm=512, name="ffn2_out_dgrad")
    gw_f2o = _matmul(g2, dy2, form="tn", out_dtype=BF16, name="ffn2_out_wgrad", tn=1024)
    dh3 = _matmul(dz2, wf2i, form="nt", a_shards=2, b_shards=N_SHARDS, name="ffn2_in_dgrad", tn=1024)
    gw_f2i = _matmul(h3, dz2, form="tn", out_dtype=BF16, b_shards=2, out_shards=N_SHARDS, name="ffn2_in_wgrad")
    dx2, dsh6, dsc7, dnw2 = _normmod_bwd(dh3, x2, dx3, mods, nw2, k_scale=7, n_xt=n_xt, name="ffn2_norm_bwd")

    dy_mix, dgate5 = _resid_bwd(dx2, y_mix, mods, k_gate=5, weight=None, n_xt=n_xt, name="mix_resid_bwd")
    dt = _matmul(dy_mix, wo, form="nt", name="mix_out_dgrad", tm=512)
    gw_wo = _matmul(t_mix, dy_mix, form="tn", out_dtype=BF16, name="mix_out_wgrad", tn=1024)
    d_abr, d_gbr, dl0, dl1, dbg0, dbg1 = _merge_bwd(dt, zx, a_br, g_br, bg, S=S, col0=gv_end, name="merge_bwd")
    d_attn = _matmul(d_abr, wba, form="nt", name="branch_attn_dgrad", tm=512)
    gw_wba = _matmul(attn, d_abr, form="tn", out_dtype=BF16, name="branch_attn_wgrad", tn=1024, tk=1024)
    d_gm = _matmul(d_gbr, wbg, form="nt", name="branch_gmlp_dgrad", tm=512)
    gw_wbg = _matmul(gm, d_gbr, form="tn", out_dtype=BF16, name="branch_gmlp_wgrad", tn=1024)
    dz_gm, dws, dbs_wide, dlnw, dlnb = _gmlp_bwd(zx, d_gm, gmlp_ln_w, gmlp_ln_b, w_s, b_sb, S=S, col0=v_end,
                                                 name="gmlp_bwd")
    delta = _attn_delta(d_attn, attn, G=N_KV_HEADS, name="attn_delta")
    dq, dk, dv = _flash_bwd(qt, kt, vt, d_attn, lse, delta, S=S, name="attn_bwd")
    dz_qkv, dqw, dkw = _qk_prep_bwd(zx, dq, dk, dv, cos2, sin2, q_norm_w, k_norm_w, q_w=q_w, kv_w=kv_w,
                                    scale=scale, n_xt=n_xt, name="qk_prep_bwd")
    ctx_pad = ((0, C), (0, 0))
    dzx = jnp.concatenate([dz_qkv, jnp.pad(dz_gm, ctx_pad), jnp.pad(dl0, ctx_pad), jnp.pad(dl1, ctx_pad)], axis=1)
    slabs_of = lambda g: g.reshape(N_SHARDS, g.shape[0] // N_SHARDS, g.shape[1])
    dh2, (rc_f2i,) = _matmul(dzx, win, form="nt", b_shards=N_SHARDS, name="mix_in_dgrad", tn=1024,
                             rider=_scatter_rider([gw_f2i]))
    gw_win, (rc_f2o, rc_wo, rc_wba, rc_wbg) = _matmul(
        h2, dzx, form="tn", out_dtype=BF16, out_shards=N_SHARDS, name="mix_in_wgrad",
        rider=_scatter_rider([slabs_of(g) for g in (gw_f2o, gw_wo, gw_wba, gw_wbg)]))
    dtok1, dsh3, dsc4, dnw1 = _normmod_bwd(dh2, tok1, dx2, mods, nw1, k_scale=4, n_xt=n_xt, name="mix_norm_bwd")

    dy1, dgate2 = _resid_bwd(dtok1, y1, mods, k_gate=2, weight=MACARON_WEIGHT, n_xt=n_xt, name="ffn1_resid_bwd")
    gw_f1o = _matmul(g1, dy1, form="tn", out_dtype=BF16, name="ffn1_out_wgrad", tn=1024)
    dz1, (rc_f1o,) = _matmul(dy1, wf1o, form="nt", out_dtype=BF16, swiglu_z=z1, tm=384, name="ffn1_out_dgrad",
                             rider=_scatter_rider([slabs_of(gw_f1o)]))
    gw_f1i, (rc_win,) = _matmul(h1, dz1, form="tn", out_dtype=BF16, b_shards=2, out_shards=N_SHARDS, name="ffn1_in_wgrad",
                                rider=_scatter_rider([gw_win]))
    dh1, (rc_f1i,) = _matmul(dz1, wf1i, form="nt", a_shards=2, b_shards=N_SHARDS, name="ffn1_in_dgrad", tn=1024,
                             rider=_scatter_rider([gw_f1i]))
    dtok0, dsh0, dsc1, dnw0 = _normmod_bwd(dh1, tok0, dtok1, mods, nw0, k_scale=1, n_xt=n_xt, name="ffn1_norm_bwd")
    grad_x = dtok0[:S][None]

    big_w = [w_ffn1_in, w_ffn2_in, w_in, w_ffn1_out, w_ffn2_out, w_branch_attn, w_branch_gmlp, w_out]
    big_m = [m_w_ffn1_in, m_w_ffn2_in, m_w_in, m_w_ffn1_out, m_w_ffn2_out, m_w_branch_attn, m_w_branch_gmlp, m_w_out]
    big_v = [v_w_ffn1_in, v_w_ffn2_in, v_w_in, v_w_ffn1_out, v_w_ffn2_out, v_w_branch_attn, v_w_branch_gmlp, v_w_out]
    big_names = ["w_ffn1_in", "w_ffn2_in", "w_in", "w_ffn1_out", "w_ffn2_out", "w_branch_attn", "w_branch_gmlp", "w_out"]
    received = [rc_f1i, rc_f2i, rc_win, rc_f1o, rc_f2o, rc_wba, rc_wbg, rc_wo]
    plane_sums = [_sum_slabs(r, "plane_sum_" + nm) for r, nm in zip(received, big_names)]
    sibling_sums = _sibling_exchange(plane_sums, "sibling_grads")
    big_out = {}
    for nm, w_, m_, v_, pa, pb in zip(big_names, big_w, big_m, big_v, plane_sums, sibling_sums):
        res = _adamw(w_[0], m_[0], v_[0], [pa, pb], "adamw_" + nm)
        big_out[nm] = [r[None] for r in res]

    zeros9 = jnp.zeros((1, D), F32)
    dmx = jnp.concatenate([dsh0[0], dsc1[0], dgate2[0], dsh3[0], dsc4[0], dgate5[0], dsh6[0], dsc7[0], dgate8[0]], axis=0)
    dmc = jnp.concatenate([dsh0[1], dsc1[1], dgate2[1], dsh3[1], dsc4[1], zeros9, zeros9, zeros9, zeros9], axis=0)
    dbs = jnp.sum(dbs_wide, axis=-1)
    parts = [dmx, dmc, jnp.concatenate([dnw0, dnw1, dnw2], axis=0), jnp.concatenate([dbg0, dbg1], axis=0),
             dqw, dkw, dlnw, dlnb, dws, dbs, dfw, loss_row[:, :1]]
    part_shapes = [p.shape for p in parts]
    small_all = _allgather_devices(_pack(parts, D), "gather_small")
    dmx_all = small_all[:, 0:N_MOD, :].reshape(N_DEV, N_MOD * D)
    small_sum = small_all[0]
    for d in range(1, N_DEV):
        small_sum = small_sum + small_all[d]
    (_, dmc_sum, g_nw, g_bg, g_qw, g_kw, g_lnw, g_lnb, g_ws, g_bs, g_fw, loss_sum) = _unpack(small_sum, part_shapes, D)
    loss = loss_sum[0, 0]
    dmx_sum = small_sum[0:N_MOD].reshape(1, N_MOD * D)
    g_b_mod = dmx_sum + dmc_sum.reshape(1, N_MOD * D)
    dm_rows = jnp.concatenate([dmx_all, jnp.pad(dmc_sum.reshape(1, N_MOD * D), ((0, 7), (0, 0)))], axis=0)
    dm_sh = lax.dynamic_slice_in_dim(dm_rows, shard * n_modsh, n_modsh, axis=1)
    g_w_mod = _mod_wgrad(cvecs, dm_sh, "mod_wgrad")
    dsc_part = _mod_dgrad(dm_sh, w_mod_l, "mod_dgrad")
    dsc_all = _allgather_devices(dsc_part, "gather_dsilu")[0::2, 8, :]
    dscc = ((dsc_all[0] + dsc_all[1]) + dsc_all[2]) + dsc_all[3]
    sg = _sigmoid(c_ctx)
    g_c_ctx = dscc * (sg * (1.0 + c_ctx * (1.0 - sg)))
    g_norm_w = lax.dynamic_slice_in_dim(g_nw, shard * n_vsh, n_vsh, axis=1)[None]
    g_b_gate = lax.dynamic_slice_in_dim(g_bg, shard * n_vsh, n_vsh, axis=1)[None]

    w_mod_out = [r[None] for r in _adamw(w_mod_l, m_w_mod[0], v_w_mod[0], [g_w_mod], "adamw_w_mod")]

    small_names = ["c_ctx", "b_mod", "norm_w", "b_gate", "q_norm_w", "k_norm_w", "gmlp_ln_w", "gmlp_ln_b",
                   "w_spatial", "b_spatial", "final_norm_w"]
    small_w = [c_ctx, b_mod, norm_w, b_gate, q_norm_w, k_norm_w, gmlp_ln_w, gmlp_ln_b, w_spatial, b_spatial, final_norm_w]
    small_m = [m_c_ctx, m_b_mod, m_norm_w, m_b_gate, m_q_norm_w, m_k_norm_w, m_gmlp_ln_w, m_gmlp_ln_b, m_w_spatial,
               m_b_spatial, m_final_norm_w]
    small_v = [v_c_ctx, v_b_mod, v_norm_w, v_b_gate, v_q_norm_w, v_k_norm_w, v_gmlp_ln_w, v_gmlp_ln_b, v_w_spatial,
               v_b_spatial, v_final_norm_w]
    small_g = [g_c_ctx, g_b_mod, g_norm_w, g_b_gate, g_qw, g_kw, g_lnw, g_lnb, g_ws, g_bs, g_fw]
    small_shapes = [w_.shape for w_ in small_w]
    packed = [_pack(group, D) for group in (small_w, small_m, small_v, small_g)]
    small_res = [_unpack(r, small_shapes, D) for r in _adamw(*packed[:3], [packed[3]], "adamw_small")]
    small_out = {nm: [small_res[t][i] for t in range(4)] for i, nm in enumerate(small_names)}

    order = ["c_ctx", "w_mod", "b_mod", "norm_w", "w_ffn1_in", "w_ffn1_out", "w_ffn2_in", "w_ffn2_out", "w_in", "b_gate",
             "q_norm_w", "k_norm_w", "gmlp_ln_w", "gmlp_ln_b", "w_spatial", "b_spatial", "w_branch_attn", "w_branch_gmlp",
             "w_out", "final_norm_w"]
    results = {**big_out, **small_out, "w_mod": w_mod_out}
    outs = [loss, grad_x]
    for t in range(4):
        outs += [results[nm][t] for nm in order]
    return tuple(outs)


def _resid_only(xp, y_in, mods, *, k_gate, name):
    M, D = xp.shape
    row = pl.BlockSpec((ROW_TILE, D), lambda i: (i, 0))

    def body(xp_ref, y_ref, g_ref, x_ref):
        x_ref[...] = xp_ref[...] + (MACARON_WEIGHT * g_ref[...]) * y_ref[...]

    return pl.pallas_call(
        body, name=name, grid=(M // ROW_TILE,), out_shape=jax.ShapeDtypeStruct((M, D), F32),
        in_specs=[row, row, pl.BlockSpec((None, 1, D), lambda i: (k_gate, 0, 0))], out_specs=row,
        compiler_params=_params("parallel"),
    )(xp, y_in, mods)
```
